```python
import math
import jax, jax.numpy as jnp
from jax import lax
import numpy as np

D_MODEL = 2048
BATCH = 4
SEQ = 2048
DEPTH = 4
DEC_BATCH = 8
DEC_SEQ = 1
PAST_LEN = 16384
PAGE_SIZE = 128

N_MIXERS = 3
SC_WIDTH = 3
N_HEADS = 16
HEAD_DIM = D_MODEL // N_HEADS
ROT_DIM = HEAD_DIM // 4
ROPE_THETA = 500000.0
MOBA_BLOCK = 256
MOBA_TOPK = 3
MOBA_Q_CHUNK = 8
D_RNN = D_MODEL
RNN_HEADS = 8
RNN_BLOCK = D_RNN // RNN_HEADS
RNN_CONV = 4
LRU_C = 8.0
D_FF = 7168
N_EXPERTS = 8
TOP_K = 2
LN_EPS = 1e-5
DEEPNORM_ALPHA = (2.0 * DEPTH) ** 0.25
DEEPNORM_BETA = (8.0 * DEPTH) ** -0.25
NEG_INF = -1e30

N_CONV_LAYERS = len(range(0, DEPTH, N_MIXERS))
N_MOBA_LAYERS = len(range(1, DEPTH, N_MIXERS))
N_LRU_LAYERS = len(range(2, DEPTH, N_MIXERS))
N_DENSE_LAYERS = len(range(0, DEPTH, 2))
N_MOE_LAYERS = len(range(1, DEPTH, 2))

kernel_name = 'hybrid_conv_moba_rglru_deepnorm_step'


def layer_norm(x, g, b):
    xf = x.astype(jnp.float32)
    mu = jnp.mean(xf, axis=-1, keepdims=True)
    var = jnp.mean(jnp.square(xf - mu), axis=-1, keepdims=True)
    y = (xf - mu) * lax.rsqrt(var + LN_EPS) * g.astype(jnp.float32) + b.astype(jnp.float32)
    return y.astype(x.dtype)


def linear(x, w):
    return jnp.einsum('btd,de->bte', x, w)


def causal_dwconv(u, buf, w):
    width, t = w.shape[0], u.shape[1]
    ext = jnp.concatenate([buf.astype(u.dtype), u], axis=1)
    y = ext[:, 0:t] * w[0]
    for j in range(1, width):
        y = y + ext[:, j:j + t] * w[j]
    return y, ext[:, ext.shape[1] - (width - 1):]


def short_conv_mixer(x, buf, w_in, w_conv, w_out):
    b_gate, c_gate, h = jnp.split(linear(x, w_in), 3, axis=-1)
    y, new_buf = causal_dwconv(c_gate * h, buf, w_conv)
    return linear(b_gate * y, w_out), new_buf


def partial_rope(x, pos):
    half = ROT_DIM // 2
    inv_freq = ROPE_THETA ** (-jnp.arange(half, dtype=jnp.float32) * 2.0 / ROT_DIM)
    ang = pos.astype(jnp.float32)[:, None] * inv_freq[None, :]
    cos = jnp.cos(ang)[None, :, None, :]
    sin = jnp.sin(ang)[None, :, None, :]
    xr = x[..., :ROT_DIM].astype(jnp.float32)
    x1, x2 = xr[..., :half], xr[..., half:]
    rot = jnp.concatenate([x1 * cos - x2 * sin, x2 * cos + x1 * sin], axis=-1)
    return jnp.concatenate([rot.astype(x.dtype), x[..., ROT_DIM:]], axis=-1)


def moba_attention(q, kb, vb, q_pos):
    bsz, tq = q.shape[0], q.shape[1]
    nb = kb.shape[1]
    k_sel = min(MOBA_TOPK, nb)
    qc = math.gcd(tq, MOBA_Q_CHUNK)
    n_chunks = tq // qc
    kmean = jnp.mean(kb, axis=2, dtype=jnp.float32)
    q_ch = q.reshape(bsz, n_chunks, qc, N_HEADS, HEAD_DIM).transpose(1, 0, 3, 2, 4)
    pos_ch = q_pos.reshape(n_chunks, qc)
    b_idx = jnp.arange(bsz)[:, None, None, None]
    h_idx = jnp.arange(N_HEADS)[None, :, None, None]
    offs = jnp.arange(MOBA_BLOCK, dtype=jnp.int32)
    blk_ids = jnp.arange(nb, dtype=jnp.int32)
    scale = HEAD_DIM ** -0.5

    def attend_chunk(args):
        qq, pp = args
        own = pp // MOBA_BLOCK
        gate = jnp.einsum('bhqd,bnhd->bhqn', qq.astype(jnp.float32), kmean)
        gate = jnp.where(blk_ids[None, :] < own[:, None], gate, NEG_INF)
        _, sel = lax.top_k(gate, k_sel)
        sel = sel.astype(jnp.int32)
        own_b = jnp.broadcast_to(own[:, None], sel.shape[:-1] + (1,))
        blocks = jnp.concatenate([sel, own_b], axis=-1)
        blk_ok = jnp.concatenate([sel < own[:, None], jnp.ones(own_b.shape, bool)], axis=-1)
        kg = kb[b_idx, blocks, :, h_idx]
        vg = vb[b_idx, blocks, :, h_idx]
        s = jnp.einsum('bhqd,bhqnkd->bhqnk', qq, kg).astype(jnp.float32) * scale
        key_pos = blocks[..., None] * MOBA_BLOCK + offs
        valid = blk_ok[..., None] & (key_pos <= pp[:, None, None])
        s = jnp.where(valid, s, NEG_INF).reshape(bsz, N_HEADS, qc, -1)
        p = jax.nn.softmax(s, axis=-1).astype(vg.dtype)
        vg = vg.reshape(bsz, N_HEADS, qc, -1, HEAD_DIM)
        return jnp.einsum('bhqk,bhqkd->bhqd', p, vg)

    out = lax.map(attend_chunk, (q_ch, pos_ch))
    return out.transpose(1, 0, 3, 2, 4).reshape(bsz, tq, N_HEADS * HEAD_DIM)


def moba_mixer(x, k_past, v_past, pos0, w_qkv, w_o):
    bsz, t, _ = x.shape
    qkv = linear(x, w_qkv).reshape(bsz, t, 3, N_HEADS, HEAD_DIM)
    pos = pos0 + jnp.arange(t, dtype=jnp.int32)
    q = partial_rope(qkv[:, :, 0], pos)
    k = partial_rope(qkv[:, :, 1], pos)
    v = qkv[:, :, 2]
    total = pos0 + t
    nb = -(-total // MOBA_BLOCK)
    pad = jnp.zeros((bsz, nb * MOBA_BLOCK - total, N_HEADS, HEAD_DIM), k.dtype)
    if k_past is None:
        k_all = jnp.concatenate([k, pad], axis=1)
        v_all = jnp.concatenate([v, pad], axis=1)
    else:
        k_all = jnp.concatenate([k_past.astype(k.dtype), k, pad], axis=1)
        v_all = jnp.concatenate([v_past.astype(v.dtype), v, pad], axis=1)
    kb = k_all.reshape(bsz, nb, MOBA_BLOCK, N_HEADS, HEAD_DIM)
    vb = v_all.reshape(bsz, nb, MOBA_BLOCK, N_HEADS, HEAD_DIM)
    o = moba_attention(q, kb, vb, pos)
    return linear(o, w_o), k, v


def rglru_mixer(x, conv_buf, h0, w_in, w_conv, b_conv, w_gate_a, b_gate_a, w_gate_x, b_gate_x, lam, w_out):
    bsz, t, _ = x.shape
    g, u = jnp.split(linear(x, w_in), 2, axis=-1)
    uc, new_buf = causal_dwconv(u, conv_buf, w_conv)
    uc = uc + b_conv
    uh = uc.reshape(bsz, t, RNN_HEADS, RNN_BLOCK)
    r = jax.nn.sigmoid(jnp.einsum('bthi,hij->bthj', uh, w_gate_a) + b_gate_a).reshape(bsz, t, D_RNN)
    i = jax.nn.sigmoid(jnp.einsum('bthi,hij->bthj', uh, w_gate_x) + b_gate_x).reshape(bsz, t, D_RNN)
    log_a = -LRU_C * r.astype(jnp.float32) * jax.nn.softplus(-lam.astype(jnp.float32))
    a = jnp.exp(log_a)
    b = jnp.sqrt(-jnp.expm1(2.0 * log_a)) * (i * uc).astype(jnp.float32)

    def step(h, ab):
        h = ab[0] * h + ab[1]
        return h, h

    h_last, hs = lax.scan(step, h0.astype(jnp.float32), (a.transpose(1, 0, 2), b.transpose(1, 0, 2)))
    y = hs.transpose(1, 0, 2).astype(x.dtype) * jax.nn.gelu(g, approximate=True)
    return linear(y, w_out), new_buf, h_last


def swiglu(x, wg, wu, wd):
    return linear(jax.nn.silu(linear(x, wg)) * linear(x, wu), wd)


def moe_swiglu(x, w_router, wg, wu, wd):
    logits = linear(x, w_router).astype(jnp.float32)
    top_logit, top_idx = lax.top_k(logits, TOP_K)
    gates = jax.nn.softmax(top_logit, axis=-1)
    comb = jnp.einsum('btk,btke->bte', gates, jax.nn.one_hot(top_idx, N_EXPERTS, dtype=jnp.float32)).astype(x.dtype)
    y = jnp.zeros_like(x)
    for e in range(N_EXPERTS):
        y = y + comb[..., e:e + 1] * swiglu(x, wg[e], wu[e], wd[e])
    return y


def setup_inputs(seed: int = 0) -> dict:
    key = jax.random.key(seed)
    ks = iter(jax.random.split(key, 40))

    def nrm(shape, scale):
        return jax.random.normal(next(ks), shape, jnp.float32) * scale

    n_pages = PAST_LEN // PAGE_SIZE
    n_phys = (5 * DEC_BATCH * n_pages) // 4
    inp = {}
    inp['x_prompt'] = nrm((BATCH, SEQ, D_MODEL), 1.0)
    inp['x_sample'] = nrm((DEC_BATCH, DEC_SEQ, D_MODEL), 1.0)
    inp['cache_k'] = nrm((N_MOBA_LAYERS, n_phys, PAGE_SIZE, N_HEADS, HEAD_DIM), 1.0)
    inp['cache_v'] = nrm((N_MOBA_LAYERS, n_phys, PAGE_SIZE, N_HEADS, HEAD_DIM), 1.0)
    perm = jax.random.permutation(next(ks), n_phys)
    inp['page_table'] = perm[:DEC_BATCH * n_pages].reshape(DEC_BATCH, n_pages).astype(jnp.int32)
    inp['state_conv'] = nrm((N_CONV_LAYERS, DEC_BATCH, SC_WIDTH - 1, D_MODEL), 1.0)
    inp['state_lru_conv'] = nrm((N_LRU_LAYERS, DEC_BATCH, RNN_CONV - 1, D_RNN), 1.0)
    inp['state_lru_h'] = nrm((N_LRU_LAYERS, DEC_BATCH, D_RNN), 0.5)
    inp['ln1_g'] = 1.0 + nrm((DEPTH, D_MODEL), 0.02)
    inp['ln1_b'] = nrm((DEPTH, D_MODEL), 0.02)
    inp['ln2_g'] = 1.0 + nrm((DEPTH, D_MODEL), 0.02)
    inp['ln2_b'] = nrm((DEPTH, D_MODEL), 0.02)
    inp['sc_w_in'] = nrm((N_CONV_LAYERS, D_MODEL, 3 * D_MODEL), D_MODEL ** -0.5)
    inp['sc_w_conv'] = nrm((N_CONV_LAYERS, SC_WIDTH, D_MODEL), SC_WIDTH ** -0.5)
    inp['sc_w_out'] = nrm((N_CONV_LAYERS, D_MODEL, D_MODEL), DEEPNORM_BETA * D_MODEL ** -0.5)
    inp['attn_w_qkv'] = nrm((N_MOBA_LAYERS, D_MODEL, 3 * N_HEADS * HEAD_DIM), D_MODEL ** -0.5)
    inp['attn_w_o'] = nrm((N_MOBA_LAYERS, N_HEADS * HEAD_DIM, D_MODEL), DEEPNORM_BETA * D_MODEL ** -0.5)
    inp['lru_w_in'] = nrm((N_LRU_LAYERS, D_MODEL, 2 * D_RNN), D_MODEL ** -0.5)
    inp['lru_w_conv'] = nrm((N_LRU_LAYERS, RNN_CONV, D_RNN), RNN_CONV ** -0.5)
    inp['lru_b_conv'] = nrm((N_LRU_LAYERS, D_RNN), 0.01)
    inp['lru_w_gate_a'] = nrm((N_LRU_LAYERS, RNN_HEADS, RNN_BLOCK, RNN_BLOCK), RNN_BLOCK ** -0.5)
    inp['lru_b_gate_a'] = nrm((N_LRU_LAYERS, RNN_HEADS, RNN_BLOCK), 0.01)
    inp['lru_w_gate_x'] = nrm((N_LRU_LAYERS, RNN_HEADS, RNN_BLOCK, RNN_BLOCK), RNN_BLOCK ** -0.5)
    inp['lru_b_gate_x'] = nrm((N_LRU_LAYERS, RNN_HEADS, RNN_BLOCK), 0.01)
    u = jax.random.uniform(next(ks), (N_LRU_LAYERS, D_RNN), jnp.float32, minval=0.9, maxval=0.999)
    base = u ** (1.0 / LRU_C)
    inp['lru_lambda'] = jnp.log(base) - jnp.log1p(-base)
    inp['lru_w_out'] = nrm((N_LRU_LAYERS, D_RNN, D_MODEL), DEEPNORM_BETA * D_RNN ** -0.5)
    inp['ffn_w_gate'] = nrm((N_DENSE_LAYERS, D_MODEL, D_FF), D_MODEL ** -0.5)
    inp['ffn_w_up'] = nrm((N_DENSE_LAYERS, D_MODEL, D_FF), D_MODEL ** -0.5)
    inp['ffn_w_down'] = nrm((N_DENSE_LAYERS, D_FF, D_MODEL), DEEPNORM_BETA * D_FF ** -0.5)
    inp['moe_w_router'] = nrm((N_MOE_LAYERS, D_MODEL, N_EXPERTS), D_MODEL ** -0.5)
    inp['moe_w_gate'] = nrm((N_MOE_LAYERS, N_EXPERTS, D_MODEL, D_FF), D_MODEL ** -0.5)
    inp['moe_w_up'] = nrm((N_MOE_LAYERS, N_EXPERTS, D_MODEL, D_FF), D_MODEL ** -0.5)
    inp['moe_w_down'] = nrm((N_MOE_LAYERS, N_EXPERTS, D_FF, D_MODEL), DEEPNORM_BETA * D_FF ** -0.5)
    return inp


def reference(x_prompt, x_sample, cache_k, cache_v, page_table, state_conv, state_lru_conv, state_lru_h,
              ln1_g, ln1_b, ln2_g, ln2_b, sc_w_in, sc_w_conv, sc_w_out, attn_w_qkv, attn_w_o,
              lru_w_in, lru_w_conv, lru_b_conv, lru_w_gate_a, lru_b_gate_a, lru_w_gate_x, lru_b_gate_x,
              lru_lambda, lru_w_out, ffn_w_gate, ffn_w_up, ffn_w_down,
              moe_w_router, moe_w_gate, moe_w_up, moe_w_down):
    bp, bs = x_prompt.shape[0], x_sample.shape[0]
    past_len = page_table.shape[1] * PAGE_SIZE
    xp, xs = x_prompt, x_sample
    conv_p, conv_s = [], []
    kp_l, vp_l, ks_l, vs_l = [], [], [], []
    lc_p, lc_s, lh_p, lh_s = [], [], [], []
    for i in range(DEPTH):
        kind = i % N_MIXERS
        m = i // N_MIXERS
        if kind == 0:
            w = (sc_w_in[m], sc_w_conv[m], sc_w_out[m])
            yp, st = short_conv_mixer(xp, jnp.zeros((bp, SC_WIDTH - 1, D_MODEL), xp.dtype), *w)
            conv_p.append(st)
            ys, st = short_conv_mixer(xs, state_conv[m], *w)
            conv_s.append(st)
        elif kind == 1:
            k_past = cache_k[m, page_table].reshape(bs, past_len, N_HEADS, HEAD_DIM)
            v_past = cache_v[m, page_table].reshape(bs, past_len, N_HEADS, HEAD_DIM)
            yp, k_new, v_new = moba_mixer(xp, None, None, 0, attn_w_qkv[m], attn_w_o[m])
            kp_l.append(k_new)
            vp_l.append(v_new)
            ys, k_new, v_new = moba_mixer(xs, k_past, v_past, past_len, attn_w_qkv[m], attn_w_o[m])
            ks_l.append(k_new)
            vs_l.append(v_new)
        else:
            w = (lru_w_in[m], lru_w_conv[m], lru_b_conv[m], lru_w_gate_a[m], lru_b_gate_a[m],
                 lru_w_gate_x[m], lru_b_gate_x[m], lru_lambda[m], lru_w_out[m])
            yp, cb, h = rglru_mixer(xp, jnp.zeros((bp, RNN_CONV - 1, D_RNN), xp.dtype),
                                    jnp.zeros((bp, D_RNN), jnp.float32), *w)
            lc_p.append(cb)
            lh_p.append(h)
            ys, cb, h = rglru_mixer(xs, state_lru_conv[m], state_lru_h[m], *w)
            lc_s.append(cb)
            lh_s.append(h)
        xp = layer_norm(DEEPNORM_ALPHA * xp + yp, ln1_g[i], ln1_b[i])
        xs = layer_norm(DEEPNORM_ALPHA * xs + ys, ln1_g[i], ln1_b[i])
        j = i // 2
        if i % 2 == 0:
            fp = swiglu(xp, ffn_w_gate[j], ffn_w_up[j], ffn_w_down[j])
            fs = swiglu(xs, ffn_w_gate[j], ffn_w_up[j], ffn_w_down[j])
        else:
            fp = moe_swiglu(xp, moe_w_router[j], moe_w_gate[j], moe_w_up[j], moe_w_down[j])
            fs = moe_swiglu(xs, moe_w_router[j], moe_w_gate[j], moe_w_up[j], moe_w_down[j])
        xp = layer_norm(DEEPNORM_ALPHA * xp + fp, ln2_g[i], ln2_b[i])
        xs = layer_norm(DEEPNORM_ALPHA * xs + fs, ln2_g[i], ln2_b[i])
    return (xp, xs,
            jnp.stack(conv_p), jnp.stack(conv_s),
            jnp.stack(kp_l), jnp.stack(vp_l), jnp.stack(ks_l), jnp.stack(vs_l),
            jnp.stack(lc_p), jnp.stack(lc_s), jnp.stack(lh_p), jnp.stack(lh_s))
```

```python
import functools

import numpy as np
import jax
import jax.numpy as jnp
from jax import lax
from jax.experimental import pallas as pl
from jax.experimental.pallas import tpu as pltpu

F32, BF16, I32 = jnp.float32, jnp.bfloat16, jnp.int32

MOBA_BLOCK = 256
MOBA_TOPK = 3
ROPE_THETA = 500000.0
ROT_FRACTION = 4
LRU_C = 8.0
MOE_TOP_K = 2
LN_EPS = 1e-5
NEG_INF = -1e30

LANES = 128
SUBLANES = 8
V7X_VMEM_BYTES = 64 * 1024 * 1024

TOKEN_TILE = 256
FFN_COL_TILE = 256
FFN_SUPER_TILES = 10


def _params(sem, vmem_mib):
    return pltpu.CompilerParams(dimension_semantics=sem, vmem_limit_bytes=vmem_mib * 1024 * 1024)


def _pick(total, candidates):
    for c in candidates:
        if total % c == 0:
            return c
    raise ValueError(f"no tile in {candidates} divides {total}")


def _mm_body(x_ref, w_ref, o_ref, wb_ref):
    @pl.when(pl.program_id(1) == 0)
    def _cast():
        wb_ref[...] = w_ref[...].astype(BF16)

    o_ref[...] = jnp.dot(x_ref[...], wb_ref[...], preferred_element_type=F32).astype(o_ref.dtype)


def _matmul(xb, w_stack, layer, out_dtype=F32):
    m, k = xb.shape
    n = w_stack.shape[2]
    tm = _pick(m, (768, 512, 256))
    tn = _pick(n, (1024, 512, 256, 128))
    return pl.pallas_call(
        _mm_body,
        grid=(n // tn, m // tm),
        in_specs=[pl.BlockSpec((tm, k), lambda j, i: (i, 0)),
                  pl.BlockSpec((None, k, tn), lambda j, i: (layer, 0, j))],
        out_specs=pl.BlockSpec((tm, tn), lambda j, i: (i, j)),
        out_shape=jax.ShapeDtypeStruct((m, n), out_dtype),
        scratch_shapes=[pltpu.VMEM((k, tn), BF16)],
        compiler_params=_params(("arbitrary", "arbitrary"), 48),
        name="matmul",
    )(xb, w_stack)


def _layer_norm(z, g, b):
    mu = jnp.mean(z, axis=-1, keepdims=True)
    zc = z - mu
    var = jnp.mean(zc * zc, axis=-1, keepdims=True)
    return zc * lax.rsqrt(var + LN_EPS) * g + b


def _add_ln_body(x_ref, y_ref, g_ref, b_ref, o_ref, ob_ref, *, alpha):
    out = _layer_norm(alpha * x_ref[...] + y_ref[...], g_ref[...], b_ref[...])
    o_ref[...] = out
    ob_ref[...] = out.astype(BF16)


def _add_ln(x, y, g_stack, b_stack, layer, alpha):
    m, d = x.shape
    tm = TOKEN_TILE
    row = pl.BlockSpec((tm, d), lambda i: (i, 0))
    par = pl.BlockSpec((None, 1, d), lambda i: (layer, 0, 0))
    return pl.pallas_call(
        functools.partial(_add_ln_body, alpha=alpha),
        grid=(m // tm,),
        in_specs=[row, row, par, par],
        out_specs=[row, row],
        out_shape=[jax.ShapeDtypeStruct((m, d), F32), jax.ShapeDtypeStruct((m, d), BF16)],
        compiler_params=_params(("arbitrary",), 32),
        name="add_ln",
    )(x, y, g_stack.reshape(-1, 1, d), b_stack.reshape(-1, 1, d))


def _shift_rows(u, k, carry_rows):
    out = pltpu.roll(u, k, 0)
    row = lax.broadcasted_iota(I32, u.shape, 0)
    for r, c in enumerate(carry_rows):
        out = jnp.where(row == r, c, out)
    return out


def _sconv_body(b_ref, c_ref, h_ref, w_ref, s0_ref, s1_ref, y_ref, newp_ref, news_ref, carry_ref,
                *, tiles_per_seq, n_prompt_tiles, bs):
    i = pl.program_id(0)
    w0, w1, w2 = w_ref[0:1, :], w_ref[1:2, :], w_ref[2:3, :]
    tm = y_ref.shape[0]

    @pl.when(i < n_prompt_tiles)
    def _prompt():
        u = c_ref[...] * h_ref[...]
        fresh = i % tiles_per_seq == 0
        c0 = jnp.where(fresh, 0.0, carry_ref[0:1, :])
        c1 = jnp.where(fresh, 0.0, carry_ref[1:2, :])
        p1 = _shift_rows(u, 1, (c1,))
        p2 = _shift_rows(u, 2, (c0, c1))
        y = w0 * p2 + w1 * p1 + w2 * u
        y_ref[...] = (b_ref[...] * y).astype(y_ref.dtype)
        carry_ref[0:2, :] = u[tm - 2:tm, :]
        newp_ref[...] = u[tm - 2:tm, :]

    @pl.when(i == n_prompt_tiles)
    def _sample():
        u = c_ref[0:bs, :] * h_ref[0:bs, :]
        y = w0 * s0_ref[...] + w1 * s1_ref[...] + w2 * u
        y_ref[...] = jnp.zeros(y_ref.shape, y_ref.dtype)
        y_ref[0:bs, :] = (b_ref[0:bs, :] * y).astype(y_ref.dtype)
        news_ref[...] = u


def _short_conv(proj, w_conv, layer, s0, s1, *, bp, seq, bs):
    m_tot = proj.shape[0]
    d = proj.shape[1] // 3
    tm = TOKEN_TILE
    tps = seq // tm
    npt = bp * tps
    col = lambda c: pl.BlockSpec((tm, d), lambda i: (i, c))
    full = lambda shape: pl.BlockSpec(shape, lambda i: (0,) * len(shape))
    return pl.pallas_call(
        functools.partial(_sconv_body, tiles_per_seq=tps, n_prompt_tiles=npt, bs=bs),
        grid=(m_tot // tm,),
        in_specs=[col(0), col(1), col(2),
                  pl.BlockSpec((None, w_conv.shape[1], d), lambda i: (layer, 0, 0)),
                  full((bs, d)), full((bs, d))],
        out_specs=[pl.BlockSpec((tm, d), lambda i: (i, 0)),
                   pl.BlockSpec((None, 2, d), lambda i: (jnp.minimum(i // tps, bp - 1), 0, 0)),
                   full((bs, d))],
        out_shape=[jax.ShapeDtypeStruct((m_tot, d), BF16),
                   jax.ShapeDtypeStruct((bp, 2, d), F32),
                   jax.ShapeDtypeStruct((bs, d), F32)],
        scratch_shapes=[pltpu.VMEM((SUBLANES, d), F32)],
        compiler_params=_params(("arbitrary",), 40),
        name="short_conv",
    )(proj, proj, proj, w_conv, s0, s1)


def _rope_head(x, cos_f, sin_f, half):
    lane = lax.broadcasted_iota(I32, x.shape, 1)
    partner = jnp.where(lane < half, pltpu.roll(x, LANES - half, 1), pltpu.roll(x, half, 1))
    return x * cos_f + partner * sin_f


def _rope_body(q_ref, k_ref, v_ref, cos_ref, sin_ref, cos_s_ref, sin_s_ref,
               qo_ref, kp_ref, vp_ref, ks_ref, vs_ref, *, n_heads, hd, half, n_prompt_tiles, bs):
    i = pl.program_id(0)

    def rotate(x_ref, rows, cos_f, sin_f):
        return [
            _rope_head(x_ref[rows, h * hd:(h + 1) * hd], cos_f, sin_f, half) for h in range(n_heads)
        ]

    @pl.when(i < n_prompt_tiles)
    def _prompt():
        rows = slice(None)
        cos_f, sin_f = cos_ref[...], sin_ref[...]
        for h, (qh, kh) in enumerate(zip(rotate(q_ref, rows, cos_f, sin_f), rotate(k_ref, rows, cos_f, sin_f))):
            qo_ref[:, h * hd:(h + 1) * hd] = qh
            kp_ref[:, h * hd:(h + 1) * hd] = kh
        vp_ref[...] = v_ref[...]

    @pl.when(i == n_prompt_tiles)
    def _sample():
        rows = slice(0, bs)
        cos_f, sin_f = cos_s_ref[...], sin_s_ref[...]
        qo_ref[...] = jnp.zeros(qo_ref.shape, qo_ref.dtype)
        for h, (qh, kh) in enumerate(zip(rotate(q_ref, rows, cos_f, sin_f), rotate(k_ref, rows, cos_f, sin_f))):
            qo_ref[0:bs, h * hd:(h + 1) * hd] = qh
            ks_ref[:, h * hd:(h + 1) * hd] = kh
        vs_ref[...] = v_ref[0:bs, :]


def _rope_tables(pos, hd):
    rot = hd // ROT_FRACTION
    half = rot // 2
    inv_freq = ROPE_THETA ** (-jnp.arange(half, dtype=F32) * 2.0 / rot)
    ang = pos.astype(F32)[:, None] * inv_freq[None, :]
    cos, sin = jnp.cos(ang), jnp.sin(ang)
    n = pos.shape[0]
    cos_f = jnp.concatenate([cos, cos, jnp.ones((n, hd - rot), F32)], axis=1)
    sin_f = jnp.concatenate([-sin, sin, jnp.zeros((n, hd - rot), F32)], axis=1)
    return cos_f, sin_f


def _rope(qkv, *, n_heads, hd, bp, seq, bs, past_len):
    m_tot = qkv.shape[0]
    hdim = n_heads * hd
    tm = TOKEN_TILE
    tps = seq // tm
    npt = bp * tps
    mp = bp * seq
    cos_p, sin_p = _rope_tables(jnp.arange(seq, dtype=I32), hd)
    cos_s, sin_s = _rope_tables(jnp.full((1,), past_len, I32), hd)
    col = lambda c: pl.BlockSpec((tm, hdim), lambda i: (i, c))
    tab = pl.BlockSpec((tm, hd), lambda i: (i % tps, 0))
    one = pl.BlockSpec((1, hd), lambda i: (0, 0))
    prow = pl.BlockSpec((tm, hdim), lambda i: (jnp.minimum(i, npt - 1), 0))
    srow = pl.BlockSpec((bs, hdim), lambda i: (0, 0))
    return pl.pallas_call(
        functools.partial(_rope_body, n_heads=n_heads, hd=hd, half=hd // ROT_FRACTION // 2,
                          n_prompt_tiles=npt, bs=bs),
        grid=(m_tot // tm,),
        in_specs=[col(0), col(1), col(2), tab, tab, one, one],
        out_specs=[pl.BlockSpec((tm, hdim), lambda i: (i, 0)), prow, prow, srow, srow],
        out_shape=[jax.ShapeDtypeStruct((m_tot, hdim), F32),
                   jax.ShapeDtypeStruct((mp, hdim), F32), jax.ShapeDtypeStruct((mp, hdim), F32),
                   jax.ShapeDtypeStruct((bs, hdim), F32), jax.ShapeDtypeStruct((bs, hdim), F32)],
        compiler_params=_params(("arbitrary",), 48),
        name="rope",
    )(qkv, qkv, qkv, cos_p, sin_p, cos_s, sin_s)


def _nt_dot(a, b, **kw):
    return lax.dot_general(a, b, (((1,), (1,)), ((), ())), preferred_element_type=F32, **kw)


def _attn_body(q_ref, k_ref, v_ref, o_ref, kmean_ref, m_ref, l_ref, acc_ref, *, nb, scale):
    j = pl.program_id(2)
    blk = MOBA_BLOCK
    nbp = kmean_ref.shape[0]

    @pl.when(j == 0)
    def _block_means():
        kmean_ref[...] = jnp.zeros(kmean_ref.shape, F32)
        for n in range(nb):
            kmean_ref[n:n + 1, :] = jnp.sum(k_ref[n * blk:(n + 1) * blk, :], axis=0, keepdims=True) * (1.0 / blk)

    q = q_ref[...]
    gate = _nt_dot(q, kmean_ref[...], precision=lax.Precision.HIGHEST)
    col = lax.broadcasted_iota(I32, gate.shape, 1)
    past = col < j
    gate = jnp.where(past, gate, NEG_INF)
    rank = jnp.zeros(gate.shape, I32)
    for m in range(nb - 1):
        gm = gate[:, m:m + 1]
        beats = (gm > gate) | ((gm == gate) & (m < col))
        rank = rank + jnp.where(beats & (m < j), 1, 0)
    sel = jnp.where(past & (rank < MOBA_TOPK), 1.0, 0.0)

    qb = q.astype(BF16)
    own = pl.ds(pl.multiple_of(j * blk, blk), blk)
    s = _nt_dot(qb, k_ref[own, :].astype(BF16)) * scale
    qi = lax.broadcasted_iota(I32, s.shape, 0)
    ki = lax.broadcasted_iota(I32, s.shape, 1)
    s = jnp.where(ki <= qi, s, NEG_INF)
    m0 = jnp.max(s, axis=1, keepdims=True)
    p = jnp.exp(s - m0)
    m_ref[...] = m0
    l_ref[...] = jnp.sum(p, axis=1, keepdims=True)
    acc_ref[...] = jnp.dot(p.astype(BF16), v_ref[own, :].astype(BF16), preferred_element_type=F32)

    for n in range(nb - 1):
        @pl.when(n < j)
        def _past(n=n):
            rows = slice(n * blk, (n + 1) * blk)
            sn = _nt_dot(qb, k_ref[rows, :].astype(BF16)) * scale
            sn = jnp.where(sel[:, n:n + 1] > 0.0, sn, NEG_INF)
            m_old = m_ref[...]
            m_new = jnp.maximum(m_old, jnp.max(sn, axis=1, keepdims=True))
            a = jnp.exp(m_old - m_new)
            pn = jnp.exp(sn - m_new)
            m_ref[...] = m_new
            l_ref[...] = a * l_ref[...] + jnp.sum(pn, axis=1, keepdims=True)
            acc_ref[...] = a * acc_ref[...] + jnp.dot(pn.astype(BF16), v_ref[rows, :].astype(BF16),
                                                      preferred_element_type=F32)

    o_ref[...] = (acc_ref[...] / l_ref[...]).astype(o_ref.dtype)


def _attn_prompt(q, k_p, v_p, *, bp, seq, n_heads, hd):
    nb = seq // MOBA_BLOCK
    nbp = -(-nb // SUBLANES) * SUBLANES
    blk = MOBA_BLOCK
    kv = pl.BlockSpec((seq, hd), lambda b, h, j: (b, h))
    qo = pl.BlockSpec((blk, hd), lambda b, h, j: (b * nb + j, h))
    return pl.pallas_call(
        functools.partial(_attn_body, nb=nb, scale=hd ** -0.5),
        grid=(bp, n_heads, nb),
        in_specs=[qo, kv, kv],
        out_specs=qo,
        out_shape=jax.ShapeDtypeStruct((bp * seq, n_heads * hd), BF16),
        scratch_shapes=[pltpu.VMEM((nbp, hd), F32), pltpu.VMEM((blk, 1), F32), pltpu.VMEM((blk, 1), F32),
                        pltpu.VMEM((blk, hd), F32)],
        compiler_params=_params(("arbitrary", "arbitrary", "arbitrary"), 32),
        name="moba_prompt",
    )(q, k_p, v_p)


def _page_mean_body(pt_ref, *refs, pages_per_step, pages_per_block):
    del pt_ref
    pages, o_ref = refs[:pages_per_step], refs[pages_per_step]
    inv = 1.0 / MOBA_BLOCK
    for c in range(pages_per_step // pages_per_block):
        tot = jnp.sum(pages[c * pages_per_block][...], axis=0)
        for r in range(1, pages_per_block):
            tot = tot + jnp.sum(pages[c * pages_per_block + r][...], axis=0)
        o_ref[c] = tot * inv


def _page_means(cache_k, layer, page_table):
    _, _, page, n_heads, hd = cache_k.shape
    bs, n_pages = page_table.shape
    ppb = MOBA_BLOCK // page
    pps = _pick(n_pages, (8, 4, 2)) if ppb == 2 else ppb
    n_blocks = n_pages // ppb

    def page_spec(c):
        return pl.BlockSpec((None, None, page, n_heads, hd),
                            lambda b, s, pt: (layer, pt[b * n_pages + s * pps + c], 0, 0, 0))

    return pl.pallas_call(
        functools.partial(_page_mean_body, pages_per_step=pps, pages_per_block=ppb),
        grid_spec=pltpu.PrefetchScalarGridSpec(
            num_scalar_prefetch=1,
            grid=(bs, n_pages // pps),
            in_specs=[page_spec(c) for c in range(pps)],
            out_specs=pl.BlockSpec((None, pps // ppb, n_heads, hd), lambda b, s, pt: (b, s, 0, 0)),
        ),
        out_shape=jax.ShapeDtypeStruct((bs, n_blocks, n_heads, hd), F32),
        compiler_params=_params(("arbitrary", "arbitrary"), 40),
        name="page_means",
    )(page_table.reshape(-1), *([cache_k] * pps))


def _select_body(km_ref, q_ref, o_ref, *, n_heads):
    n_blocks = km_ref.shape[1]
    res = jnp.zeros(o_ref.shape, I32)
    row = lax.broadcasted_iota(I32, o_ref.shape, 0)
    lane = lax.broadcasted_iota(I32, o_ref.shape, 1)
    blk_id = lax.broadcasted_iota(I32, (n_blocks, 1), 0)
    for h in range(n_heads):
        gate = jnp.sum(km_ref[h] * q_ref[h:h + 1, :], axis=1, keepdims=True)
        for k in range(MOBA_TOPK):
            best = jnp.max(gate, axis=0, keepdims=True)
            idx = jnp.min(jnp.where(gate == best, blk_id, n_blocks), axis=0, keepdims=True)
            res = jnp.where((row == k) & (lane == h), idx, res)
            gate = jnp.where(blk_id == idx, -jnp.inf, gate)
    o_ref[...] = res


def _select_blocks(kmean_t, q_s):
    bs, n_heads, n_blocks, hd = kmean_t.shape
    return pl.pallas_call(
        functools.partial(_select_body, n_heads=n_heads),
        grid=(bs,),
        in_specs=[pl.BlockSpec((None, n_heads, n_blocks, hd), lambda b: (b, 0, 0, 0)),
                  pl.BlockSpec((None, n_heads, hd), lambda b: (b, 0, 0))],
        out_specs=pl.BlockSpec((None, SUBLANES, LANES), lambda b: (b, 0, 0)),
        out_shape=jax.ShapeDtypeStruct((bs, SUBLANES, LANES), I32),
        compiler_params=_params(("arbitrary",), 32),
        name="moba_select",
    )(kmean_t, q_s.reshape(bs, n_heads, hd))


def _decode_attn_body(ph_ref, q_ref, kn_ref, vn_ref, ck_hbm, cv_hbm, o_ref, kbuf, vbuf, sem,
                      *, layer, bs, n_heads, n_sel_pages, scale):
    t = pl.program_id(0)
    n_steps = pl.num_programs(0)

    def page_copies(step, slot):
        h, b = step // bs, step % bs
        out = []
        for r in range(n_sel_pages):
            pg = ph_ref[(b * n_heads + h) * n_sel_pages + r]
            out.append(pltpu.make_async_copy(ck_hbm.at[layer, pg, :, h, :], kbuf.at[slot, r], sem.at[slot]))
            out.append(pltpu.make_async_copy(cv_hbm.at[layer, pg, :, h, :], vbuf.at[slot, r], sem.at[slot]))
        return out

    @pl.when(t == 0)
    def _first():
        for c in page_copies(t, 0):
            c.start()

    @pl.when(t + 1 < n_steps)
    def _prefetch():
        for c in page_copies(t + 1, (t + 1) % 2):
            c.start()

    slot = t % 2
    for c in page_copies(t, slot):
        c.wait()

    row = pl.ds(t % bs, 1)
    q = q_ref[row, :]
    s_new = jnp.sum(q * kn_ref[row, :], axis=1, keepdims=True) * scale
    scores = [jnp.sum(kbuf[slot, r] * q, axis=1, keepdims=True) * scale for r in range(n_sel_pages)]
    top = s_new
    for s in scores:
        top = jnp.maximum(top, jnp.max(s, axis=0, keepdims=True))
    p_new = jnp.exp(s_new - top)
    denom = p_new
    acc = p_new * vn_ref[row, :]
    for r, s in enumerate(scores):
        p = jnp.exp(s - top)
        denom = denom + jnp.sum(p, axis=0, keepdims=True)
        acc = acc + jnp.sum(p * vbuf[slot, r], axis=0, keepdims=True)
    o_ref[row, :] = acc / denom


def _decode_attn(q_s, k_s, v_s, cache_k, cache_v, layer, phys_pages, *, n_heads, hd):
    bs = q_s.shape[0]
    page = cache_k.shape[2]
    nsp = phys_pages.shape[-1]
    head_rows = pl.BlockSpec((bs, hd), lambda t, ph: (0, t // bs))
    anywhere = pl.BlockSpec(memory_space=pl.ANY)
    return pl.pallas_call(
        functools.partial(_decode_attn_body, layer=layer, bs=bs, n_heads=n_heads, n_sel_pages=nsp,
                          scale=hd ** -0.5),
        grid_spec=pltpu.PrefetchScalarGridSpec(
            num_scalar_prefetch=1,
            grid=(n_heads * bs,),
            in_specs=[head_rows, head_rows, head_rows, anywhere, anywhere],
            out_specs=head_rows,
            scratch_shapes=[pltpu.VMEM((2, nsp, page, hd), F32), pltpu.VMEM((2, nsp, page, hd), F32),
                            pltpu.SemaphoreType.DMA((2,))],
        ),
        out_shape=jax.ShapeDtypeStruct((bs, n_heads * hd), F32),
        compiler_params=_params(("arbitrary",), 32),
        name="moba_decode",
    )(phys_pages.reshape(-1), q_s, k_s, v_s, cache_k, cache_v)


def _gelu_tanh(x):
    return 0.5 * x * (1.0 + jnp.tanh(np.sqrt(2.0 / np.pi).astype(np.float32) * (x + 0.044715 * (x * x * x))))


def _softplus(x):
    return jnp.maximum(x, 0.0) + jnp.log1p(jnp.exp(-jnp.abs(x)))


def _lru_gates(uc, wa_ref, wx_ref, ba, bx, lam, n_heads, rb):
    ucb = uc.astype(BF16)
    a_parts, b_parts = [], []
    sp = _softplus(-lam)
    for h in range(n_heads):
        cols = slice(h * rb, (h + 1) * rb)
        r = jax.nn.sigmoid(jnp.dot(ucb[:, cols], wa_ref[h].astype(BF16), preferred_element_type=F32) + ba[:, cols])
        g = jax.nn.sigmoid(jnp.dot(ucb[:, cols], wx_ref[h].astype(BF16), preferred_element_type=F32) + bx[:, cols])
        log_a = -LRU_C * r * sp[:, cols]
        a_parts.append(jnp.exp(log_a))
        b_parts.append(jnp.sqrt(1.0 - jnp.exp(2.0 * log_a)) * (g * uc[:, cols]))
    return jnp.concatenate(a_parts, axis=1), jnp.concatenate(b_parts, axis=1)


def _lru_body(g_ref, u_ref, wc_ref, bc_ref, wa_ref, wx_ref, ba_ref, bx_ref, lam_ref,
              c0_ref, c1_ref, c2_ref, h0_ref,
              y_ref, newc_ref, newh_ref, us_ref, hs_ref,
              carry_ref, hcarry_ref, a_buf, b_buf, *, tiles_per_seq, n_prompt_tiles, bs, n_heads, rb):
    i = pl.program_id(0)
    tm = y_ref.shape[0]
    w0, w1, w2, w3 = wc_ref[0:1, :], wc_ref[1:2, :], wc_ref[2:3, :], wc_ref[3:4, :]
    bias = bc_ref[...]
    gates = functools.partial(_lru_gates, wa_ref=wa_ref, wx_ref=wx_ref, ba=ba_ref[...], bx=bx_ref[...],
                              lam=lam_ref[...], n_heads=n_heads, rb=rb)

    @pl.when(i < n_prompt_tiles)
    def _prompt():
        u = u_ref[...]
        fresh = i % tiles_per_seq == 0
        c0, c1, c2 = (jnp.where(fresh, 0.0, carry_ref[r:r + 1, :]) for r in range(3))
        p1 = _shift_rows(u, 1, (c2,))
        p2 = _shift_rows(u, 2, (c1, c2))
        p3 = _shift_rows(u, 3, (c0, c1, c2))
        uc = w0 * p3 + w1 * p2 + w2 * p1 + w3 * u + bias
        a, b = gates(uc)
        a_buf[...] = a
        b_buf[...] = b
        sub = lax.broadcasted_iota(I32, (SUBLANES, a.shape[1]), 0)

        def group(gi, h):
            rows = pl.ds(pl.multiple_of(gi * SUBLANES, SUBLANES), SUBLANES)
            ag, bg = a_buf[rows, :], b_buf[rows, :]
            for d in (1, 2, 4):
                bg = bg + ag * jnp.where(sub >= d, pltpu.roll(bg, d, 0), 0.0)
                ag = ag * jnp.where(sub >= d, pltpu.roll(ag, d, 0), 1.0)
            hg = ag * h + bg
            b_buf[rows, :] = hg
            return hg[SUBLANES - 1:SUBLANES, :]

        h_last = lax.fori_loop(0, tm // SUBLANES, group, jnp.where(fresh, 0.0, hcarry_ref[0:1, :]))
        hcarry_ref[0:1, :] = h_last
        y_ref[...] = (b_buf[...] * _gelu_tanh(g_ref[...])).astype(y_ref.dtype)
        carry_ref[0:3, :] = u[tm - 3:tm, :]
        newc_ref[...] = u[tm - 3:tm, :]
        newh_ref[...] = h_last

    @pl.when(i == n_prompt_tiles)
    def _sample():
        u = u_ref[0:bs, :]
        uc = w0 * c0_ref[...] + w1 * c1_ref[...] + w2 * c2_ref[...] + w3 * u + bias
        a, b = gates(uc)
        h = a * h0_ref[...] + b
        y_ref[...] = jnp.zeros(y_ref.shape, y_ref.dtype)
        y_ref[0:bs, :] = (h * _gelu_tanh(g_ref[0:bs, :])).astype(y_ref.dtype)
        us_ref[...] = u
        hs_ref[...] = h


def _rglru(proj, w_conv, b_conv, w_gate_a, b_gate_a, w_gate_x, b_gate_x, lam, layer, conv_state, h0,
           *, bp, seq, bs):
    m_tot = proj.shape[0]
    dr = proj.shape[1] // 2
    n_heads, rb = w_gate_a.shape[1], w_gate_a.shape[2]
    tm = TOKEN_TILE
    tps = seq // tm
    npt = bp * tps
    width = w_conv.shape[1]
    col = lambda c: pl.BlockSpec((tm, dr), lambda i: (i, c))
    vec = pl.BlockSpec((None, 1, dr), lambda i: (layer, 0, 0))
    gw = pl.BlockSpec((None, n_heads, rb, rb), lambda i: (layer, 0, 0, 0))
    st = pl.BlockSpec((bs, dr), lambda i: (0, 0))
    pb = lambda rows: pl.BlockSpec((None, rows, dr), lambda i: (jnp.minimum(i // tps, bp - 1), 0, 0))
    return pl.pallas_call(
        functools.partial(_lru_body, tiles_per_seq=tps, n_prompt_tiles=npt, bs=bs, n_heads=n_heads, rb=rb),
        grid=(m_tot // tm,),
        in_specs=[col(0), col(1), pl.BlockSpec((None, width, dr), lambda i: (layer, 0, 0)), vec,
                  gw, gw, vec, vec, vec, st, st, st, st],
        out_specs=[pl.BlockSpec((tm, dr), lambda i: (i, 0)), pb(width - 1), pb(1), st, st],
        out_shape=[jax.ShapeDtypeStruct((m_tot, dr), BF16),
                   jax.ShapeDtypeStruct((bp, width - 1, dr), F32),
                   jax.ShapeDtypeStruct((bp, 1, dr), F32),
                   jax.ShapeDtypeStruct((bs, dr), F32),
                   jax.ShapeDtypeStruct((bs, dr), F32)],
        scratch_shapes=[pltpu.VMEM((SUBLANES, dr), F32), pltpu.VMEM((SUBLANES, dr), F32),
                        pltpu.VMEM((tm, dr), F32), pltpu.VMEM((tm, dr), F32)],
        compiler_params=_params(("arbitrary",), 48),
        name="rglru",
    )(proj, proj, w_conv, b_conv.reshape(-1, 1, dr), w_gate_a, w_gate_x,
      b_gate_a.reshape(-1, 1, dr), b_gate_x.reshape(-1, 1, dr), lam.reshape(-1, 1, dr),
      conv_state[:, 0], conv_state[:, 1], conv_state[:, 2], h0)


def _ffn_body(e_ref, start_ref, nt_ref, nused_ref, xs_hbm, wg_ref, wu_ref, wd_ref, o_hbm,
              xbuf, acc, wgb, wub, wdb, sem_in, sem_out, *, tm, nf):
    del e_ref
    s, f = pl.program_id(0), pl.program_id(1)
    nt = nt_ref[s]
    start = start_ref[s]

    def copy_in(r):
        return pltpu.make_async_copy(xs_hbm.at[pl.ds(pl.multiple_of(start + r * tm, tm), tm), :],
                                     xbuf.at[pl.ds(pl.multiple_of(r * tm, tm), tm), :], sem_in)

    def copy_out(r):
        return pltpu.make_async_copy(acc.at[pl.ds(pl.multiple_of(r * tm, tm), tm), :],
                                     o_hbm.at[pl.ds(pl.multiple_of(start + r * tm, tm), tm), :], sem_out)

    def for_tiles(fn):
        def body(r, c):
            fn(r)
            return c
        lax.fori_loop(0, nt, body, 0)

    @pl.when(nt > 0)
    def _run():
        @pl.when(f == 0)
        def _load():
            for_tiles(lambda r: copy_in(r).start())
            for_tiles(lambda r: copy_in(r).wait())

        wgb[...] = wg_ref[...].astype(BF16)
        wub[...] = wu_ref[...].astype(BF16)
        wdb[...] = wd_ref[...].astype(BF16)

        def tile(r):
            rows = pl.ds(pl.multiple_of(r * tm, tm), tm)
            x = xbuf[rows, :]
            g = jnp.dot(x, wgb[...], preferred_element_type=F32)
            u = jnp.dot(x, wub[...], preferred_element_type=F32)
            h = (g * jax.nn.sigmoid(g) * u).astype(BF16)
            d = jnp.dot(h, wdb[...], preferred_element_type=F32)

            @pl.when(f == 0)
            def _first():
                acc[rows, :] = d

            @pl.when(f > 0)
            def _rest():
                acc[rows, :] += d

        for_tiles(tile)

        @pl.when(f == nf - 1)
        def _store():
            for_tiles(lambda r: copy_out(r).start())
            for_tiles(lambda r: copy_out(r).wait())

    @pl.when((s == pl.num_programs(0) - 1) & (f == nf - 1))
    def _zero_unused():
        acc[0:tm, :] = jnp.zeros((tm, acc.shape[1]), F32)

        def zero_tile(t, c):
            cp = pltpu.make_async_copy(acc.at[0:tm, :], o_hbm.at[pl.ds(pl.multiple_of(t * tm, tm), tm), :], sem_out)
            cp.start()
            cp.wait()
            return c

        lax.fori_loop(nused_ref[0], o_hbm.shape[0] // tm, zero_tile, 0)


def _ffn(xs, w_gate, w_up, w_down, layer, sched_e, sched_start, sched_nt, n_used_tiles):
    s_rows, d = xs.shape
    dff = w_gate.shape[3]
    tm, tf, rt = TOKEN_TILE, FFN_COL_TILE, FFN_SUPER_TILES
    nf = dff // tf
    n_super = sched_e.shape[0]

    def fcol(s, f, nt_ref):
        return jnp.where(nt_ref[s] > 0, f, nf - 1)

    up_spec = pl.BlockSpec((None, None, d, tf), lambda s, f, e, st, nt, nu: (layer, e[s], 0, fcol(s, f, nt)))
    down_spec = pl.BlockSpec((None, None, tf, d), lambda s, f, e, st, nt, nu: (layer, e[s], fcol(s, f, nt), 0))
    return pl.pallas_call(
        functools.partial(_ffn_body, tm=tm, nf=nf),
        grid_spec=pltpu.PrefetchScalarGridSpec(
            num_scalar_prefetch=4,
            grid=(n_super, nf),
            in_specs=[pl.BlockSpec(memory_space=pl.ANY), up_spec, up_spec, down_spec],
            out_specs=pl.BlockSpec(memory_space=pl.ANY),
            scratch_shapes=[pltpu.VMEM((rt * tm, d), BF16), pltpu.VMEM((rt * tm, d), F32),
                            pltpu.VMEM((d, tf), BF16), pltpu.VMEM((d, tf), BF16), pltpu.VMEM((tf, d), BF16),
                            pltpu.SemaphoreType.DMA(()), pltpu.SemaphoreType.DMA(())],
        ),
        out_shape=jax.ShapeDtypeStruct((s_rows, d), F32),
        compiler_params=_params(("arbitrary", "arbitrary"), 56),
        name="swiglu_ffn",
    )(sched_e, sched_start, sched_nt, n_used_tiles, xs, w_gate, w_up, w_down)


def _dense_schedule(n_tiles):
    rt = FFN_SUPER_TILES
    n_super = -(-n_tiles // rt)
    start = np.arange(n_super, dtype=np.int32) * rt
    nt = np.minimum(rt, n_tiles - start).astype(np.int32)
    return (jnp.zeros((n_super,), I32), jnp.asarray(start * TOKEN_TILE, I32), jnp.asarray(nt, I32),
            jnp.full((1,), n_tiles, I32))


def _router_body(x_ref, w_ref, o_ref, *, n_exp):
    logits = jnp.dot(x_ref[...], w_ref[...], precision=lax.Precision.HIGHEST, preferred_element_type=F32)
    lane = lax.broadcasted_iota(I32, logits.shape, 1)
    lg = jnp.where(lane < n_exp, logits, -jnp.inf)
    m1 = jnp.max(lg, axis=1, keepdims=True)
    i1 = jnp.min(jnp.where(lg == m1, lane, LANES), axis=1, keepdims=True)
    lg = jnp.where(lane == i1, -jnp.inf, lg)
    m2 = jnp.max(lg, axis=1, keepdims=True)
    i2 = jnp.min(jnp.where(lg == m2, lane, LANES), axis=1, keepdims=True)
    e = jnp.exp(m2 - m1)
    g1 = 1.0 / (1.0 + e)
    g2 = e / (1.0 + e)
    o_ref[...] = jnp.where(lane == 0, i1.astype(F32),
                           jnp.where(lane == 1, i2.astype(F32),
                                     jnp.where(lane == 2, g1, jnp.where(lane == 3, g2, 0.0))))


def _router(x, w_router, layer):
    m, d = x.shape
    n_exp = w_router.shape[2]
    w_pad = jnp.pad(w_router, ((0, 0), (0, 0), (0, LANES - n_exp)))
    tm = TOKEN_TILE
    return pl.pallas_call(
        functools.partial(_router_body, n_exp=n_exp),
        grid=(m // tm,),
        in_specs=[pl.BlockSpec((tm, d), lambda i: (i, 0)),
                  pl.BlockSpec((None, d, LANES), lambda i: (layer, 0, 0))],
        out_specs=pl.BlockSpec((tm, LANES), lambda i: (i, 0)),
        out_shape=jax.ShapeDtypeStruct((m, LANES), F32),
        compiler_params=_params(("arbitrary",), 32),
        name="moe_router",
    )(x, w_pad)


def _moe_plan(route, n_valid, n_exp):
    m_tot = route.shape[0]
    tm, rt = TOKEN_TILE, FFN_SUPER_TILES
    n_tiles_max = (MOE_TOP_K * n_valid + n_exp * (tm - 1)) // tm
    s_rows = n_tiles_max * tm
    n_super = n_tiles_max // rt + n_exp
    e1, e2 = route[:, 0].astype(I32), route[:, 1].astype(I32)
    tok = jnp.arange(m_tot, dtype=I32)
    valid = tok < n_valid
    ids = jnp.arange(n_exp, dtype=I32)[None, :]
    oh1 = (e1[:, None] == ids) & valid[:, None]
    oh2 = (e2[:, None] == ids) & valid[:, None]
    oh = oh1.astype(I32) + oh2.astype(I32)
    csum = jnp.cumsum(oh, axis=0)
    before = csum - oh
    counts = csum[-1]
    tiles_e = (counts + tm - 1) // tm
    tile_end = jnp.cumsum(tiles_e)
    tile_off = tile_end - tiles_e
    row_off = tile_off * tm
    pos1 = jnp.sum(jnp.where(oh1, before + row_off[None, :], 0), axis=1)
    pos2 = jnp.sum(jnp.where(oh2, before + row_off[None, :], 0), axis=1)
    drop = jnp.where(valid, 0, s_rows)
    slot_token = jnp.zeros((s_rows,), I32)
    slot_token = slot_token.at[pos1 + drop].set(tok, mode="drop").at[pos2 + drop].set(tok, mode="drop")
    n_used_tiles = tile_end[-1:]
    super_e = -(-tiles_e // rt)
    super_end = jnp.cumsum(super_e)
    super_off = super_end - super_e
    n_used_super = super_end[-1]
    sidx = jnp.arange(n_super, dtype=I32)
    live = sidx < n_used_super
    sclamp = jnp.minimum(sidx, n_used_super - 1)
    exp_of = jnp.sum((sclamp[:, None] >= super_end[None, :]).astype(I32), axis=1)
    k = sclamp - super_off[exp_of]
    sched_start = (tile_off[exp_of] + k * rt) * tm
    sched_nt = jnp.where(live, jnp.clip(tiles_e[exp_of] - k * rt, 0, rt), 0)
    return pos1, pos2, slot_token, n_used_tiles, (exp_of, sched_start, sched_nt), s_rows


def _gather_body(tok_ref, nused_ref, x_hbm, o_ref, buf, sem, *, tm):
    t = pl.program_id(0)

    def row_copy(r):
        return pltpu.make_async_copy(x_hbm.at[pl.ds(tok_ref[t * tm + r], 1), :], buf.at[pl.ds(r, 1), :], sem)

    @pl.when(t < nused_ref[0])
    def _used():
        def issue(r, c):
            row_copy(r).start()
            return c

        def drain(r, c):
            row_copy(r).wait()
            return c

        lax.fori_loop(0, tm, issue, 0)
        lax.fori_loop(0, tm, drain, 0)
        o_ref[...] = buf[...].astype(o_ref.dtype)

    @pl.when(t >= nused_ref[0])
    def _unused():
        o_ref[...] = jnp.zeros(o_ref.shape, o_ref.dtype)


def _gather_rows(x, slot_token, n_used_tiles):
    s_rows = slot_token.shape[0]
    d = x.shape[1]
    tm = TOKEN_TILE
    return pl.pallas_call(
        functools.partial(_gather_body, tm=tm),
        grid_spec=pltpu.PrefetchScalarGridSpec(
            num_scalar_prefetch=2,
            grid=(s_rows // tm,),
            in_specs=[pl.BlockSpec(memory_space=pl.ANY)],
            out_specs=pl.BlockSpec((tm, d), lambda t, tok, nu: (t, 0)),
            scratch_shapes=[pltpu.VMEM((tm, d), F32), pltpu.SemaphoreType.DMA(())],
        ),
        out_shape=jax.ShapeDtypeStruct((s_rows, d), BF16),
        compiler_params=_params(("arbitrary",), 32),
        name="moe_gather",
    )(slot_token, n_used_tiles, x)


def _combine_body(p1_ref, p2_ref, x_ref, route_ref, g_ref, b_ref, y_hbm, o_ref, ob_ref, buf, sem, *, tm, alpha):
    t = pl.program_id(0)

    def row_copy(r, which, p_ref):
        return pltpu.make_async_copy(y_hbm.at[pl.ds(p_ref[t * tm + r], 1), :],
                                     buf.at[which, pl.ds(r, 1), :], sem)

    def issue(r, c):
        row_copy(r, 0, p1_ref).start()
        row_copy(r, 1, p2_ref).start()
        return c

    def drain(r, c):
        row_copy(r, 0, p1_ref).wait()
        row_copy(r, 1, p2_ref).wait()
        return c

    lax.fori_loop(0, tm, issue, 0)
    lax.fori_loop(0, tm, drain, 0)
    route = route_ref[...]
    y = route[:, 2:3] * buf[0] + route[:, 3:4] * buf[1]
    out = _layer_norm(alpha * x_ref[...] + y, g_ref[...], b_ref[...])
    o_ref[...] = out
    ob_ref[...] = out.astype(BF16)


def _combine_ln(x, route, y_slots, pos1, pos2, g_stack, b_stack, layer, alpha):
    m, d = x.shape
    tm = TOKEN_TILE
    row = pl.BlockSpec((tm, d), lambda t, p1, p2: (t, 0))
    par = pl.BlockSpec((None, 1, d), lambda t, p1, p2: (layer, 0, 0))
    return pl.pallas_call(
        functools.partial(_combine_body, tm=tm, alpha=alpha),
        grid_spec=pltpu.PrefetchScalarGridSpec(
            num_scalar_prefetch=2,
            grid=(m // tm,),
            in_specs=[row, pl.BlockSpec((tm, LANES), lambda t, p1, p2: (t, 0)), par, par,
                      pl.BlockSpec(memory_space=pl.ANY)],
            out_specs=[row, row],
            scratch_shapes=[pltpu.VMEM((2, tm, d), F32), pltpu.SemaphoreType.DMA(())],
        ),
        out_shape=[jax.ShapeDtypeStruct((m, d), F32), jax.ShapeDtypeStruct((m, d), BF16)],
        compiler_params=_params(("arbitrary",), 32),
        name="moe_combine_ln",
    )(pos1, pos2, x, route, g_stack.reshape(-1, 1, d), b_stack.reshape(-1, 1, d), y_slots)


def kernel(x_prompt, x_sample, cache_k, cache_v, page_table, state_conv, state_lru_conv, state_lru_h, ln1_g, ln1_b, ln2_g, ln2_b, sc_w_in, sc_w_conv, sc_w_out, attn_w_qkv, attn_w_o, lru_w_in, lru_w_conv, lru_b_conv, lru_w_gate_a, lru_b_gate_a, lru_w_gate_x, lru_b_gate_x, lru_lambda, lru_w_out, ffn_w_gate, ffn_w_up, ffn_w_down, moe_w_router, moe_w_gate, moe_w_up, moe_w_down):
    bp, seq, d = x_prompt.shape
    bs, dec_seq, _ = x_sample.shape
    depth = ln1_g.shape[0]
    n_heads, hd = cache_k.shape[3], cache_k.shape[4]
    page = cache_k.shape[2]
    past_len = page_table.shape[1] * page
    n_exp = moe_w_router.shape[2]
    assert dec_seq == 1 and bs <= TOKEN_TILE
    assert seq % MOBA_BLOCK == 0 and seq % TOKEN_TILE == 0 and past_len % MOBA_BLOCK == 0
    assert MOBA_BLOCK % page == 0 and past_len // MOBA_BLOCK >= MOBA_TOPK
    alpha = float((2.0 * depth) ** 0.25)
    mp = bp * seq
    n_valid = mp + bs
    m_tot = -(-n_valid // TOKEN_TILE) * TOKEN_TILE
    ppb = MOBA_BLOCK // page

    x = jnp.concatenate([x_prompt.reshape(mp, d), x_sample.reshape(bs, d),
                         jnp.zeros((m_tot - n_valid, d), x_prompt.dtype)], axis=0)
    xb = x.astype(BF16)
    dense_sched = _dense_schedule(m_tot // TOKEN_TILE)

    conv_p, conv_s, kp_l, vp_l, ks_l, vs_l, lc_p, lc_s, lh_p, lh_s = ([] for _ in range(10))
    for i in range(depth):
        kind, m = i % 3, i // 3
        if kind == 0:
            proj = _matmul(xb, sc_w_in, m)
            y, new_p, u_s = _short_conv(proj, sc_w_conv, m, state_conv[m, :, 0], state_conv[m, :, 1],
                                        bp=bp, seq=seq, bs=bs)
            conv_p.append(new_p)
            conv_s.append(jnp.stack([state_conv[m, :, 1], u_s], axis=1))
            mix = _matmul(y, sc_w_out, m)
        elif kind == 1:
            qkv = _matmul(xb, attn_w_qkv, m)
            q, k_p, v_p, k_s, v_s = _rope(qkv, n_heads=n_heads, hd=hd, bp=bp, seq=seq, bs=bs, past_len=past_len)
            kp_l.append(k_p.reshape(bp, seq, n_heads, hd))
            vp_l.append(v_p.reshape(bp, seq, n_heads, hd))
            ks_l.append(k_s.reshape(bs, 1, n_heads, hd))
            vs_l.append(v_s.reshape(bs, 1, n_heads, hd))
            o_p = _attn_prompt(q, k_p, v_p, bp=bp, seq=seq, n_heads=n_heads, hd=hd)
            q_s = q[mp:mp + bs]
            kmean = _page_means(cache_k, m, page_table)
            sel = _select_blocks(kmean.transpose(0, 2, 1, 3), q_s)
            sel = sel[:, :MOBA_TOPK, :n_heads].transpose(0, 2, 1)
            logical = sel[..., None] * ppb + jnp.arange(ppb, dtype=I32)
            phys = jnp.take_along_axis(page_table, logical.reshape(bs, -1), axis=1)
            o_s = _decode_attn(q_s, k_s, v_s, cache_k, cache_v, m, phys.reshape(bs, n_heads, -1),
                               n_heads=n_heads, hd=hd)
            o = jnp.concatenate([o_p, o_s.astype(BF16), jnp.zeros((m_tot - n_valid, n_heads * hd), BF16)], axis=0)
            mix = _matmul(o, attn_w_o, m)
        else:
            proj = _matmul(xb, lru_w_in, m)
            y, new_c, new_h, u_s, h_s = _rglru(proj, lru_w_conv, lru_b_conv, lru_w_gate_a, lru_b_gate_a,
                                               lru_w_gate_x, lru_b_gate_x, lru_lambda, m,
                                               state_lru_conv[m], state_lru_h[m], bp=bp, seq=seq, bs=bs)
            lc_p.append(new_c)
            lc_s.append(jnp.concatenate([state_lru_conv[m, :, 1:], u_s[:, None, :]], axis=1))
            lh_p.append(new_h.reshape(bp, -1))
            lh_s.append(h_s)
            mix = _matmul(y, lru_w_out, m)
        x, xb = _add_ln(x, mix, ln1_g, ln1_b, i, alpha)
        j = i // 2
        if i % 2 == 0:
            f = _ffn(xb, ffn_w_gate[:, None], ffn_w_up[:, None], ffn_w_down[:, None], j, *dense_sched)
            x, xb = _add_ln(x, f, ln2_g, ln2_b, i, alpha)
        else:
            route = _router(x, moe_w_router, j)
            pos1, pos2, slot_token, n_used, sched, _ = _moe_plan(route, n_valid, n_exp)
            xs = _gather_rows(x, slot_token, n_used)
            ys = _ffn(xs, moe_w_gate, moe_w_up, moe_w_down, j, *sched, n_used)
            x, xb = _combine_ln(x, route, ys, pos1, pos2, ln2_g, ln2_b, i, alpha)
    return (x[:mp].reshape(bp, seq, d), x[mp:mp + bs].reshape(bs, 1, d),
            jnp.stack(conv_p), jnp.stack(conv_s),
            jnp.stack(kp_l), jnp.stack(vp_l), jnp.stack(ks_l), jnp.stack(vs_l),
            jnp.stack(lc_p), jnp.stack(lc_s), jnp.stack(lh_p), jnp.stack(lh_s))
```

```python
import functools

import numpy as np
import jax
import jax.numpy as jnp
from jax import lax
from jax.experimental import pallas as pl
from jax.experimental.pallas import tpu as pltpu

F32, BF16, I32 = jnp.float32, jnp.bfloat16, jnp.int32

MOBA_BLOCK = 256
MOBA_TOPK = 3
ROPE_THETA = 500000.0
ROT_FRACTION = 4
LRU_C = 8.0
MOE_TOP_K = 2
LN_EPS = 1e-5
NEG_INF = -1e30

LANES = 128
SUBLANES = 8
V7X_VMEM_BYTES = 64 * 1024 * 1024

TOKEN_TILE = 256
FFN_COL_TILE = 256
FFN_SUPER_TILES = 10
DMA_ISSUE_UNROLL = 8


def _params(sem, vmem_mib, **kw):
    return pltpu.CompilerParams(dimension_semantics=sem, vmem_limit_bytes=vmem_mib * 1024 * 1024, **kw)


def _pick(total, candidates):
    for c in candidates:
        if total % c == 0:
            return c
    raise ValueError(f"no tile in {candidates} divides {total}")


def _mm_body(x_ref, w_ref, o_ref, wb_ref):
    @pl.when(pl.program_id(1) == 0)
    def _cast():
        wb_ref[...] = w_ref[...].astype(BF16)

    o_ref[...] = jnp.dot(x_ref[...], wb_ref[...], preferred_element_type=F32).astype(o_ref.dtype)


def _matmul(xb, w_stack, layer, out_dtype=F32):
    m, k = xb.shape
    n = w_stack.shape[2]
    tm = _pick(m, (768, 512, 256))
    tn = _pick(n, (1024, 512, 256, 128))
    return pl.pallas_call(
        _mm_body,
        grid=(n // tn, m // tm),
        in_specs=[pl.BlockSpec((tm, k), lambda j, i: (i, 0)),
                  pl.BlockSpec((None, k, tn), lambda j, i: (layer, 0, j))],
        out_specs=pl.BlockSpec((tm, tn), lambda j, i: (i, j)),
        out_shape=jax.ShapeDtypeStruct((m, n), out_dtype),
        scratch_shapes=[pltpu.VMEM((k, tn), BF16)],
        compiler_params=_params(("arbitrary", "arbitrary"), 48),
        name="matmul",
    )(xb, w_stack)


def _layer_norm(z, g, b):
    mu = jnp.mean(z, axis=-1, keepdims=True)
    zc = z - mu
    var = jnp.mean(zc * zc, axis=-1, keepdims=True)
    return zc * lax.rsqrt(var + LN_EPS) * g + b


def _add_ln_body(x_ref, y_ref, g_ref, b_ref, o_ref, ob_ref, *, alpha):
    out = _layer_norm(alpha * x_ref[...] + y_ref[...], g_ref[...], b_ref[...])
    o_ref[...] = out
    ob_ref[...] = out.astype(BF16)


def _add_ln(x, y, g_stack, b_stack, layer, alpha):
    m, d = x.shape
    tm = TOKEN_TILE
    row = pl.BlockSpec((tm, d), lambda i: (i, 0))
    par = pl.BlockSpec((None, 1, d), lambda i: (layer, 0, 0))
    return pl.pallas_call(
        functools.partial(_add_ln_body, alpha=alpha),
        grid=(m // tm,),
        in_specs=[row, row, par, par],
        out_specs=[row, row],
        out_shape=[jax.ShapeDtypeStruct((m, d), F32), jax.ShapeDtypeStruct((m, d), BF16)],
        compiler_params=_params(("arbitrary",), 32),
        name="add_ln",
    )(x, y, g_stack.reshape(-1, 1, d), b_stack.reshape(-1, 1, d))


def _shift_rows(u, k, carry_rows):
    out = pltpu.roll(u, k, 0)
    row = lax.broadcasted_iota(I32, u.shape, 0)
    for r, c in enumerate(carry_rows):
        out = jnp.where(row == r, c, out)
    return out


def _sconv_body(b_ref, c_ref, h_ref, w_ref, s0_ref, s1_ref, y_ref, newp_ref, news_ref, carry_ref,
                *, tiles_per_seq, n_prompt_tiles, bs):
    i = pl.program_id(0)
    w0, w1, w2 = w_ref[0:1, :], w_ref[1:2, :], w_ref[2:3, :]
    tm = y_ref.shape[0]

    @pl.when(i < n_prompt_tiles)
    def _prompt():
        u = c_ref[...] * h_ref[...]
        fresh = i % tiles_per_seq == 0
        c0 = jnp.where(fresh, 0.0, carry_ref[0:1, :])
        c1 = jnp.where(fresh, 0.0, carry_ref[1:2, :])
        p1 = _shift_rows(u, 1, (c1,))
        p2 = _shift_rows(u, 2, (c0, c1))
        y = w0 * p2 + w1 * p1 + w2 * u
        y_ref[...] = (b_ref[...] * y).astype(y_ref.dtype)
        carry_ref[0:2, :] = u[tm - 2:tm, :]
        newp_ref[...] = u[tm - 2:tm, :]

    @pl.when(i == n_prompt_tiles)
    def _sample():
        u = c_ref[0:bs, :] * h_ref[0:bs, :]
        y = w0 * s0_ref[...] + w1 * s1_ref[...] + w2 * u
        y_ref[...] = jnp.zeros(y_ref.shape, y_ref.dtype)
        y_ref[0:bs, :] = (b_ref[0:bs, :] * y).astype(y_ref.dtype)
        news_ref[...] = u


def _short_conv(proj, w_conv, layer, s0, s1, *, bp, seq, bs):
    m_tot = proj.shape[0]
    d = proj.shape[1] // 3
    tm = TOKEN_TILE
    tps = seq // tm
    npt = bp * tps
    col = lambda c: pl.BlockSpec((tm, d), lambda i: (i, c))
    full = lambda shape: pl.BlockSpec(shape, lambda i: (0,) * len(shape))
    return pl.pallas_call(
        functools.partial(_sconv_body, tiles_per_seq=tps, n_prompt_tiles=npt, bs=bs),
        grid=(m_tot // tm,),
        in_specs=[col(0), col(1), col(2),
                  pl.BlockSpec((None, w_conv.shape[1], d), lambda i: (layer, 0, 0)),
                  full((bs, d)), full((bs, d))],
        out_specs=[pl.BlockSpec((tm, d), lambda i: (i, 0)),
                   pl.BlockSpec((None, 2, d), lambda i: (jnp.minimum(i // tps, bp - 1), 0, 0)),
                   full((bs, d))],
        out_shape=[jax.ShapeDtypeStruct((m_tot, d), BF16),
                   jax.ShapeDtypeStruct((bp, 2, d), F32),
                   jax.ShapeDtypeStruct((bs, d), F32)],
        scratch_shapes=[pltpu.VMEM((SUBLANES, d), F32)],
        compiler_params=_params(("arbitrary",), 40),
        name="short_conv",
    )(proj, proj, proj, w_conv, s0, s1)


def _rope_head(x, cos_f, sin_f, half):
    lane = lax.broadcasted_iota(I32, x.shape, 1)
    partner = jnp.where(lane < half, pltpu.roll(x, LANES - half, 1), pltpu.roll(x, half, 1))
    return x * cos_f + partner * sin_f


def _rope_body(q_ref, k_ref, v_ref, cos_ref, sin_ref, cos_s_ref, sin_s_ref,
               qo_ref, kp_ref, vp_ref, ks_ref, vs_ref, *, n_heads, hd, half, n_prompt_tiles, bs):
    i = pl.program_id(0)

    def rotate(x_ref, rows, cos_f, sin_f):
        return [
            _rope_head(x_ref[rows, h * hd:(h + 1) * hd], cos_f, sin_f, half) for h in range(n_heads)
        ]

    @pl.when(i < n_prompt_tiles)
    def _prompt():
        rows = slice(None)
        cos_f, sin_f = cos_ref[...], sin_ref[...]
        for h, (qh, kh) in enumerate(zip(rotate(q_ref, rows, cos_f, sin_f), rotate(k_ref, rows, cos_f, sin_f))):
            qo_ref[:, h * hd:(h + 1) * hd] = qh
            kp_ref[:, h * hd:(h + 1) * hd] = kh
        vp_ref[...] = v_ref[...]

    @pl.when(i == n_prompt_tiles)
    def _sample():
        rows = slice(0, bs)
        cos_f, sin_f = cos_s_ref[...], sin_s_ref[...]
        qo_ref[...] = jnp.zeros(qo_ref.shape, qo_ref.dtype)
        for h, (qh, kh) in enumerate(zip(rotate(q_ref, rows, cos_f, sin_f), rotate(k_ref, rows, cos_f, sin_f))):
            qo_ref[0:bs, h * hd:(h + 1) * hd] = qh
            ks_ref[:, h * hd:(h + 1) * hd] = kh
        vs_ref[...] = v_ref[0:bs, :]


def _rope_tables(pos, hd):
    rot = hd // ROT_FRACTION
    half = rot // 2
    inv_freq = ROPE_THETA ** (-jnp.arange(half, dtype=F32) * 2.0 / rot)
    ang = pos.astype(F32)[:, None] * inv_freq[None, :]
    cos, sin = jnp.cos(ang), jnp.sin(ang)
    n = pos.shape[0]
    cos_f = jnp.concatenate([cos, cos, jnp.ones((n, hd - rot), F32)], axis=1)
    sin_f = jnp.concatenate([-sin, sin, jnp.zeros((n, hd - rot), F32)], axis=1)
    return cos_f, sin_f


def _rope(qkv, *, n_heads, hd, bp, seq, bs, past_len):
    m_tot = qkv.shape[0]
    hdim = n_heads * hd
    tm = TOKEN_TILE
    tps = seq // tm
    npt = bp * tps
    mp = bp * seq
    cos_p, sin_p = _rope_tables(jnp.arange(seq, dtype=I32), hd)
    cos_s, sin_s = _rope_tables(jnp.full((1,), past_len, I32), hd)
    col = lambda c: pl.BlockSpec((tm, hdim), lambda i: (i, c))
    tab = pl.BlockSpec((tm, hd), lambda i: (i % tps, 0))
    one = pl.BlockSpec((1, hd), lambda i: (0, 0))
    prow = pl.BlockSpec((tm, hdim), lambda i: (jnp.minimum(i, npt - 1), 0))
    srow = pl.BlockSpec((bs, hdim), lambda i: (0, 0))
    return pl.pallas_call(
        functools.partial(_rope_body, n_heads=n_heads, hd=hd, half=hd // ROT_FRACTION // 2,
                          n_prompt_tiles=npt, bs=bs),
        grid=(m_tot // tm,),
        in_specs=[col(0), col(1), col(2), tab, tab, one, one],
        out_specs=[pl.BlockSpec((tm, hdim), lambda i: (i, 0)), prow, prow, srow, srow],
        out_shape=[jax.ShapeDtypeStruct((m_tot, hdim), F32),
                   jax.ShapeDtypeStruct((mp, hdim), F32), jax.ShapeDtypeStruct((mp, hdim), F32),
                   jax.ShapeDtypeStruct((bs, hdim), F32), jax.ShapeDtypeStruct((bs, hdim), F32)],
        compiler_params=_params(("arbitrary",), 48),
        name="rope",
    )(qkv, qkv, qkv, cos_p, sin_p, cos_s, sin_s)


def _nt_dot(a, b, **kw):
    return lax.dot_general(a, b, (((1,), (1,)), ((), ())), preferred_element_type=F32, **kw)


def _attn_body(q_ref, k_ref, v_ref, o_ref, kmean_ref, kb_ref, vt_ref, *, nb, scale):
    blk = MOBA_BLOCK
    kmean_ref[...] = jnp.zeros(kmean_ref.shape, F32)
    for n in range(nb):
        rows = slice(n * blk, (n + 1) * blk)
        kmean_ref[n:n + 1, :] = jnp.sum(k_ref[rows, :], axis=0, keepdims=True) * (1.0 / blk)
        vt_ref[:, rows] = v_ref[rows, :].T.astype(BF16)
    kb_ref[...] = k_ref[...].astype(BF16)
    gate_all = _nt_dot(kmean_ref[...], q_ref[...], precision=lax.Precision.HIGHEST)
    bid = lax.broadcasted_iota(I32, (kmean_ref.shape[0], blk), 0)
    ki = lax.broadcasted_iota(I32, (blk, blk), 0)
    qi = lax.broadcasted_iota(I32, (blk, blk), 1)

    for j in range(nb):
        cols = slice(j * blk, (j + 1) * blk)
        gate = jnp.where(bid < j, gate_all[:, cols], NEG_INF)
        rank = jnp.zeros(gate.shape, I32)
        for m in range(j):
            gm = gate[m:m + 1, :]
            beats = (gm > gate) | ((gm == gate) & (m < bid))
            rank = rank + jnp.where(beats, 1, 0)
        sel = jnp.where((bid < j) & (rank < MOBA_TOPK), 1.0, 0.0)
        keys = (j + 1) * blk
        s = _nt_dot(kb_ref[0:keys, :], q_ref[cols, :].astype(BF16)) * scale
        slabs = [jnp.where(sel[n:n + 1, :] > 0.0, s[n * blk:(n + 1) * blk, :], NEG_INF) for n in range(j)]
        slabs.append(jnp.where(ki <= qi, s[j * blk:keys, :], NEG_INF))
        top = slabs[0]
        for sl in slabs[1:]:
            top = jnp.maximum(top, sl)
        top = jnp.max(top, axis=0, keepdims=True)
        p = [jnp.exp(sl - top) for sl in slabs]
        tot = p[0]
        for pn in p[1:]:
            tot = tot + pn
        denom = jnp.sum(tot, axis=0, keepdims=True)
        pv = jnp.dot(vt_ref[:, 0:keys], jnp.concatenate(p, axis=0).astype(BF16),
                     preferred_element_type=F32)
        o_ref[cols, :] = (pv / denom).T.astype(o_ref.dtype)


def _attn_prompt(q, k_p, v_p, *, bp, seq, n_heads, hd):
    nb = seq // MOBA_BLOCK
    nbp = -(-nb // SUBLANES) * SUBLANES
    blk = pl.BlockSpec((seq, hd), lambda b, h: (b, h))
    return pl.pallas_call(
        functools.partial(_attn_body, nb=nb, scale=hd ** -0.5),
        grid=(bp, n_heads),
        in_specs=[blk, blk, blk],
        out_specs=blk,
        out_shape=jax.ShapeDtypeStruct((bp * seq, n_heads * hd), BF16),
        scratch_shapes=[pltpu.VMEM((nbp, hd), F32), pltpu.VMEM((seq, hd), BF16), pltpu.VMEM((hd, seq), BF16)],
        compiler_params=_params(("arbitrary", "arbitrary"), 48),
        name="moba_prompt",
    )(q, k_p, v_p)


def _page_mean_body(pt_ref, *refs, pages_per_step, pages_per_block):
    del pt_ref
    pages, o_ref = refs[:pages_per_step], refs[pages_per_step]
    inv = 1.0 / MOBA_BLOCK
    for c in range(pages_per_step // pages_per_block):
        tot = jnp.sum(pages[c * pages_per_block][...], axis=0)
        for r in range(1, pages_per_block):
            tot = tot + jnp.sum(pages[c * pages_per_block + r][...], axis=0)
        o_ref[c] = tot * inv


def _page_means(cache_k, layer, page_table):
    _, _, page, n_heads, hd = cache_k.shape
    bs, n_pages = page_table.shape
    ppb = MOBA_BLOCK // page
    pps = _pick(n_pages, (8, 4, 2)) if ppb == 2 else ppb
    n_blocks = n_pages // ppb

    def page_spec(c):
        return pl.BlockSpec((None, None, page, n_heads, hd),
                            lambda b, s, pt: (layer, pt[b * n_pages + s * pps + c], 0, 0, 0))

    return pl.pallas_call(
        functools.partial(_page_mean_body, pages_per_step=pps, pages_per_block=ppb),
        grid_spec=pltpu.PrefetchScalarGridSpec(
            num_scalar_prefetch=1,
            grid=(bs, n_pages // pps),
            in_specs=[page_spec(c) for c in range(pps)],
            out_specs=pl.BlockSpec((None, pps // ppb, n_heads, hd), lambda b, s, pt: (b, s, 0, 0)),
        ),
        out_shape=jax.ShapeDtypeStruct((bs, n_blocks, n_heads, hd), F32),
        compiler_params=_params(("arbitrary", "arbitrary"), 40),
        name="page_means",
    )(page_table.reshape(-1), *([cache_k] * pps))


def _select_body(km_ref, q_ref, o_ref, *, n_heads):
    n_blocks = km_ref.shape[1]
    res = jnp.zeros(o_ref.shape, I32)
    row = lax.broadcasted_iota(I32, o_ref.shape, 0)
    lane = lax.broadcasted_iota(I32, o_ref.shape, 1)
    blk_id = lax.broadcasted_iota(I32, (n_blocks, 1), 0)
    for h in range(n_heads):
        gate = jnp.sum(km_ref[h] * q_ref[h:h + 1, :], axis=1, keepdims=True)
        for k in range(MOBA_TOPK):
            best = jnp.max(gate, axis=0, keepdims=True)
            idx = jnp.min(jnp.where(gate == best, blk_id, n_blocks), axis=0, keepdims=True)
            res = jnp.where((row == k) & (lane == h), idx, res)
            gate = jnp.where(blk_id == idx, -jnp.inf, gate)
    o_ref[...] = res


def _select_blocks(kmean_t, q_s):
    bs, n_heads, n_blocks, hd = kmean_t.shape
    return pl.pallas_call(
        functools.partial(_select_body, n_heads=n_heads),
        grid=(bs,),
        in_specs=[pl.BlockSpec((None, n_heads, n_blocks, hd), lambda b: (b, 0, 0, 0)),
                  pl.BlockSpec((None, n_heads, hd), lambda b: (b, 0, 0))],
        out_specs=pl.BlockSpec((None, SUBLANES, LANES), lambda b: (b, 0, 0)),
        out_shape=jax.ShapeDtypeStruct((bs, SUBLANES, LANES), I32),
        compiler_params=_params(("arbitrary",), 32),
        name="moba_select",
    )(kmean_t, q_s.reshape(bs, n_heads, hd))


def _decode_attn_body(ph_ref, q_ref, kn_ref, vn_ref, ck_hbm, cv_hbm, o_ref, kbuf, vbuf, sem,
                      *, layer, bs, n_heads, n_sel_pages, scale):
    t = pl.program_id(0)
    n_steps = pl.num_programs(0)

    def page_copies(step, slot):
        h, b = step // bs, step % bs
        out = []
        for r in range(n_sel_pages):
            pg = ph_ref[(b * n_heads + h) * n_sel_pages + r]
            out.append(pltpu.make_async_copy(ck_hbm.at[layer, pg, :, h, :], kbuf.at[slot, r], sem.at[slot]))
            out.append(pltpu.make_async_copy(cv_hbm.at[layer, pg, :, h, :], vbuf.at[slot, r], sem.at[slot]))
        return out

    @pl.when(t == 0)
    def _first():
        for c in page_copies(t, 0):
            c.start()

    @pl.when(t + 1 < n_steps)
    def _prefetch():
        for c in page_copies(t + 1, (t + 1) % 2):
            c.start()

    slot = t % 2
    for c in page_copies(t, slot):
        c.wait()

    row = pl.ds(t % bs, 1)
    q = q_ref[row, :]
    s_new = jnp.sum(q * kn_ref[row, :], axis=1, keepdims=True) * scale
    scores = [jnp.sum(kbuf[slot, r] * q, axis=1, keepdims=True) * scale for r in range(n_sel_pages)]
    top = s_new
    for s in scores:
        top = jnp.maximum(top, jnp.max(s, axis=0, keepdims=True))
    p_new = jnp.exp(s_new - top)
    denom = p_new
    acc = p_new * vn_ref[row, :]
    for r, s in enumerate(scores):
        p = jnp.exp(s - top)
        denom = denom + jnp.sum(p, axis=0, keepdims=True)
        acc = acc + jnp.sum(p * vbuf[slot, r], axis=0, keepdims=True)
    o_ref[row, :] = acc / denom


def _decode_attn(q_s, k_s, v_s, cache_k, cache_v, layer, phys_pages, *, n_heads, hd):
    bs = q_s.shape[0]
    page = cache_k.shape[2]
    nsp = phys_pages.shape[-1]
    head_rows = pl.BlockSpec((bs, hd), lambda t, ph: (0, t // bs))
    anywhere = pl.BlockSpec(memory_space=pl.ANY)
    return pl.pallas_call(
        functools.partial(_decode_attn_body, layer=layer, bs=bs, n_heads=n_heads, n_sel_pages=nsp,
                          scale=hd ** -0.5),
        grid_spec=pltpu.PrefetchScalarGridSpec(
            num_scalar_prefetch=1,
            grid=(n_heads * bs,),
            in_specs=[head_rows, head_rows, head_rows, anywhere, anywhere],
            out_specs=head_rows,
            scratch_shapes=[pltpu.VMEM((2, nsp, page, hd), F32), pltpu.VMEM((2, nsp, page, hd), F32),
                            pltpu.SemaphoreType.DMA((2,))],
        ),
        out_shape=jax.ShapeDtypeStruct((bs, n_heads * hd), F32),
        compiler_params=_params(("arbitrary",), 32),
        name="moba_decode",
    )(phys_pages.reshape(-1), q_s, k_s, v_s, cache_k, cache_v)


def _gelu_tanh(x):
    return 0.5 * x * (1.0 + jnp.tanh(np.sqrt(2.0 / np.pi).astype(np.float32) * (x + 0.044715 * (x * x * x))))


def _softplus(x):
    return jnp.maximum(x, 0.0) + jnp.log1p(jnp.exp(-jnp.abs(x)))


def _lru_gates(uc, wa_ref, wx_ref, ba, bx, lam, n_heads, rb):
    ucb = uc.astype(BF16)
    a_parts, b_parts = [], []
    sp = _softplus(-lam)
    for h in range(n_heads):
        cols = slice(h * rb, (h + 1) * rb)
        r = jax.nn.sigmoid(jnp.dot(ucb[:, cols], wa_ref[h].astype(BF16), preferred_element_type=F32) + ba[:, cols])
        g = jax.nn.sigmoid(jnp.dot(ucb[:, cols], wx_ref[h].astype(BF16), preferred_element_type=F32) + bx[:, cols])
        log_a = -LRU_C * r * sp[:, cols]
        a_parts.append(jnp.exp(log_a))
        b_parts.append(jnp.sqrt(1.0 - jnp.exp(2.0 * log_a)) * (g * uc[:, cols]))
    return jnp.concatenate(a_parts, axis=1), jnp.concatenate(b_parts, axis=1)


def _lru_body(g_ref, u_ref, wc_ref, bc_ref, wa_ref, wx_ref, ba_ref, bx_ref, lam_ref,
              c0_ref, c1_ref, c2_ref, h0_ref,
              y_ref, newc_ref, newh_ref, us_ref, hs_ref,
              carry_ref, hcarry_ref, a_buf, b_buf, *, tiles_per_seq, n_prompt_tiles, bs, n_heads, rb):
    i = pl.program_id(0)
    tm = y_ref.shape[0]
    w0, w1, w2, w3 = wc_ref[0:1, :], wc_ref[1:2, :], wc_ref[2:3, :], wc_ref[3:4, :]
    bias = bc_ref[...]
    gates = functools.partial(_lru_gates, wa_ref=wa_ref, wx_ref=wx_ref, ba=ba_ref[...], bx=bx_ref[...],
                              lam=lam_ref[...], n_heads=n_heads, rb=rb)

    @pl.when(i < n_prompt_tiles)
    def _prompt():
        u = u_ref[...]
        fresh = i % tiles_per_seq == 0
        c0, c1, c2 = (jnp.where(fresh, 0.0, carry_ref[r:r + 1, :]) for r in range(3))
        p1 = _shift_rows(u, 1, (c2,))
        p2 = _shift_rows(u, 2, (c1, c2))
        p3 = _shift_rows(u, 3, (c0, c1, c2))
        uc = w0 * p3 + w1 * p2 + w2 * p1 + w3 * u + bias
        a, b = gates(uc)
        a_buf[...] = a
        b_buf[...] = b
        sub = lax.broadcasted_iota(I32, (SUBLANES, a.shape[1]), 0)

        def group(gi, h):
            rows = pl.ds(pl.multiple_of(gi * SUBLANES, SUBLANES), SUBLANES)
            ag, bg = a_buf[rows, :], b_buf[rows, :]
            for d in (1, 2, 4):
                bg = bg + ag * jnp.where(sub >= d, pltpu.roll(bg, d, 0), 0.0)
                ag = ag * jnp.where(sub >= d, pltpu.roll(ag, d, 0), 1.0)
            hg = ag * h + bg
            b_buf[rows, :] = hg
            return hg[SUBLANES - 1:SUBLANES, :]

        h_last = lax.fori_loop(0, tm // SUBLANES, group, jnp.where(fresh, 0.0, hcarry_ref[0:1, :]))
        hcarry_ref[0:1, :] = h_last
        y_ref[...] = (b_buf[...] * _gelu_tanh(g_ref[...])).astype(y_ref.dtype)
        carry_ref[0:3, :] = u[tm - 3:tm, :]
        newc_ref[...] = u[tm - 3:tm, :]
        newh_ref[...] = h_last

    @pl.when(i == n_prompt_tiles)
    def _sample():
        u = u_ref[0:bs, :]
        uc = w0 * c0_ref[...] + w1 * c1_ref[...] + w2 * c2_ref[...] + w3 * u + bias
        a, b = gates(uc)
        h = a * h0_ref[...] + b
        y_ref[...] = jnp.zeros(y_ref.shape, y_ref.dtype)
        y_ref[0:bs, :] = (h * _gelu_tanh(g_ref[0:bs, :])).astype(y_ref.dtype)
        us_ref[...] = u
        hs_ref[...] = h


def _rglru(proj, w_conv, b_conv, w_gate_a, b_gate_a, w_gate_x, b_gate_x, lam, layer, conv_state, h0,
           *, bp, seq, bs):
    m_tot = proj.shape[0]
    dr = proj.shape[1] // 2
    n_heads, rb = w_gate_a.shape[1], w_gate_a.shape[2]
    tm = TOKEN_TILE
    tps = seq // tm
    npt = bp * tps
    width = w_conv.shape[1]
    col = lambda c: pl.BlockSpec((tm, dr), lambda i: (i, c))
    vec = pl.BlockSpec((None, 1, dr), lambda i: (layer, 0, 0))
    gw = pl.BlockSpec((None, n_heads, rb, rb), lambda i: (layer, 0, 0, 0))
    st = pl.BlockSpec((bs, dr), lambda i: (0, 0))
    pb = lambda rows: pl.BlockSpec((None, rows, dr), lambda i: (jnp.minimum(i // tps, bp - 1), 0, 0))
    return pl.pallas_call(
        functools.partial(_lru_body, tiles_per_seq=tps, n_prompt_tiles=npt, bs=bs, n_heads=n_heads, rb=rb),
        grid=(m_tot // tm,),
        in_specs=[col(0), col(1), pl.BlockSpec((None, width, dr), lambda i: (layer, 0, 0)), vec,
                  gw, gw, vec, vec, vec, st, st, st, st],
        out_specs=[pl.BlockSpec((tm, dr), lambda i: (i, 0)), pb(width - 1), pb(1), st, st],
        out_shape=[jax.ShapeDtypeStruct((m_tot, dr), BF16),
                   jax.ShapeDtypeStruct((bp, width - 1, dr), F32),
                   jax.ShapeDtypeStruct((bp, 1, dr), F32),
                   jax.ShapeDtypeStruct((bs, dr), F32),
                   jax.ShapeDtypeStruct((bs, dr), F32)],
        scratch_shapes=[pltpu.VMEM((SUBLANES, dr), F32), pltpu.VMEM((SUBLANES, dr), F32),
                        pltpu.VMEM((tm, dr), F32), pltpu.VMEM((tm, dr), F32)],
        compiler_params=_params(("arbitrary",), 48),
        name="rglru",
    )(proj, proj, w_conv, b_conv.reshape(-1, 1, dr), w_gate_a, w_gate_x,
      b_gate_a.reshape(-1, 1, dr), b_gate_x.reshape(-1, 1, dr), lam.reshape(-1, 1, dr),
      conv_state[:, 0], conv_state[:, 1], conv_state[:, 2], h0)


def _ffn_body(e_ref, start_ref, nt_ref, nused_ref, xs_hbm, wg_ref, wu_ref, wd_ref, o_hbm,
              xbuf, acc, wgb, wub, wdb, sem_in, sem_out, *, tm, nf):
    del e_ref
    s, f = pl.program_id(0), pl.program_id(1)
    nt = nt_ref[s]
    start = start_ref[s]

    def copy_in(r):
        return pltpu.make_async_copy(xs_hbm.at[pl.ds(pl.multiple_of(start + r * tm, tm), tm), :],
                                     xbuf.at[pl.ds(pl.multiple_of(r * tm, tm), tm), :], sem_in)

    def copy_out(r):
        return pltpu.make_async_copy(acc.at[pl.ds(pl.multiple_of(r * tm, tm), tm), :],
                                     o_hbm.at[pl.ds(pl.multiple_of(start + r * tm, tm), tm), :], sem_out)

    def for_tiles(fn):
        def body(r, c):
            fn(r)
            return c
        lax.fori_loop(0, nt, body, 0)

    @pl.when(nt > 0)
    def _run():
        @pl.when(f == 0)
        def _load():
            for_tiles(lambda r: copy_in(r).start())

            def clear(r):
                acc[pl.ds(pl.multiple_of(r * tm, tm), tm), :] = jnp.zeros((tm, acc.shape[1]), F32)

            for_tiles(clear)
            for_tiles(lambda r: copy_in(r).wait())

        wgb[...] = wg_ref[...].astype(BF16)
        wub[...] = wu_ref[...].astype(BF16)
        wdb[...] = wd_ref[...].astype(BF16)

        def rows_step(first_tile, n_tiles):
            rows = pl.ds(pl.multiple_of(first_tile * tm, tm), n_tiles * tm)
            x = xbuf[rows, :]
            g = jnp.dot(x, wgb[...], preferred_element_type=F32)
            u = jnp.dot(x, wub[...], preferred_element_type=F32)
            h = (g * jax.nn.sigmoid(g) * u).astype(BF16)
            acc[rows, :] += jnp.dot(h, wdb[...], preferred_element_type=F32)

        def pair(i, c):
            rows_step(2 * i, 2)
            return c

        lax.fori_loop(0, nt // 2, pair, 0)

        @pl.when(nt % 2 == 1)
        def _last_single():
            rows_step(nt - 1, 1)

        @pl.when(f == nf - 1)
        def _store():
            for_tiles(lambda r: copy_out(r).start())
            for_tiles(lambda r: copy_out(r).wait())

    @pl.when((s == pl.num_programs(0) - 1) & (f == nf - 1))
    def _zero_unused():
        acc[0:tm, :] = jnp.zeros((tm, acc.shape[1]), F32)

        def zero_tile(t, c):
            cp = pltpu.make_async_copy(acc.at[0:tm, :], o_hbm.at[pl.ds(pl.multiple_of(t * tm, tm), tm), :], sem_out)
            cp.start()
            cp.wait()
            return c

        lax.fori_loop(nused_ref[0], o_hbm.shape[0] // tm, zero_tile, 0)


def _ffn(xs, w_gate, w_up, w_down, layer, sched_e, sched_start, sched_nt, n_used_tiles):
    s_rows, d = xs.shape
    dff = w_gate.shape[3]
    tm, tf, rt = TOKEN_TILE, FFN_COL_TILE, FFN_SUPER_TILES
    nf = dff // tf
    n_super = sched_e.shape[0]

    def fcol(s, f, nt_ref):
        return jnp.where(nt_ref[s] > 0, f, nf - 1)

    up_spec = pl.BlockSpec((None, None, d, tf), lambda s, f, e, st, nt, nu: (layer, e[s], 0, fcol(s, f, nt)))
    down_spec = pl.BlockSpec((None, None, tf, d), lambda s, f, e, st, nt, nu: (layer, e[s], fcol(s, f, nt), 0))
    return pl.pallas_call(
        functools.partial(_ffn_body, tm=tm, nf=nf),
        grid_spec=pltpu.PrefetchScalarGridSpec(
            num_scalar_prefetch=4,
            grid=(n_super, nf),
            in_specs=[pl.BlockSpec(memory_space=pl.ANY), up_spec, up_spec, down_spec],
            out_specs=pl.BlockSpec(memory_space=pl.ANY),
            scratch_shapes=[pltpu.VMEM((rt * tm, d), BF16), pltpu.VMEM((rt * tm, d), F32),
                            pltpu.VMEM((d, tf), BF16), pltpu.VMEM((d, tf), BF16), pltpu.VMEM((tf, d), BF16),
                            pltpu.SemaphoreType.DMA(()), pltpu.SemaphoreType.DMA(())],
        ),
        out_shape=jax.ShapeDtypeStruct((s_rows, d), F32),
        compiler_params=_params(("arbitrary", "arbitrary"), 56),
        name="swiglu_ffn",
    )(sched_e, sched_start, sched_nt, n_used_tiles, xs, w_gate, w_up, w_down)


def _dense_schedule(n_tiles):
    rt = FFN_SUPER_TILES
    n_super = -(-n_tiles // rt)
    start = np.arange(n_super, dtype=np.int32) * rt
    nt = np.minimum(rt, n_tiles - start).astype(np.int32)
    return (jnp.zeros((n_super,), I32), jnp.asarray(start * TOKEN_TILE, I32), jnp.asarray(nt, I32),
            jnp.full((1,), n_tiles, I32))


def _router_body(x_ref, w_ref, o_ref, *, n_exp):
    logits = jnp.dot(x_ref[...], w_ref[...], precision=lax.Precision.HIGHEST, preferred_element_type=F32)
    lane = lax.broadcasted_iota(I32, logits.shape, 1)
    lg = jnp.where(lane < n_exp, logits, -jnp.inf)
    m1 = jnp.max(lg, axis=1, keepdims=True)
    i1 = jnp.min(jnp.where(lg == m1, lane, LANES), axis=1, keepdims=True)
    lg = jnp.where(lane == i1, -jnp.inf, lg)
    m2 = jnp.max(lg, axis=1, keepdims=True)
    i2 = jnp.min(jnp.where(lg == m2, lane, LANES), axis=1, keepdims=True)
    e = jnp.exp(m2 - m1)
    g1 = 1.0 / (1.0 + e)
    g2 = e / (1.0 + e)
    o_ref[...] = jnp.where(lane == 0, i1.astype(F32),
                           jnp.where(lane == 1, i2.astype(F32),
                                     jnp.where(lane == 2, g1, jnp.where(lane == 3, g2, 0.0))))


def _router(x, w_router, layer):
    m, d = x.shape
    n_exp = w_router.shape[2]
    w_pad = jnp.pad(w_router, ((0, 0), (0, 0), (0, LANES - n_exp)))
    tm = TOKEN_TILE
    return pl.pallas_call(
        functools.partial(_router_body, n_exp=n_exp),
        grid=(m // tm,),
        in_specs=[pl.BlockSpec((tm, d), lambda i: (i, 0)),
                  pl.BlockSpec((None, d, LANES), lambda i: (layer, 0, 0))],
        out_specs=pl.BlockSpec((tm, LANES), lambda i: (i, 0)),
        out_shape=jax.ShapeDtypeStruct((m, LANES), F32),
        compiler_params=_params(("arbitrary",), 32),
        name="moe_router",
    )(x, w_pad)


def _moe_plan(route, n_valid, n_exp):
    m_tot = route.shape[0]
    tm, rt = TOKEN_TILE, FFN_SUPER_TILES
    n_tiles_max = (MOE_TOP_K * n_valid + n_exp * (tm - 1)) // tm
    s_rows = n_tiles_max * tm
    n_super = n_tiles_max // rt + n_exp
    e1, e2 = route[:, 0].astype(I32), route[:, 1].astype(I32)
    tok = jnp.arange(m_tot, dtype=I32)
    valid = tok < n_valid
    ids = jnp.arange(n_exp, dtype=I32)[None, :]
    oh1 = (e1[:, None] == ids) & valid[:, None]
    oh2 = (e2[:, None] == ids) & valid[:, None]
    oh = oh1.astype(I32) + oh2.astype(I32)
    csum = jnp.cumsum(oh, axis=0)
    before = csum - oh
    counts = csum[-1]
    tiles_e = (counts + tm - 1) // tm
    tile_end = jnp.cumsum(tiles_e)
    tile_off = tile_end - tiles_e
    row_off = tile_off * tm
    pos1 = jnp.sum(jnp.where(oh1, before + row_off[None, :], 0), axis=1)
    pos2 = jnp.sum(jnp.where(oh2, before + row_off[None, :], 0), axis=1)
    drop = jnp.where(valid, 0, s_rows)
    slot_token = jnp.zeros((s_rows,), I32)
    slot_token = slot_token.at[pos1 + drop].set(tok, mode="drop").at[pos2 + drop].set(tok, mode="drop")
    n_used_tiles = tile_end[-1:]
    super_e = -(-tiles_e // rt)
    super_end = jnp.cumsum(super_e)
    super_off = super_end - super_e
    n_used_super = super_end[-1]
    sidx = jnp.arange(n_super, dtype=I32)
    live = sidx < n_used_super
    sclamp = jnp.minimum(sidx, n_used_super - 1)
    exp_of = jnp.sum((sclamp[:, None] >= super_end[None, :]).astype(I32), axis=1)
    k = sclamp - super_off[exp_of]
    sched_start = (tile_off[exp_of] + k * rt) * tm
    sched_nt = jnp.where(live, jnp.clip(tiles_e[exp_of] - k * rt, 0, rt), 0)
    return pos1, pos2, slot_token, n_used_tiles, (exp_of, sched_start, sched_nt), s_rows


def _gather_body(tok_ref, nused_ref, x_hbm, o_ref, buf, sem, *, tm):
    t = pl.program_id(0)

    def row_copy(r):
        return pltpu.make_async_copy(x_hbm.at[pl.ds(tok_ref[t * tm + r], 1), :], buf.at[pl.ds(r, 1), :], sem)

    @pl.when(t < nused_ref[0])
    def _used():
        def issue(g, c):
            for r in range(DMA_ISSUE_UNROLL):
                row_copy(g * DMA_ISSUE_UNROLL + r).start()
            return c

        lax.fori_loop(0, tm // DMA_ISSUE_UNROLL, issue, 0)
        pltpu.make_async_copy(x_hbm.at[pl.ds(0, tm), :], buf, sem).wait()
        o_ref[...] = buf[...].astype(o_ref.dtype)

    @pl.when(t >= nused_ref[0])
    def _unused():
        o_ref[...] = jnp.zeros(o_ref.shape, o_ref.dtype)


def _gather_rows(x, slot_token, n_used_tiles):
    s_rows = slot_token.shape[0]
    d = x.shape[1]
    tm = TOKEN_TILE
    return pl.pallas_call(
        functools.partial(_gather_body, tm=tm),
        grid_spec=pltpu.PrefetchScalarGridSpec(
            num_scalar_prefetch=2,
            grid=(s_rows // tm,),
            in_specs=[pl.BlockSpec(memory_space=pl.ANY)],
            out_specs=pl.BlockSpec((tm, d), lambda t, tok, nu: (t, 0)),
            scratch_shapes=[pltpu.VMEM((tm, d), F32), pltpu.SemaphoreType.DMA(())],
        ),
        out_shape=jax.ShapeDtypeStruct((s_rows, d), BF16),
        compiler_params=_params(("arbitrary",), 32),
        name="moe_gather",
    )(slot_token, n_used_tiles, x)


def _combine_body(p1_ref, p2_ref, x_ref, route_ref, g_ref, b_ref, y_hbm, o_ref, ob_ref, buf, sem, *, tm, alpha):
    t = pl.program_id(0)

    def row_copy(r, which, p_ref):
        return pltpu.make_async_copy(y_hbm.at[pl.ds(p_ref[t * tm + r], 1), :],
                                     buf.at[which, pl.ds(r, 1), :], sem)

    def issue(g, c):
        for r in range(DMA_ISSUE_UNROLL):
            row_copy(g * DMA_ISSUE_UNROLL + r, 0, p1_ref).start()
            row_copy(g * DMA_ISSUE_UNROLL + r, 1, p2_ref).start()
        return c

    lax.fori_loop(0, tm // DMA_ISSUE_UNROLL, issue, 0)
    for which in range(2):
        pltpu.make_async_copy(y_hbm.at[pl.ds(0, tm), :], buf.at[which], sem).wait()
    route = route_ref[...]
    y = route[:, 2:3] * buf[0] + route[:, 3:4] * buf[1]
    out = _layer_norm(alpha * x_ref[...] + y, g_ref[...], b_ref[...])
    o_ref[...] = out
    ob_ref[...] = out.astype(BF16)


def _combine_ln(x, route, y_slots, pos1, pos2, g_stack, b_stack, layer, alpha):
    m, d = x.shape
    tm = TOKEN_TILE
    row = pl.BlockSpec((tm, d), lambda t, p1, p2: (t, 0))
    par = pl.BlockSpec((None, 1, d), lambda t, p1, p2: (layer, 0, 0))
    return pl.pallas_call(
        functools.partial(_combine_body, tm=tm, alpha=alpha),
        grid_spec=pltpu.PrefetchScalarGridSpec(
            num_scalar_prefetch=2,
            grid=(m // tm,),
            in_specs=[row, pl.BlockSpec((tm, LANES), lambda t, p1, p2: (t, 0)), par, par,
                      pl.BlockSpec(memory_space=pl.ANY)],
            out_specs=[row, row],
            scratch_shapes=[pltpu.VMEM((2, tm, d), F32), pltpu.SemaphoreType.DMA(())],
        ),
        out_shape=[jax.ShapeDtypeStruct((m, d), F32), jax.ShapeDtypeStruct((m, d), BF16)],
        compiler_params=_params(("arbitrary",), 32),
        name="moe_combine_ln",
    )(pos1, pos2, x, route, g_stack.reshape(-1, 1, d), b_stack.reshape(-1, 1, d), y_slots)


def kernel(x_prompt, x_sample, cache_k, cache_v, page_table, state_conv, state_lru_conv, state_lru_h, ln1_g, ln1_b, ln2_g, ln2_b, sc_w_in, sc_w_conv, sc_w_out, attn_w_qkv, attn_w_o, lru_w_in, lru_w_conv, lru_b_conv, lru_w_gate_a, lru_b_gate_a, lru_w_gate_x, lru_b_gate_x, lru_lambda, lru_w_out, ffn_w_gate, ffn_w_up, ffn_w_down, moe_w_router, moe_w_gate, moe_w_up, moe_w_down):
    bp, seq, d = x_prompt.shape
    bs, dec_seq, _ = x_sample.shape
    depth = ln1_g.shape[0]
    n_heads, hd = cache_k.shape[3], cache_k.shape[4]
    page = cache_k.shape[2]
    past_len = page_table.shape[1] * page
    n_exp = moe_w_router.shape[2]
    assert dec_seq == 1 and bs <= TOKEN_TILE
    assert seq % MOBA_BLOCK == 0 and seq % TOKEN_TILE == 0 and past_len % MOBA_BLOCK == 0
    assert MOBA_BLOCK % page == 0 and past_len // MOBA_BLOCK >= MOBA_TOPK
    alpha = float((2.0 * depth) ** 0.25)
    mp = bp * seq
    n_valid = mp + bs
    m_tot = -(-n_valid // TOKEN_TILE) * TOKEN_TILE
    ppb = MOBA_BLOCK // page

    x = jnp.concatenate([x_prompt.reshape(mp, d), x_sample.reshape(bs, d),
                         jnp.zeros((m_tot - n_valid, d), x_prompt.dtype)], axis=0)
    xb = x.astype(BF16)
    dense_sched = _dense_schedule(m_tot // TOKEN_TILE)

    conv_p, conv_s, kp_l, vp_l, ks_l, vs_l, lc_p, lc_s, lh_p, lh_s = ([] for _ in range(10))
    for i in range(depth):
        kind, m = i % 3, i // 3
        if kind == 0:
            proj = _matmul(xb, sc_w_in, m)
            y, new_p, u_s = _short_conv(proj, sc_w_conv, m, state_conv[m, :, 0], state_conv[m, :, 1],
                                        bp=bp, seq=seq, bs=bs)
            conv_p.append(new_p)
            conv_s.append(jnp.stack([state_conv[m, :, 1], u_s], axis=1))
            mix = _matmul(y, sc_w_out, m)
        elif kind == 1:
            qkv = _matmul(xb, attn_w_qkv, m)
            q, k_p, v_p, k_s, v_s = _rope(qkv, n_heads=n_heads, hd=hd, bp=bp, seq=seq, bs=bs, past_len=past_len)
            kp_l.append(k_p.reshape(bp, seq, n_heads, hd))
            vp_l.append(v_p.reshape(bp, seq, n_heads, hd))
            ks_l.append(k_s.reshape(bs, 1, n_heads, hd))
            vs_l.append(v_s.reshape(bs, 1, n_heads, hd))
            o_p = _attn_prompt(q, k_p, v_p, bp=bp, seq=seq, n_heads=n_heads, hd=hd)
            q_s = q[mp:mp + bs]
            kmean = _page_means(cache_k, m, page_table)
            sel = _select_blocks(kmean.transpose(0, 2, 1, 3), q_s)
            sel = sel[:, :MOBA_TOPK, :n_heads].transpose(0, 2, 1)
            logical = sel[..., None] * ppb + jnp.arange(ppb, dtype=I32)
            phys = jnp.take_along_axis(page_table, logical.reshape(bs, -1), axis=1)
            o_s = _decode_attn(q_s, k_s, v_s, cache_k, cache_v, m, phys.reshape(bs, n_heads, -1),
                               n_heads=n_heads, hd=hd)
            o = jnp.concatenate([o_p, o_s.astype(BF16), jnp.zeros((m_tot - n_valid, n_heads * hd), BF16)], axis=0)
            mix = _matmul(o, attn_w_o, m)
        else:
            proj = _matmul(xb, lru_w_in, m)
            y, new_c, new_h, u_s, h_s = _rglru(proj, lru_w_conv, lru_b_conv, lru_w_gate_a, lru_b_gate_a,
                                               lru_w_gate_x, lru_b_gate_x, lru_lambda, m,
                                               state_lru_conv[m], state_lru_h[m], bp=bp, seq=seq, bs=bs)
            lc_p.append(new_c)
            lc_s.append(jnp.concatenate([state_lru_conv[m, :, 1:], u_s[:, None, :]], axis=1))
            lh_p.append(new_h.reshape(bp, -1))
            lh_s.append(h_s)
            mix = _matmul(y, lru_w_out, m)
        x, xb = _add_ln(x, mix, ln1_g, ln1_b, i, alpha)
        j = i // 2
        if i % 2 == 0:
            f = _ffn(xb, ffn_w_gate[:, None], ffn_w_up[:, None], ffn_w_down[:, None], j, *dense_sched)
            x, xb = _add_ln(x, f, ln2_g, ln2_b, i, alpha)
        else:
            route = _router(x, moe_w_router, j)
            pos1, pos2, slot_token, n_used, sched, _ = _moe_plan(route, n_valid, n_exp)
            xs = _gather_rows(x, slot_token, n_used)
            ys = _ffn(xs, moe_w_gate, moe_w_up, moe_w_down, j, *sched, n_used)
            x, xb = _combine_ln(x, route, ys, pos1, pos2, ln2_g, ln2_b, i, alpha)
    return (x[:mp].reshape(bp, seq, d), x[mp:mp + bs].reshape(bs, 1, d),
            jnp.stack(conv_p), jnp.stack(conv_s),
            jnp.stack(kp_l), jnp.stack(vp_l), jnp.stack(ks_l), jnp.stack(vs_l),
            jnp.stack(lc_p), jnp.stack(lc_s), jnp.stack(lh_p), jnp.stack(lh_s))
```

```python
import functools

import numpy as np
import jax
import jax.numpy as jnp
from jax import lax
from jax.experimental import pallas as pl
from jax.experimental.pallas import tpu as pltpu

F32, BF16, I32 = jnp.float32, jnp.bfloat16, jnp.int32

MOBA_BLOCK = 256
MOBA_TOPK = 3
ROPE_THETA = 500000.0
ROT_FRACTION = 4
LRU_C = 8.0
MOE_TOP_K = 2
LN_EPS = 1e-5
NEG_INF = -1e30

LANES = 128
SUBLANES = 8
V7X_VMEM_BYTES = 64 * 1024 * 1024

TOKEN_TILE = 256
FFN_COL_TILE = 256
FFN_SUPER_TILES = 10
DMA_ISSUE_UNROLL = 8


def _params(sem, vmem_mib, **kw):
    return pltpu.CompilerParams(dimension_semantics=sem, vmem_limit_bytes=vmem_mib * 1024 * 1024, **kw)


def _pick(total, candidates):
    for c in candidates:
        if total % c == 0:
            return c
    raise ValueError(f"no tile in {candidates} divides {total}")


def _mm_body(x_ref, w_ref, o_ref, wb_ref):
    @pl.when(pl.program_id(1) == 0)
    def _cast():
        wb_ref[...] = w_ref[...].astype(BF16)

    o_ref[...] = jnp.dot(x_ref[...], wb_ref[...], preferred_element_type=F32).astype(o_ref.dtype)


def _matmul(xb, w_stack, layer, out_dtype=F32):
    m, k = xb.shape
    n = w_stack.shape[2]
    tm = _pick(m, (768, 512, 256))
    tn = _pick(n, (1024, 512, 256, 128))
    return pl.pallas_call(
        _mm_body,
        grid=(n // tn, m // tm),
        in_specs=[pl.BlockSpec((tm, k), lambda j, i: (i, 0)),
                  pl.BlockSpec((None, k, tn), lambda j, i: (layer, 0, j))],
        out_specs=pl.BlockSpec((tm, tn), lambda j, i: (i, j)),
        out_shape=jax.ShapeDtypeStruct((m, n), out_dtype),
        scratch_shapes=[pltpu.VMEM((k, tn), BF16)],
        compiler_params=_params(("arbitrary", "arbitrary"), 48),
        name="matmul",
    )(xb, w_stack)


def _layer_norm(z, g, b):
    mu = jnp.mean(z, axis=-1, keepdims=True)
    zc = z - mu
    var = jnp.mean(zc * zc, axis=-1, keepdims=True)
    return zc * lax.rsqrt(var + LN_EPS) * g + b


def _mm_ln_body(y_ref, w_ref, x_ref, g_ref, b_ref, o_ref, ob_ref, wb_ref, *, alpha):
    @pl.when(pl.program_id(0) == 0)
    def _cast():
        wb_ref[...] = w_ref[...].astype(BF16)

    mix = jnp.dot(y_ref[...], wb_ref[...], preferred_element_type=F32)
    out = _layer_norm(alpha * x_ref[...] + mix, g_ref[...], b_ref[...])
    o_ref[...] = out
    ob_ref[...] = out.astype(BF16)


def _matmul_add_ln(yb, w_stack, w_layer, x, g_stack, b_stack, layer, alpha):
    m, k = yb.shape
    d = x.shape[1]
    assert w_stack.shape[2] == d
    tm = TOKEN_TILE
    row = lambda width: pl.BlockSpec((tm, width), lambda i: (i, 0))
    par = pl.BlockSpec((None, 1, d), lambda i: (layer, 0, 0))
    return pl.pallas_call(
        functools.partial(_mm_ln_body, alpha=alpha),
        grid=(m // tm,),
        in_specs=[row(k),
                  pl.BlockSpec((None, k, d), lambda i: (w_layer, 0, 0), pipeline_mode=pl.Buffered(1)),
                  row(d), par, par],
        out_specs=[row(d), row(d)],
        out_shape=[jax.ShapeDtypeStruct((m, d), F32), jax.ShapeDtypeStruct((m, d), BF16)],
        scratch_shapes=[pltpu.VMEM((k, d), BF16)],
        compiler_params=_params(("arbitrary",), 48),
        name="matmul_add_ln",
    )(yb, w_stack, x, g_stack.reshape(-1, 1, d), b_stack.reshape(-1, 1, d))


def _add_ln_body(x_ref, y_ref, g_ref, b_ref, o_ref, ob_ref, *, alpha):
    out = _layer_norm(alpha * x_ref[...] + y_ref[...], g_ref[...], b_ref[...])
    o_ref[...] = out
    ob_ref[...] = out.astype(BF16)


def _add_ln(x, y, g_stack, b_stack, layer, alpha):
    m, d = x.shape
    tm = TOKEN_TILE
    row = pl.BlockSpec((tm, d), lambda i: (i, 0))
    par = pl.BlockSpec((None, 1, d), lambda i: (layer, 0, 0))
    return pl.pallas_call(
        functools.partial(_add_ln_body, alpha=alpha),
        grid=(m // tm,),
        in_specs=[row, row, par, par],
        out_specs=[row, row],
        out_shape=[jax.ShapeDtypeStruct((m, d), F32), jax.ShapeDtypeStruct((m, d), BF16)],
        compiler_params=_params(("arbitrary",), 32),
        name="add_ln",
    )(x, y, g_stack.reshape(-1, 1, d), b_stack.reshape(-1, 1, d))


def _shift_rows(u, k, carry_rows):
    out = pltpu.roll(u, k, 0)
    row = lax.broadcasted_iota(I32, u.shape, 0)
    for r, c in enumerate(carry_rows):
        out = jnp.where(row == r, c, out)
    return out


def _sconv_body(b_ref, c_ref, h_ref, w_ref, s0_ref, s1_ref, y_ref, newp_ref, news_ref, carry_ref,
                *, tiles_per_seq, n_prompt_tiles, bs):
    i = pl.program_id(0)
    w0, w1, w2 = w_ref[0:1, :], w_ref[1:2, :], w_ref[2:3, :]
    tm = y_ref.shape[0]

    @pl.when(i < n_prompt_tiles)
    def _prompt():
        u = c_ref[...] * h_ref[...]
        fresh = i % tiles_per_seq == 0
        c0 = jnp.where(fresh, 0.0, carry_ref[0:1, :])
        c1 = jnp.where(fresh, 0.0, carry_ref[1:2, :])
        p1 = _shift_rows(u, 1, (c1,))
        p2 = _shift_rows(u, 2, (c0, c1))
        y = w0 * p2 + w1 * p1 + w2 * u
        y_ref[...] = (b_ref[...] * y).astype(y_ref.dtype)
        carry_ref[0:2, :] = u[tm - 2:tm, :]
        newp_ref[...] = u[tm - 2:tm, :]

    @pl.when(i == n_prompt_tiles)
    def _sample():
        u = c_ref[0:bs, :] * h_ref[0:bs, :]
        y = w0 * s0_ref[...] + w1 * s1_ref[...] + w2 * u
        y_ref[...] = jnp.zeros(y_ref.shape, y_ref.dtype)
        y_ref[0:bs, :] = (b_ref[0:bs, :] * y).astype(y_ref.dtype)
        news_ref[...] = u


def _short_conv(proj, w_conv, layer, s0, s1, *, bp, seq, bs):
    m_tot = proj.shape[0]
    d = proj.shape[1] // 3
    tm = TOKEN_TILE
    tps = seq // tm
    npt = bp * tps
    col = lambda c: pl.BlockSpec((tm, d), lambda i: (i, c))
    full = lambda shape: pl.BlockSpec(shape, lambda i: (0,) * len(shape))
    return pl.pallas_call(
        functools.partial(_sconv_body, tiles_per_seq=tps, n_prompt_tiles=npt, bs=bs),
        grid=(m_tot // tm,),
        in_specs=[col(0), col(1), col(2),
                  pl.BlockSpec((None, w_conv.shape[1], d), lambda i: (layer, 0, 0)),
                  full((bs, d)), full((bs, d))],
        out_specs=[pl.BlockSpec((tm, d), lambda i: (i, 0)),
                   pl.BlockSpec((None, 2, d), lambda i: (jnp.minimum(i // tps, bp - 1), 0, 0)),
                   full((bs, d))],
        out_shape=[jax.ShapeDtypeStruct((m_tot, d), BF16),
                   jax.ShapeDtypeStruct((bp, 2, d), F32),
                   jax.ShapeDtypeStruct((bs, d), F32)],
        scratch_shapes=[pltpu.VMEM((SUBLANES, d), F32)],
        compiler_params=_params(("arbitrary",), 40),
        name="short_conv",
    )(proj, proj, proj, w_conv, s0, s1)


def _rope_head(x, cos_f, sin_f, half):
    lane = lax.broadcasted_iota(I32, x.shape, 1)
    partner = jnp.where(lane < half, pltpu.roll(x, LANES - half, 1), pltpu.roll(x, half, 1))
    return x * cos_f + partner * sin_f


def _rope_body(q_ref, k_ref, v_ref, cos_ref, sin_ref, cos_s_ref, sin_s_ref,
               qo_ref, kp_ref, vp_ref, ks_ref, vs_ref, *, n_heads, hd, half, n_prompt_tiles, bs):
    i = pl.program_id(0)

    def rotate(x_ref, rows, cos_f, sin_f):
        return [
            _rope_head(x_ref[rows, h * hd:(h + 1) * hd], cos_f, sin_f, half) for h in range(n_heads)
        ]

    @pl.when(i < n_prompt_tiles)
    def _prompt():
        rows = slice(None)
        cos_f, sin_f = cos_ref[...], sin_ref[...]
        for h, (qh, kh) in enumerate(zip(rotate(q_ref, rows, cos_f, sin_f), rotate(k_ref, rows, cos_f, sin_f))):
            qo_ref[:, h * hd:(h + 1) * hd] = qh
            kp_ref[:, h * hd:(h + 1) * hd] = kh
        vp_ref[...] = v_ref[...]

    @pl.when(i == n_prompt_tiles)
    def _sample():
        rows = slice(0, bs)
        cos_f, sin_f = cos_s_ref[...], sin_s_ref[...]
        qo_ref[...] = jnp.zeros(qo_ref.shape, qo_ref.dtype)
        for h, (qh, kh) in enumerate(zip(rotate(q_ref, rows, cos_f, sin_f), rotate(k_ref, rows, cos_f, sin_f))):
            qo_ref[0:bs, h * hd:(h + 1) * hd] = qh
            ks_ref[:, h * hd:(h + 1) * hd] = kh
        vs_ref[...] = v_ref[0:bs, :]


def _rope_tables(pos, hd):
    rot = hd // ROT_FRACTION
    half = rot // 2
    inv_freq = ROPE_THETA ** (-jnp.arange(half, dtype=F32) * 2.0 / rot)
    ang = pos.astype(F32)[:, None] * inv_freq[None, :]
    cos, sin = jnp.cos(ang), jnp.sin(ang)
    n = pos.shape[0]
    cos_f = jnp.concatenate([cos, cos, jnp.ones((n, hd - rot), F32)], axis=1)
    sin_f = jnp.concatenate([-sin, sin, jnp.zeros((n, hd - rot), F32)], axis=1)
    return cos_f, sin_f


def _rope(qkv, *, n_heads, hd, bp, seq, bs, past_len):
    m_tot = qkv.shape[0]
    hdim = n_heads * hd
    tm = TOKEN_TILE
    tps = seq // tm
    npt = bp * tps
    mp = bp * seq
    cos_p, sin_p = _rope_tables(jnp.arange(seq, dtype=I32), hd)
    cos_s, sin_s = _rope_tables(jnp.full((1,), past_len, I32), hd)
    col = lambda c: pl.BlockSpec((tm, hdim), lambda i: (i, c))
    tab = pl.BlockSpec((tm, hd), lambda i: (i % tps, 0))
    one = pl.BlockSpec((1, hd), lambda i: (0, 0))
    prow = pl.BlockSpec((tm, hdim), lambda i: (jnp.minimum(i, npt - 1), 0))
    srow = pl.BlockSpec((bs, hdim), lambda i: (0, 0))
    return pl.pallas_call(
        functools.partial(_rope_body, n_heads=n_heads, hd=hd, half=hd // ROT_FRACTION // 2,
                          n_prompt_tiles=npt, bs=bs),
        grid=(m_tot // tm,),
        in_specs=[col(0), col(1), col(2), tab, tab, one, one],
        out_specs=[pl.BlockSpec((tm, hdim), lambda i: (i, 0)), prow, prow, srow, srow],
        out_shape=[jax.ShapeDtypeStruct((m_tot, hdim), F32),
                   jax.ShapeDtypeStruct((mp, hdim), F32), jax.ShapeDtypeStruct((mp, hdim), F32),
                   jax.ShapeDtypeStruct((bs, hdim), F32), jax.ShapeDtypeStruct((bs, hdim), F32)],
        compiler_params=_params(("arbitrary",), 48),
        name="rope",
    )(qkv, qkv, qkv, cos_p, sin_p, cos_s, sin_s)


def _nt_dot(a, b, **kw):
    return lax.dot_general(a, b, (((1,), (1,)), ((), ())), preferred_element_type=F32, **kw)


def _attn_body(q_ref, k_ref, v_ref, o_ref, kmean_ref, kb_ref, vt_ref, *, nb, scale):
    blk = MOBA_BLOCK
    kmean_ref[...] = jnp.zeros(kmean_ref.shape, F32)
    for n in range(nb):
        rows = slice(n * blk, (n + 1) * blk)
        kmean_ref[n:n + 1, :] = jnp.sum(k_ref[rows, :], axis=0, keepdims=True) * (1.0 / blk)
        vt_ref[:, rows] = v_ref[rows, :].T.astype(BF16)
    kb_ref[...] = k_ref[...].astype(BF16)
    gate_all = _nt_dot(kmean_ref[...], q_ref[...], precision=lax.Precision.HIGHEST)
    bid = lax.broadcasted_iota(I32, (kmean_ref.shape[0], blk), 0)
    ki = lax.broadcasted_iota(I32, (blk, blk), 0)
    qi = lax.broadcasted_iota(I32, (blk, blk), 1)

    for j in range(nb):
        cols = slice(j * blk, (j + 1) * blk)
        gate = jnp.where(bid < j, gate_all[:, cols], NEG_INF)
        rank = jnp.zeros(gate.shape, I32)
        for m in range(j):
            gm = gate[m:m + 1, :]
            beats = (gm > gate) | ((gm == gate) & (m < bid))
            rank = rank + jnp.where(beats, 1, 0)
        sel = jnp.where((bid < j) & (rank < MOBA_TOPK), 1.0, 0.0)
        keys = (j + 1) * blk
        s = _nt_dot(kb_ref[0:keys, :], q_ref[cols, :].astype(BF16)) * scale
        slabs = [jnp.where(sel[n:n + 1, :] > 0.0, s[n * blk:(n + 1) * blk, :], NEG_INF) for n in range(j)]
        slabs.append(jnp.where(ki <= qi, s[j * blk:keys, :], NEG_INF))
        top = slabs[0]
        for sl in slabs[1:]:
            top = jnp.maximum(top, sl)
        top = jnp.max(top, axis=0, keepdims=True)
        p = [jnp.exp(sl - top) for sl in slabs]
        tot = p[0]
        for pn in p[1:]:
            tot = tot + pn
        denom = jnp.sum(tot, axis=0, keepdims=True)
        pv = jnp.dot(vt_ref[:, 0:keys], jnp.concatenate(p, axis=0).astype(BF16),
                     preferred_element_type=F32)
        o_ref[cols, :] = (pv / denom).T.astype(o_ref.dtype)


def _attn_prompt(q, k_p, v_p, *, bp, seq, n_heads, hd):
    nb = seq // MOBA_BLOCK
    nbp = -(-nb // SUBLANES) * SUBLANES
    blk = pl.BlockSpec((seq, hd), lambda b, h: (b, h))
    return pl.pallas_call(
        functools.partial(_attn_body, nb=nb, scale=hd ** -0.5),
        grid=(bp, n_heads),
        in_specs=[blk, blk, blk],
        out_specs=blk,
        out_shape=jax.ShapeDtypeStruct((bp * seq, n_heads * hd), BF16),
        scratch_shapes=[pltpu.VMEM((nbp, hd), F32), pltpu.VMEM((seq, hd), BF16), pltpu.VMEM((hd, seq), BF16)],
        compiler_params=_params(("arbitrary", "arbitrary"), 48),
        name="moba_prompt",
    )(q, k_p, v_p)


def _page_mean_body(pt_ref, *refs, pages_per_step, pages_per_block):
    del pt_ref
    pages, o_ref = refs[:pages_per_step], refs[pages_per_step]
    inv = 1.0 / MOBA_BLOCK
    for c in range(pages_per_step // pages_per_block):
        tot = jnp.sum(pages[c * pages_per_block][...], axis=0)
        for r in range(1, pages_per_block):
            tot = tot + jnp.sum(pages[c * pages_per_block + r][...], axis=0)
        o_ref[c] = tot * inv


def _page_means(cache_k, layer, page_table):
    _, _, page, n_heads, hd = cache_k.shape
    bs, n_pages = page_table.shape
    ppb = MOBA_BLOCK // page
    pps = _pick(n_pages, (8, 4, 2)) if ppb == 2 else ppb
    n_blocks = n_pages // ppb

    def page_spec(c):
        return pl.BlockSpec((None, None, page, n_heads, hd),
                            lambda b, s, pt: (layer, pt[b * n_pages + s * pps + c], 0, 0, 0))

    return pl.pallas_call(
        functools.partial(_page_mean_body, pages_per_step=pps, pages_per_block=ppb),
        grid_spec=pltpu.PrefetchScalarGridSpec(
            num_scalar_prefetch=1,
            grid=(bs, n_pages // pps),
            in_specs=[page_spec(c) for c in range(pps)],
            out_specs=pl.BlockSpec((None, pps // ppb, n_heads, hd), lambda b, s, pt: (b, s, 0, 0)),
        ),
        out_shape=jax.ShapeDtypeStruct((bs, n_blocks, n_heads, hd), F32),
        compiler_params=_params(("arbitrary", "arbitrary"), 40),
        name="page_means",
    )(page_table.reshape(-1), *([cache_k] * pps))


def _select_body(km_ref, q_ref, o_ref, *, n_heads):
    n_blocks = km_ref.shape[1]
    res = jnp.zeros(o_ref.shape, I32)
    row = lax.broadcasted_iota(I32, o_ref.shape, 0)
    lane = lax.broadcasted_iota(I32, o_ref.shape, 1)
    blk_id = lax.broadcasted_iota(I32, (n_blocks, 1), 0)
    for h in range(n_heads):
        gate = jnp.sum(km_ref[h] * q_ref[h:h + 1, :], axis=1, keepdims=True)
        for k in range(MOBA_TOPK):
            best = jnp.max(gate, axis=0, keepdims=True)
            idx = jnp.min(jnp.where(gate == best, blk_id, n_blocks), axis=0, keepdims=True)
            res = jnp.where((row == k) & (lane == h), idx, res)
            gate = jnp.where(blk_id == idx, -jnp.inf, gate)
    o_ref[...] = res


def _select_blocks(kmean_t, q_s):
    bs, n_heads, n_blocks, hd = kmean_t.shape
    return pl.pallas_call(
        functools.partial(_select_body, n_heads=n_heads),
        grid=(bs,),
        in_specs=[pl.BlockSpec((None, n_heads, n_blocks, hd), lambda b: (b, 0, 0, 0)),
                  pl.BlockSpec((None, n_heads, hd), lambda b: (b, 0, 0))],
        out_specs=pl.BlockSpec((None, SUBLANES, LANES), lambda b: (b, 0, 0)),
        out_shape=jax.ShapeDtypeStruct((bs, SUBLANES, LANES), I32),
        compiler_params=_params(("arbitrary",), 32),
        name="moba_select",
    )(kmean_t, q_s.reshape(bs, n_heads, hd))


def _decode_attn_body(ph_ref, q_ref, kn_ref, vn_ref, ck_hbm, cv_hbm, o_ref, kbuf, vbuf, sem,
                      *, layer, bs, n_heads, n_sel_pages, scale):
    t = pl.program_id(0)
    n_steps = pl.num_programs(0)

    def page_copies(step, slot):
        h, b = step // bs, step % bs
        out = []
        for r in range(n_sel_pages):
            pg = ph_ref[(b * n_heads + h) * n_sel_pages + r]
            out.append(pltpu.make_async_copy(ck_hbm.at[layer, pg, :, h, :], kbuf.at[slot, r], sem.at[slot]))
            out.append(pltpu.make_async_copy(cv_hbm.at[layer, pg, :, h, :], vbuf.at[slot, r], sem.at[slot]))
        return out

    @pl.when(t == 0)
    def _first():
        for c in page_copies(t, 0):
            c.start()

    @pl.when(t + 1 < n_steps)
    def _prefetch():
        for c in page_copies(t + 1, (t + 1) % 2):
            c.start()

    slot = t % 2
    for c in page_copies(t, slot):
        c.wait()

    row = pl.ds(t % bs, 1)
    q = q_ref[row, :]
    s_new = jnp.sum(q * kn_ref[row, :], axis=1, keepdims=True) * scale
    scores = [jnp.sum(kbuf[slot, r] * q, axis=1, keepdims=True) * scale for r in range(n_sel_pages)]
    top = s_new
    for s in scores:
        top = jnp.maximum(top, jnp.max(s, axis=0, keepdims=True))
    p_new = jnp.exp(s_new - top)
    denom = p_new
    acc = p_new * vn_ref[row, :]
    for r, s in enumerate(scores):
        p = jnp.exp(s - top)
        denom = denom + jnp.sum(p, axis=0, keepdims=True)
        acc = acc + jnp.sum(p * vbuf[slot, r], axis=0, keepdims=True)
    o_ref[row, :] = acc / denom


def _decode_attn(q_s, k_s, v_s, cache_k, cache_v, layer, phys_pages, *, n_heads, hd):
    bs = q_s.shape[0]
    page = cache_k.shape[2]
    nsp = phys_pages.shape[-1]
    head_rows = pl.BlockSpec((bs, hd), lambda t, ph: (0, t // bs))
    anywhere = pl.BlockSpec(memory_space=pl.ANY)
    return pl.pallas_call(
        functools.partial(_decode_attn_body, layer=layer, bs=bs, n_heads=n_heads, n_sel_pages=nsp,
                          scale=hd ** -0.5),
        grid_spec=pltpu.PrefetchScalarGridSpec(
            num_scalar_prefetch=1,
            grid=(n_heads * bs,),
            in_specs=[head_rows, head_rows, head_rows, anywhere, anywhere],
            out_specs=head_rows,
            scratch_shapes=[pltpu.VMEM((2, nsp, page, hd), F32), pltpu.VMEM((2, nsp, page, hd), F32),
                            pltpu.SemaphoreType.DMA((2,))],
        ),
        out_shape=jax.ShapeDtypeStruct((bs, n_heads * hd), F32),
        compiler_params=_params(("arbitrary",), 32),
        name="moba_decode",
    )(phys_pages.reshape(-1), q_s, k_s, v_s, cache_k, cache_v)


def _gelu_tanh(x):
    return 0.5 * x * (1.0 + jnp.tanh(np.sqrt(2.0 / np.pi).astype(np.float32) * (x + 0.044715 * (x * x * x))))


def _softplus(x):
    return jnp.maximum(x, 0.0) + jnp.log1p(jnp.exp(-jnp.abs(x)))


def _lru_gates(uc, wa_ref, wx_ref, ba, bx, lam, n_heads, rb):
    ucb = uc.astype(BF16)
    a_parts, b_parts = [], []
    sp = _softplus(-lam)
    for h in range(n_heads):
        cols = slice(h * rb, (h + 1) * rb)
        r = jax.nn.sigmoid(jnp.dot(ucb[:, cols], wa_ref[h].astype(BF16), preferred_element_type=F32) + ba[:, cols])
        g = jax.nn.sigmoid(jnp.dot(ucb[:, cols], wx_ref[h].astype(BF16), preferred_element_type=F32) + bx[:, cols])
        log_a = -LRU_C * r * sp[:, cols]
        a_parts.append(jnp.exp(log_a))
        b_parts.append(jnp.sqrt(1.0 - jnp.exp(2.0 * log_a)) * (g * uc[:, cols]))
    return jnp.concatenate(a_parts, axis=1), jnp.concatenate(b_parts, axis=1)


def _lru_body(g_ref, u_ref, wc_ref, bc_ref, wa_ref, wx_ref, ba_ref, bx_ref, lam_ref,
              c0_ref, c1_ref, c2_ref, h0_ref,
              y_ref, newc_ref, newh_ref, us_ref, hs_ref,
              carry_ref, hcarry_ref, a_buf, b_buf, *, tiles_per_seq, n_prompt_tiles, bs, n_heads, rb):
    i = pl.program_id(0)
    tm = y_ref.shape[0]
    w0, w1, w2, w3 = wc_ref[0:1, :], wc_ref[1:2, :], wc_ref[2:3, :], wc_ref[3:4, :]
    bias = bc_ref[...]
    gates = functools.partial(_lru_gates, wa_ref=wa_ref, wx_ref=wx_ref, ba=ba_ref[...], bx=bx_ref[...],
                              lam=lam_ref[...], n_heads=n_heads, rb=rb)

    @pl.when(i < n_prompt_tiles)
    def _prompt():
        u = u_ref[...]
        fresh = i % tiles_per_seq == 0
        c0, c1, c2 = (jnp.where(fresh, 0.0, carry_ref[r:r + 1, :]) for r in range(3))
        p1 = _shift_rows(u, 1, (c2,))
        p2 = _shift_rows(u, 2, (c1, c2))
        p3 = _shift_rows(u, 3, (c0, c1, c2))
        uc = w0 * p3 + w1 * p2 + w2 * p1 + w3 * u + bias
        a, b = gates(uc)
        a_buf[...] = a
        b_buf[...] = b
        sub = lax.broadcasted_iota(I32, (SUBLANES, a.shape[1]), 0)

        def group(gi, h):
            rows = pl.ds(pl.multiple_of(gi * SUBLANES, SUBLANES), SUBLANES)
            ag, bg = a_buf[rows, :], b_buf[rows, :]
            for d in (1, 2, 4):
                bg = bg + ag * jnp.where(sub >= d, pltpu.roll(bg, d, 0), 0.0)
                ag = ag * jnp.where(sub >= d, pltpu.roll(ag, d, 0), 1.0)
            hg = ag * h + bg
            b_buf[rows, :] = hg
            return hg[SUBLANES - 1:SUBLANES, :]

        h_last = lax.fori_loop(0, tm // SUBLANES, group, jnp.where(fresh, 0.0, hcarry_ref[0:1, :]))
        hcarry_ref[0:1, :] = h_last
        y_ref[...] = (b_buf[...] * _gelu_tanh(g_ref[...])).astype(y_ref.dtype)
        carry_ref[0:3, :] = u[tm - 3:tm, :]
        newc_ref[...] = u[tm - 3:tm, :]
        newh_ref[...] = h_last

    @pl.when(i == n_prompt_tiles)
    def _sample():
        u = u_ref[0:bs, :]
        uc = w0 * c0_ref[...] + w1 * c1_ref[...] + w2 * c2_ref[...] + w3 * u + bias
        a, b = gates(uc)
        h = a * h0_ref[...] + b
        y_ref[...] = jnp.zeros(y_ref.shape, y_ref.dtype)
        y_ref[0:bs, :] = (h * _gelu_tanh(g_ref[0:bs, :])).astype(y_ref.dtype)
        us_ref[...] = u
        hs_ref[...] = h


def _rglru(proj, w_conv, b_conv, w_gate_a, b_gate_a, w_gate_x, b_gate_x, lam, layer, conv_state, h0,
           *, bp, seq, bs):
    m_tot = proj.shape[0]
    dr = proj.shape[1] // 2
    n_heads, rb = w_gate_a.shape[1], w_gate_a.shape[2]
    tm = TOKEN_TILE
    tps = seq // tm
    npt = bp * tps
    width = w_conv.shape[1]
    col = lambda c: pl.BlockSpec((tm, dr), lambda i: (i, c))
    vec = pl.BlockSpec((None, 1, dr), lambda i: (layer, 0, 0))
    gw = pl.BlockSpec((None, n_heads, rb, rb), lambda i: (layer, 0, 0, 0))
    st = pl.BlockSpec((bs, dr), lambda i: (0, 0))
    pb = lambda rows: pl.BlockSpec((None, rows, dr), lambda i: (jnp.minimum(i // tps, bp - 1), 0, 0))
    return pl.pallas_call(
        functools.partial(_lru_body, tiles_per_seq=tps, n_prompt_tiles=npt, bs=bs, n_heads=n_heads, rb=rb),
        grid=(m_tot // tm,),
        in_specs=[col(0), col(1), pl.BlockSpec((None, width, dr), lambda i: (layer, 0, 0)), vec,
                  gw, gw, vec, vec, vec, st, st, st, st],
        out_specs=[pl.BlockSpec((tm, dr), lambda i: (i, 0)), pb(width - 1), pb(1), st, st],
        out_shape=[jax.ShapeDtypeStruct((m_tot, dr), BF16),
                   jax.ShapeDtypeStruct((bp, width - 1, dr), F32),
                   jax.ShapeDtypeStruct((bp, 1, dr), F32),
                   jax.ShapeDtypeStruct((bs, dr), F32),
                   jax.ShapeDtypeStruct((bs, dr), F32)],
        scratch_shapes=[pltpu.VMEM((SUBLANES, dr), F32), pltpu.VMEM((SUBLANES, dr), F32),
                        pltpu.VMEM((tm, dr), F32), pltpu.VMEM((tm, dr), F32)],
        compiler_params=_params(("arbitrary",), 48),
        name="rglru",
    )(proj, proj, w_conv, b_conv.reshape(-1, 1, dr), w_gate_a, w_gate_x,
      b_gate_a.reshape(-1, 1, dr), b_gate_x.reshape(-1, 1, dr), lam.reshape(-1, 1, dr),
      conv_state[:, 0], conv_state[:, 1], conv_state[:, 2], h0)


def _ffn_body(e_ref, start_ref, nt_ref, nused_ref, xs_hbm, wg_ref, wu_ref, wd_ref, o_hbm,
              xbuf, acc, wgb, wub, wdb, sem_in, sem_out, *, tm, nf):
    del e_ref
    s, f = pl.program_id(0), pl.program_id(1)
    nt = nt_ref[s]
    start = start_ref[s]

    def copy_in(r):
        return pltpu.make_async_copy(xs_hbm.at[pl.ds(pl.multiple_of(start + r * tm, tm), tm), :],
                                     xbuf.at[pl.ds(pl.multiple_of(r * tm, tm), tm), :], sem_in)

    def copy_out(r):
        return pltpu.make_async_copy(acc.at[pl.ds(pl.multiple_of(r * tm, tm), tm), :],
                                     o_hbm.at[pl.ds(pl.multiple_of(start + r * tm, tm), tm), :], sem_out)

    def for_tiles(fn):
        def body(r, c):
            fn(r)
            return c
        lax.fori_loop(0, nt, body, 0)

    @pl.when(nt > 0)
    def _run():
        @pl.when(f == 0)
        def _load():
            for_tiles(lambda r: copy_in(r).start())

            def clear(r):
                acc[pl.ds(pl.multiple_of(r * tm, tm), tm), :] = jnp.zeros((tm, acc.shape[1]), F32)

            for_tiles(clear)
            for_tiles(lambda r: copy_in(r).wait())

        wgb[...] = wg_ref[...].astype(BF16)
        wub[...] = wu_ref[...].astype(BF16)
        wdb[...] = wd_ref[...].astype(BF16)

        def rows_step(first_tile, n_tiles):
            rows = pl.ds(pl.multiple_of(first_tile * tm, tm), n_tiles * tm)
            x = xbuf[rows, :]
            g = jnp.dot(x, wgb[...], preferred_element_type=F32)
            u = jnp.dot(x, wub[...], preferred_element_type=F32)
            h = (g * jax.nn.sigmoid(g) * u).astype(BF16)
            acc[rows, :] += jnp.dot(h, wdb[...], preferred_element_type=F32)

            @pl.when(f == nf - 1)
            def _store():
                for r in range(n_tiles):
                    copy_out(first_tile + r).start()

        def pair(i, c):
            rows_step(2 * i, 2)
            return c

        lax.fori_loop(0, nt // 2, pair, 0)

        @pl.when(nt % 2 == 1)
        def _last_single():
            rows_step(nt - 1, 1)

        @pl.when(f == nf - 1)
        def _drain_stores():
            for_tiles(lambda r: copy_out(r).wait())

    @pl.when((s == pl.num_programs(0) - 1) & (f == nf - 1))
    def _zero_unused():
        acc[0:tm, :] = jnp.zeros((tm, acc.shape[1]), F32)

        def zero_tile(t, c):
            cp = pltpu.make_async_copy(acc.at[0:tm, :], o_hbm.at[pl.ds(pl.multiple_of(t * tm, tm), tm), :], sem_out)
            cp.start()
            cp.wait()
            return c

        lax.fori_loop(nused_ref[0], o_hbm.shape[0] // tm, zero_tile, 0)


def _ffn(xs, w_gate, w_up, w_down, layer, sched_e, sched_start, sched_nt, n_used_tiles):
    s_rows, d = xs.shape
    dff = w_gate.shape[3]
    tm, tf, rt = TOKEN_TILE, FFN_COL_TILE, FFN_SUPER_TILES
    nf = dff // tf
    n_super = sched_e.shape[0]

    def fcol(s, f, nt_ref):
        return jnp.where(nt_ref[s] > 0, f, nf - 1)

    up_spec = pl.BlockSpec((None, None, d, tf), lambda s, f, e, st, nt, nu: (layer, e[s], 0, fcol(s, f, nt)))
    down_spec = pl.BlockSpec((None, None, tf, d), lambda s, f, e, st, nt, nu: (layer, e[s], fcol(s, f, nt), 0))
    return pl.pallas_call(
        functools.partial(_ffn_body, tm=tm, nf=nf),
        grid_spec=pltpu.PrefetchScalarGridSpec(
            num_scalar_prefetch=4,
            grid=(n_super, nf),
            in_specs=[pl.BlockSpec(memory_space=pl.ANY), up_spec, up_spec, down_spec],
            out_specs=pl.BlockSpec(memory_space=pl.ANY),
            scratch_shapes=[pltpu.VMEM((rt * tm, d), BF16), pltpu.VMEM((rt * tm, d), F32),
                            pltpu.VMEM((d, tf), BF16), pltpu.VMEM((d, tf), BF16), pltpu.VMEM((tf, d), BF16),
                            pltpu.SemaphoreType.DMA(()), pltpu.SemaphoreType.DMA(())],
        ),
        out_shape=jax.ShapeDtypeStruct((s_rows, d), F32),
        compiler_params=_params(("arbitrary", "arbitrary"), 56),
        name="swiglu_ffn",
    )(sched_e, sched_start, sched_nt, n_used_tiles, xs, w_gate, w_up, w_down)


def _dense_schedule(n_tiles):
    rt = FFN_SUPER_TILES
    n_super = -(-n_tiles // rt)
    start = np.arange(n_super, dtype=np.int32) * rt
    nt = np.minimum(rt, n_tiles - start).astype(np.int32)
    return (jnp.zeros((n_super,), I32), jnp.asarray(start * TOKEN_TILE, I32), jnp.asarray(nt, I32),
            jnp.full((1,), n_tiles, I32))


def _router_body(x_ref, w_ref, o_ref, *, n_exp):
    logits = jnp.dot(x_ref[...], w_ref[...], precision=lax.Precision.HIGHEST, preferred_element_type=F32)
    lane = lax.broadcasted_iota(I32, logits.shape, 1)
    lg = jnp.where(lane < n_exp, logits, -jnp.inf)
    m1 = jnp.max(lg, axis=1, keepdims=True)
    i1 = jnp.min(jnp.where(lg == m1, lane, LANES), axis=1, keepdims=True)
    lg = jnp.where(lane == i1, -jnp.inf, lg)
    m2 = jnp.max(lg, axis=1, keepdims=True)
    i2 = jnp.min(jnp.where(lg == m2, lane, LANES), axis=1, keepdims=True)
    e = jnp.exp(m2 - m1)
    g1 = 1.0 / (1.0 + e)
    g2 = e / (1.0 + e)
    o_ref[...] = jnp.where(lane == 0, i1.astype(F32),
                           jnp.where(lane == 1, i2.astype(F32),
                                     jnp.where(lane == 2, g1, jnp.where(lane == 3, g2, 0.0))))


def _router(x, w_router, layer):
    m, d = x.shape
    n_exp = w_router.shape[2]
    w_pad = jnp.pad(w_router, ((0, 0), (0, 0), (0, LANES - n_exp)))
    tm = TOKEN_TILE
    return pl.pallas_call(
        functools.partial(_router_body, n_exp=n_exp),
        grid=(m // tm,),
        in_specs=[pl.BlockSpec((tm, d), lambda i: (i, 0)),
                  pl.BlockSpec((None, d, LANES), lambda i: (layer, 0, 0))],
        out_specs=pl.BlockSpec((tm, LANES), lambda i: (i, 0)),
        out_shape=jax.ShapeDtypeStruct((m, LANES), F32),
        compiler_params=_params(("arbitrary",), 32),
        name="moe_router",
    )(x, w_pad)


def _moe_plan(route, n_valid, n_exp):
    m_tot = route.shape[0]
    tm, rt = TOKEN_TILE, FFN_SUPER_TILES
    n_tiles_max = (MOE_TOP_K * n_valid + n_exp * (tm - 1)) // tm
    s_rows = n_tiles_max * tm
    n_super = n_tiles_max // rt + n_exp
    e1, e2 = route[:, 0].astype(I32), route[:, 1].astype(I32)
    tok = jnp.arange(m_tot, dtype=I32)
    valid = tok < n_valid
    ids = jnp.arange(n_exp, dtype=I32)[None, :]
    oh1 = (e1[:, None] == ids) & valid[:, None]
    oh2 = (e2[:, None] == ids) & valid[:, None]
    oh = oh1.astype(I32) + oh2.astype(I32)
    csum = jnp.cumsum(oh, axis=0)
    before = csum - oh
    counts = csum[-1]
    tiles_e = (counts + tm - 1) // tm
    tile_end = jnp.cumsum(tiles_e)
    tile_off = tile_end - tiles_e
    row_off = tile_off * tm
    pos1 = jnp.sum(jnp.where(oh1, before + row_off[None, :], 0), axis=1)
    pos2 = jnp.sum(jnp.where(oh2, before + row_off[None, :], 0), axis=1)
    drop = jnp.where(valid, 0, s_rows)
    slot_token = jnp.zeros((s_rows,), I32).at[jnp.concatenate([pos1 + drop, pos2 + drop])].set(
        jnp.concatenate([tok, tok]), mode="drop")
    n_used_tiles = tile_end[-1:]
    super_e = -(-tiles_e // rt)
    super_end = jnp.cumsum(super_e)
    super_off = super_end - super_e
    n_used_super = super_end[-1]
    sidx = jnp.arange(n_super, dtype=I32)
    live = sidx < n_used_super
    sclamp = jnp.minimum(sidx, n_used_super - 1)
    exp_of = jnp.sum((sclamp[:, None] >= super_end[None, :]).astype(I32), axis=1)
    k = sclamp - super_off[exp_of]
    sched_start = (tile_off[exp_of] + k * rt) * tm
    sched_nt = jnp.where(live, jnp.clip(tiles_e[exp_of] - k * rt, 0, rt), 0)
    return pos1, pos2, slot_token, n_used_tiles, (exp_of, sched_start, sched_nt), s_rows


def _gather_body(tok_ref, nused_ref, x_hbm, o_ref, buf, sem, *, tm):
    t = pl.program_id(0)
    n_used = nused_ref[0]

    def issue_tile(tile, slot):
        def issue(g, c):
            for r in range(DMA_ISSUE_UNROLL):
                row = g * DMA_ISSUE_UNROLL + r
                pltpu.make_async_copy(x_hbm.at[pl.ds(tok_ref[tile * tm + row], 1), :],
                                      buf.at[slot, pl.ds(row, 1), :], sem.at[slot]).start()
            return c

        lax.fori_loop(0, tm // DMA_ISSUE_UNROLL, issue, 0)

    @pl.when((t == 0) & (n_used > 0))
    def _first():
        issue_tile(0, 0)

    @pl.when(t + 1 < n_used)
    def _prefetch():
        issue_tile(t + 1, (t + 1) % 2)

    @pl.when(t < n_used)
    def _used():
        slot = t % 2
        pltpu.make_async_copy(x_hbm.at[pl.ds(0, tm), :], buf.at[slot], sem.at[slot]).wait()
        o_ref[...] = buf[slot].astype(o_ref.dtype)

    @pl.when(t >= n_used)
    def _unused():
        o_ref[...] = jnp.zeros(o_ref.shape, o_ref.dtype)


def _gather_rows(x, slot_token, n_used_tiles):
    s_rows = slot_token.shape[0]
    d = x.shape[1]
    tm = TOKEN_TILE
    return pl.pallas_call(
        functools.partial(_gather_body, tm=tm),
        grid_spec=pltpu.PrefetchScalarGridSpec(
            num_scalar_prefetch=2,
            grid=(s_rows // tm,),
            in_specs=[pl.BlockSpec(memory_space=pl.ANY)],
            out_specs=pl.BlockSpec((tm, d), lambda t, tok, nu: (t, 0)),
            scratch_shapes=[pltpu.VMEM((2, tm, d), F32), pltpu.SemaphoreType.DMA((2,))],
        ),
        out_shape=jax.ShapeDtypeStruct((s_rows, d), BF16),
        compiler_params=_params(("arbitrary",), 32),
        name="moe_gather",
    )(slot_token, n_used_tiles, x)


def _combine_body(p1_ref, p2_ref, x_ref, route_ref, g_ref, b_ref, y_hbm, o_ref, ob_ref, buf, sem, *, tm, alpha):
    t = pl.program_id(0)

    def issue_tile(tile, slot):
        def issue(g, c):
            for r in range(DMA_ISSUE_UNROLL):
                row = g * DMA_ISSUE_UNROLL + r
                for which, p_ref in enumerate((p1_ref, p2_ref)):
                    pltpu.make_async_copy(y_hbm.at[pl.ds(p_ref[tile * tm + row], 1), :],
                                          buf.at[slot, which, pl.ds(row, 1), :], sem.at[slot]).start()
            return c

        lax.fori_loop(0, tm // DMA_ISSUE_UNROLL, issue, 0)

    @pl.when(t == 0)
    def _first():
        issue_tile(0, 0)

    @pl.when(t + 1 < pl.num_programs(0))
    def _prefetch():
        issue_tile(t + 1, (t + 1) % 2)

    slot = t % 2
    for which in range(2):
        pltpu.make_async_copy(y_hbm.at[pl.ds(0, tm), :], buf.at[slot, which], sem.at[slot]).wait()
    route = route_ref[...]
    y = route[:, 2:3] * buf[slot, 0] + route[:, 3:4] * buf[slot, 1]
    out = _layer_norm(alpha * x_ref[...] + y, g_ref[...], b_ref[...])
    o_ref[...] = out
    ob_ref[...] = out.astype(BF16)


def _combine_ln(x, route, y_slots, pos1, pos2, g_stack, b_stack, layer, alpha):
    m, d = x.shape
    tm = TOKEN_TILE
    row = pl.BlockSpec((tm, d), lambda t, p1, p2: (t, 0))
    par = pl.BlockSpec((None, 1, d), lambda t, p1, p2: (layer, 0, 0))
    return pl.pallas_call(
        functools.partial(_combine_body, tm=tm, alpha=alpha),
        grid_spec=pltpu.PrefetchScalarGridSpec(
            num_scalar_prefetch=2,
            grid=(m // tm,),
            in_specs=[row, pl.BlockSpec((tm, LANES), lambda t, p1, p2: (t, 0)), par, par,
                      pl.BlockSpec(memory_space=pl.ANY)],
            out_specs=[row, row],
            scratch_shapes=[pltpu.VMEM((2, 2, tm, d), F32), pltpu.SemaphoreType.DMA((2,))],
        ),
        out_shape=[jax.ShapeDtypeStruct((m, d), F32), jax.ShapeDtypeStruct((m, d), BF16)],
        compiler_params=_params(("arbitrary",), 40),
        name="moe_combine_ln",
    )(pos1, pos2, x, route, g_stack.reshape(-1, 1, d), b_stack.reshape(-1, 1, d), y_slots)


def kernel(x_prompt, x_sample, cache_k, cache_v, page_table, state_conv, state_lru_conv, state_lru_h, ln1_g, ln1_b, ln2_g, ln2_b, sc_w_in, sc_w_conv, sc_w_out, attn_w_qkv, attn_w_o, lru_w_in, lru_w_conv, lru_b_conv, lru_w_gate_a, lru_b_gate_a, lru_w_gate_x, lru_b_gate_x, lru_lambda, lru_w_out, ffn_w_gate, ffn_w_up, ffn_w_down, moe_w_router, moe_w_gate, moe_w_up, moe_w_down):
    bp, seq, d = x_prompt.shape
    bs, dec_seq, _ = x_sample.shape
    depth = ln1_g.shape[0]
    n_heads, hd = cache_k.shape[3], cache_k.shape[4]
    page = cache_k.shape[2]
    past_len = page_table.shape[1] * page
    n_exp = moe_w_router.shape[2]
    assert dec_seq == 1 and bs <= TOKEN_TILE
    assert seq % MOBA_BLOCK == 0 and seq % TOKEN_TILE == 0 and past_len % MOBA_BLOCK == 0
    assert MOBA_BLOCK % page == 0 and past_len // MOBA_BLOCK >= MOBA_TOPK
    alpha = float((2.0 * depth) ** 0.25)
    mp = bp * seq
    n_valid = mp + bs
    m_tot = -(-n_valid // TOKEN_TILE) * TOKEN_TILE
    ppb = MOBA_BLOCK // page

    x = jnp.concatenate([x_prompt.reshape(mp, d), x_sample.reshape(bs, d),
                         jnp.zeros((m_tot - n_valid, d), x_prompt.dtype)], axis=0)
    xb = x.astype(BF16)
    dense_sched = _dense_schedule(m_tot // TOKEN_TILE)

    conv_p, conv_s, kp_l, vp_l, ks_l, vs_l, lc_p, lc_s, lh_p, lh_s = ([] for _ in range(10))
    for i in range(depth):
        kind, m = i % 3, i // 3
        if kind == 0:
            proj = _matmul(xb, sc_w_in, m)
            y, new_p, u_s = _short_conv(proj, sc_w_conv, m, state_conv[m, :, 0], state_conv[m, :, 1],
                                        bp=bp, seq=seq, bs=bs)
            conv_p.append(new_p)
            conv_s.append(jnp.stack([state_conv[m, :, 1], u_s], axis=1))
            mixed, w_out = y, sc_w_out
        elif kind == 1:
            qkv = _matmul(xb, attn_w_qkv, m)
            q, k_p, v_p, k_s, v_s = _rope(qkv, n_heads=n_heads, hd=hd, bp=bp, seq=seq, bs=bs, past_len=past_len)
            kp_l.append(k_p.reshape(bp, seq, n_heads, hd))
            vp_l.append(v_p.reshape(bp, seq, n_heads, hd))
            ks_l.append(k_s.reshape(bs, 1, n_heads, hd))
            vs_l.append(v_s.reshape(bs, 1, n_heads, hd))
            o_p = _attn_prompt(q, k_p, v_p, bp=bp, seq=seq, n_heads=n_heads, hd=hd)
            q_s = q[mp:mp + bs]
            kmean = _page_means(cache_k, m, page_table)
            sel = _select_blocks(kmean.transpose(0, 2, 1, 3), q_s)
            sel = sel[:, :MOBA_TOPK, :n_heads].transpose(0, 2, 1)
            logical = sel[..., None] * ppb + jnp.arange(ppb, dtype=I32)
            phys = jnp.take_along_axis(page_table, logical.reshape(bs, -1), axis=1)
            o_s = _decode_attn(q_s, k_s, v_s, cache_k, cache_v, m, phys.reshape(bs, n_heads, -1),
                               n_heads=n_heads, hd=hd)
            o = jnp.concatenate([o_p, o_s.astype(BF16), jnp.zeros((m_tot - n_valid, n_heads * hd), BF16)], axis=0)
            mixed, w_out = o, attn_w_o
        else:
            proj = _matmul(xb, lru_w_in, m)
            y, new_c, new_h, u_s, h_s = _rglru(proj, lru_w_conv, lru_b_conv, lru_w_gate_a, lru_b_gate_a,
                                               lru_w_gate_x, lru_b_gate_x, lru_lambda, m,
                                               state_lru_conv[m], state_lru_h[m], bp=bp, seq=seq, bs=bs)
            lc_p.append(new_c)
            lc_s.append(jnp.concatenate([state_lru_conv[m, :, 1:], u_s[:, None, :]], axis=1))
            lh_p.append(new_h.reshape(bp, -1))
            lh_s.append(h_s)
            mixed, w_out = y, lru_w_out
        x, xb = _matmul_add_ln(mixed, w_out, m, x, ln1_g, ln1_b, i, alpha)
        j = i // 2
        if i % 2 == 0:
            f = _ffn(xb, ffn_w_gate[:, None], ffn_w_up[:, None], ffn_w_down[:, None], j, *dense_sched)
            x, xb = _add_ln(x, f, ln2_g, ln2_b, i, alpha)
        else:
            route = _router(x, moe_w_router, j)
            pos1, pos2, slot_token, n_used, sched, _ = _moe_plan(route, n_valid, n_exp)
            xs = _gather_rows(x, slot_token, n_used)
            ys = _ffn(xs, moe_w_gate, moe_w_up, moe_w_down, j, *sched, n_used)
            x, xb = _combine_ln(x, route, ys, pos1, pos2, ln2_g, ln2_b, i, alpha)
    return (x[:mp].reshape(bp, seq, d), x[mp:mp + bs].reshape(bs, 1, d),
            jnp.stack(conv_p), jnp.stack(conv_s),
            jnp.stack(kp_l), jnp.stack(vp_l), jnp.stack(ks_l), jnp.stack(vs_l),
            jnp.stack(lc_p), jnp.stack(lc_s), jnp.stack(lh_p), jnp.stack(lh_s))
```

```python
import functools

import numpy as np
import jax
import jax.numpy as jnp
from jax import lax
from jax.experimental import pallas as pl
from jax.experimental.pallas import tpu as pltpu

F32, BF16, I32 = jnp.float32, jnp.bfloat16, jnp.int32

MOBA_BLOCK = 256
MOBA_TOPK = 3
ROPE_THETA = 500000.0
ROT_FRACTION = 4
LRU_C = 8.0
MOE_TOP_K = 2
LN_EPS = 1e-5
NEG_INF = -1e30
LOG2_E = 1.4426950408889634

LANES = 128
SUBLANES = 8
V7X_VMEM_BYTES = 64 * 1024 * 1024

TOKEN_TILE = 256
FFN_COL_TILE = 256
FFN_SUPER_TILES = 10
DMA_ISSUE_UNROLL = 8


def _params(sem, vmem_mib, **kw):
    return pltpu.CompilerParams(dimension_semantics=sem, vmem_limit_bytes=vmem_mib * 1024 * 1024, **kw)


def _pick(total, candidates):
    for c in candidates:
        if total % c == 0:
            return c
    raise ValueError(f"no tile in {candidates} divides {total}")


def _mm_body(x_ref, w_ref, o_ref, wb_ref):
    @pl.when(pl.program_id(1) == 0)
    def _cast():
        wb_ref[...] = w_ref[...].astype(BF16)

    o_ref[...] = jnp.dot(x_ref[...], wb_ref[...], preferred_element_type=F32).astype(o_ref.dtype)


def _matmul(xb, w_stack, layer, out_dtype=F32):
    m, k = xb.shape
    n = w_stack.shape[2]
    tm = _pick(m, (768, 512, 256))
    tn = _pick(n, (1024, 512, 256, 128))
    return pl.pallas_call(
        _mm_body,
        grid=(n // tn, m // tm),
        in_specs=[pl.BlockSpec((tm, k), lambda j, i: (i, 0)),
                  pl.BlockSpec((None, k, tn), lambda j, i: (layer, 0, j))],
        out_specs=pl.BlockSpec((tm, tn), lambda j, i: (i, j)),
        out_shape=jax.ShapeDtypeStruct((m, n), out_dtype),
        scratch_shapes=[pltpu.VMEM((k, tn), BF16)],
        compiler_params=_params(("arbitrary", "arbitrary"), 48),
        name="matmul",
    )(xb, w_stack)


def _layer_norm(z, g, b):
    mu = jnp.mean(z, axis=-1, keepdims=True)
    zc = z - mu
    var = jnp.mean(zc * zc, axis=-1, keepdims=True)
    return zc * lax.rsqrt(var + LN_EPS) * g + b


def _mm_ln_body(y_ref, w_ref, x_ref, g_ref, b_ref, o_ref, ob_ref, wb_ref, *, alpha):
    @pl.when(pl.program_id(0) == 0)
    def _cast():
        wb_ref[...] = w_ref[...].astype(BF16)

    mix = jnp.dot(y_ref[...], wb_ref[...], preferred_element_type=F32)
    out = _layer_norm(alpha * x_ref[...] + mix, g_ref[...], b_ref[...])
    o_ref[...] = out
    ob_ref[...] = out.astype(BF16)


def _matmul_add_ln(yb, w_stack, w_layer, x, g_stack, b_stack, layer, alpha):
    m, k = yb.shape
    d = x.shape[1]
    assert w_stack.shape[2] == d
    tm = TOKEN_TILE
    row = lambda width: pl.BlockSpec((tm, width), lambda i: (i, 0))
    par = pl.BlockSpec((None, 1, d), lambda i: (layer, 0, 0))
    return pl.pallas_call(
        functools.partial(_mm_ln_body, alpha=alpha),
        grid=(m // tm,),
        in_specs=[row(k),
                  pl.BlockSpec((None, k, d), lambda i: (w_layer, 0, 0), pipeline_mode=pl.Buffered(1)),
                  row(d), par, par],
        out_specs=[row(d), row(d)],
        out_shape=[jax.ShapeDtypeStruct((m, d), F32), jax.ShapeDtypeStruct((m, d), BF16)],
        scratch_shapes=[pltpu.VMEM((k, d), BF16)],
        compiler_params=_params(("arbitrary",), 48),
        name="matmul_add_ln",
    )(yb, w_stack, x, g_stack.reshape(-1, 1, d), b_stack.reshape(-1, 1, d))


def _add_ln_body(x_ref, y_ref, g_ref, b_ref, o_ref, ob_ref, *, alpha):
    out = _layer_norm(alpha * x_ref[...] + y_ref[...], g_ref[...], b_ref[...])
    o_ref[...] = out
    ob_ref[...] = out.astype(BF16)


def _add_ln(x, y, g_stack, b_stack, layer, alpha):
    m, d = x.shape
    tm = TOKEN_TILE
    row = pl.BlockSpec((tm, d), lambda i: (i, 0))
    par = pl.BlockSpec((None, 1, d), lambda i: (layer, 0, 0))
    return pl.pallas_call(
        functools.partial(_add_ln_body, alpha=alpha),
        grid=(m // tm,),
        in_specs=[row, row, par, par],
        out_specs=[row, row],
        out_shape=[jax.ShapeDtypeStruct((m, d), F32), jax.ShapeDtypeStruct((m, d), BF16)],
        compiler_params=_params(("arbitrary",), 32),
        name="add_ln",
    )(x, y, g_stack.reshape(-1, 1, d), b_stack.reshape(-1, 1, d))


def _shift_rows(u, k, carry_rows):
    out = pltpu.roll(u, k, 0)
    row = lax.broadcasted_iota(I32, u.shape, 0)
    for r, c in enumerate(carry_rows):
        out = jnp.where(row == r, c, out)
    return out


def _sconv_body(b_ref, c_ref, h_ref, w_ref, s0_ref, s1_ref, y_ref, newp_ref, news_ref, carry_ref,
                *, tiles_per_seq, n_prompt_tiles, bs):
    i = pl.program_id(0)
    w0, w1, w2 = w_ref[0:1, :], w_ref[1:2, :], w_ref[2:3, :]
    tm = y_ref.shape[0]

    @pl.when(i < n_prompt_tiles)
    def _prompt():
        u = c_ref[...] * h_ref[...]
        fresh = i % tiles_per_seq == 0
        c0 = jnp.where(fresh, 0.0, carry_ref[0:1, :])
        c1 = jnp.where(fresh, 0.0, carry_ref[1:2, :])
        p1 = _shift_rows(u, 1, (c1,))
        p2 = _shift_rows(u, 2, (c0, c1))
        y = w0 * p2 + w1 * p1 + w2 * u
        y_ref[...] = (b_ref[...] * y).astype(y_ref.dtype)
        carry_ref[0:2, :] = u[tm - 2:tm, :]
        newp_ref[...] = u[tm - 2:tm, :]

    @pl.when(i == n_prompt_tiles)
    def _sample():
        u = c_ref[0:bs, :] * h_ref[0:bs, :]
        y = w0 * s0_ref[...] + w1 * s1_ref[...] + w2 * u
        y_ref[...] = jnp.zeros(y_ref.shape, y_ref.dtype)
        y_ref[0:bs, :] = (b_ref[0:bs, :] * y).astype(y_ref.dtype)
        news_ref[...] = u


def _short_conv(proj, w_conv, layer, s0, s1, *, bp, seq, bs):
    m_tot = proj.shape[0]
    d = proj.shape[1] // 3
    tm = TOKEN_TILE
    tps = seq // tm
    npt = bp * tps
    col = lambda c: pl.BlockSpec((tm, d), lambda i: (i, c))
    full = lambda shape: pl.BlockSpec(shape, lambda i: (0,) * len(shape))
    return pl.pallas_call(
        functools.partial(_sconv_body, tiles_per_seq=tps, n_prompt_tiles=npt, bs=bs),
        grid=(m_tot // tm,),
        in_specs=[col(0), col(1), col(2),
                  pl.BlockSpec((None, w_conv.shape[1], d), lambda i: (layer, 0, 0)),
                  full((bs, d)), full((bs, d))],
        out_specs=[pl.BlockSpec((tm, d), lambda i: (i, 0)),
                   pl.BlockSpec((None, 2, d), lambda i: (jnp.minimum(i // tps, bp - 1), 0, 0)),
                   full((bs, d))],
        out_shape=[jax.ShapeDtypeStruct((m_tot, d), BF16),
                   jax.ShapeDtypeStruct((bp, 2, d), F32),
                   jax.ShapeDtypeStruct((bs, d), F32)],
        scratch_shapes=[pltpu.VMEM((SUBLANES, d), F32)],
        compiler_params=_params(("arbitrary",), 40),
        name="short_conv",
    )(proj, proj, proj, w_conv, s0, s1)


def _rope_head(x, cos_f, sin_f, half):
    lane = lax.broadcasted_iota(I32, x.shape, 1)
    partner = jnp.where(lane < half, pltpu.roll(x, LANES - half, 1), pltpu.roll(x, half, 1))
    return x * cos_f + partner * sin_f


def _rope_body(q_ref, k_ref, v_ref, cos_ref, sin_ref, cos_s_ref, sin_s_ref,
               qo_ref, kp_ref, vp_ref, ks_ref, vs_ref, *, n_heads, hd, half, n_prompt_tiles, bs):
    i = pl.program_id(0)

    def rotate(x_ref, rows, cos_f, sin_f):
        return [
            _rope_head(x_ref[rows, h * hd:(h + 1) * hd], cos_f, sin_f, half) for h in range(n_heads)
        ]

    @pl.when(i < n_prompt_tiles)
    def _prompt():
        rows = slice(None)
        cos_f, sin_f = cos_ref[...], sin_ref[...]
        for h, (qh, kh) in enumerate(zip(rotate(q_ref, rows, cos_f, sin_f), rotate(k_ref, rows, cos_f, sin_f))):
            qo_ref[:, h * hd:(h + 1) * hd] = qh
            kp_ref[:, h * hd:(h + 1) * hd] = kh
        vp_ref[...] = v_ref[...]

    @pl.when(i == n_prompt_tiles)
    def _sample():
        rows = slice(0, bs)
        cos_f, sin_f = cos_s_ref[...], sin_s_ref[...]
        qo_ref[...] = jnp.zeros(qo_ref.shape, qo_ref.dtype)
        for h, (qh, kh) in enumerate(zip(rotate(q_ref, rows, cos_f, sin_f), rotate(k_ref, rows, cos_f, sin_f))):
            qo_ref[0:bs, h * hd:(h + 1) * hd] = qh
            ks_ref[:, h * hd:(h + 1) * hd] = kh
        vs_ref[...] = v_ref[0:bs, :]


def _rope_tables(pos, hd):
    rot = hd // ROT_FRACTION
    half = rot // 2
    inv_freq = ROPE_THETA ** (-jnp.arange(half, dtype=F32) * 2.0 / rot)
    ang = pos.astype(F32)[:, None] * inv_freq[None, :]
    cos, sin = jnp.cos(ang), jnp.sin(ang)
    n = pos.shape[0]
    cos_f = jnp.concatenate([cos, cos, jnp.ones((n, hd - rot), F32)], axis=1)
    sin_f = jnp.concatenate([-sin, sin, jnp.zeros((n, hd - rot), F32)], axis=1)
    return cos_f, sin_f


def _rope(qkv, *, n_heads, hd, bp, seq, bs, past_len):
    m_tot = qkv.shape[0]
    hdim = n_heads * hd
    tm = TOKEN_TILE
    tps = seq // tm
    npt = bp * tps
    mp = bp * seq
    cos_p, sin_p = _rope_tables(jnp.arange(seq, dtype=I32), hd)
    cos_s, sin_s = _rope_tables(jnp.full((1,), past_len, I32), hd)
    col = lambda c: pl.BlockSpec((tm, hdim), lambda i: (i, c))
    tab = pl.BlockSpec((tm, hd), lambda i: (i % tps, 0))
    one = pl.BlockSpec((1, hd), lambda i: (0, 0))
    prow = pl.BlockSpec((tm, hdim), lambda i: (jnp.minimum(i, npt - 1), 0))
    srow = pl.BlockSpec((bs, hdim), lambda i: (0, 0))
    return pl.pallas_call(
        functools.partial(_rope_body, n_heads=n_heads, hd=hd, half=hd // ROT_FRACTION // 2,
                          n_prompt_tiles=npt, bs=bs),
        grid=(m_tot // tm,),
        in_specs=[col(0), col(1), col(2), tab, tab, one, one],
        out_specs=[pl.BlockSpec((tm, hdim), lambda i: (i, 0)), prow, prow, srow, srow],
        out_shape=[jax.ShapeDtypeStruct((m_tot, hdim), F32),
                   jax.ShapeDtypeStruct((mp, hdim), F32), jax.ShapeDtypeStruct((mp, hdim), F32),
                   jax.ShapeDtypeStruct((bs, hdim), F32), jax.ShapeDtypeStruct((bs, hdim), F32)],
        compiler_params=_params(("arbitrary",), 48),
        name="rope",
    )(qkv, qkv, qkv, cos_p, sin_p, cos_s, sin_s)


def _nt_dot(a, b, **kw):
    return lax.dot_general(a, b, (((1,), (1,)), ((), ())), preferred_element_type=F32, **kw)


def _attn_body(q_ref, k_ref, v_ref, o_ref, kmean_ref, kb_ref, vt_ref, *, nb, scale):
    blk = MOBA_BLOCK
    kmean_ref[...] = jnp.zeros(kmean_ref.shape, F32)
    for n in range(nb):
        rows = slice(n * blk, (n + 1) * blk)
        kmean_ref[n:n + 1, :] = jnp.sum(k_ref[rows, :], axis=0, keepdims=True) * (1.0 / blk)
        vt_ref[:, rows] = v_ref[rows, :].T.astype(BF16)
    kb_ref[...] = k_ref[...].astype(BF16)
    gate_all = _nt_dot(kmean_ref[...], q_ref[...], precision=lax.Precision.HIGHEST)
    bid = lax.broadcasted_iota(I32, (kmean_ref.shape[0], blk), 0)
    ki = lax.broadcasted_iota(I32, (blk, blk), 0)
    qi = lax.broadcasted_iota(I32, (blk, blk), 1)

    for j in range(nb):
        cols = slice(j * blk, (j + 1) * blk)
        gate = jnp.where(bid < j, gate_all[:, cols], NEG_INF)
        rank = jnp.zeros(gate.shape, I32)
        for m in range(j):
            gm = gate[m:m + 1, :]
            beats = (gm > gate) | ((gm == gate) & (m < bid))
            rank = rank + jnp.where(beats, 1, 0)
        sel = jnp.where((bid < j) & (rank < MOBA_TOPK), 1.0, 0.0)
        keys = (j + 1) * blk
        s = _nt_dot(kb_ref[0:keys, :], q_ref[cols, :].astype(BF16)) * (scale * LOG2_E)
        slabs = [jnp.where(sel[n:n + 1, :] > 0.0, s[n * blk:(n + 1) * blk, :], NEG_INF) for n in range(j)]
        slabs.append(jnp.where(ki <= qi, s[j * blk:keys, :], NEG_INF))
        top = slabs[0]
        for sl in slabs[1:]:
            top = jnp.maximum(top, sl)
        top = jnp.max(top, axis=0, keepdims=True)
        p = [jnp.exp2(sl - top) for sl in slabs]
        tot = p[0]
        for pn in p[1:]:
            tot = tot + pn
        denom = jnp.sum(tot, axis=0, keepdims=True)
        pv = jnp.dot(vt_ref[:, 0:keys], jnp.concatenate(p, axis=0).astype(BF16),
                     preferred_element_type=F32)
        o_ref[cols, :] = (pv / denom).T.astype(o_ref.dtype)


def _attn_prompt(q, k_p, v_p, *, bp, seq, n_heads, hd):
    nb = seq // MOBA_BLOCK
    nbp = -(-nb // SUBLANES) * SUBLANES
    blk = pl.BlockSpec((seq, hd), lambda b, h: (b, h))
    return pl.pallas_call(
        functools.partial(_attn_body, nb=nb, scale=hd ** -0.5),
        grid=(bp, n_heads),
        in_specs=[blk, blk, blk],
        out_specs=blk,
        out_shape=jax.ShapeDtypeStruct((bp * seq, n_heads * hd), BF16),
        scratch_shapes=[pltpu.VMEM((nbp, hd), F32), pltpu.VMEM((seq, hd), BF16), pltpu.VMEM((hd, seq), BF16)],
        compiler_params=_params(("arbitrary", "arbitrary"), 48),
        name="moba_prompt",
    )(q, k_p, v_p)


def _page_mean_body(pt_ref, *refs, pages_per_step, pages_per_block):
    del pt_ref
    pages, o_ref = refs[:pages_per_step], refs[pages_per_step]
    inv = 1.0 / MOBA_BLOCK
    for c in range(pages_per_step // pages_per_block):
        tot = jnp.sum(pages[c * pages_per_block][...], axis=0)
        for r in range(1, pages_per_block):
            tot = tot + jnp.sum(pages[c * pages_per_block + r][...], axis=0)
        o_ref[c] = tot * inv


def _page_means(cache_k, layer, page_table):
    _, _, page, n_heads, hd = cache_k.shape
    bs, n_pages = page_table.shape
    ppb = MOBA_BLOCK // page
    pps = _pick(n_pages, (16, 8, 4, 2)) if ppb == 2 else ppb
    n_blocks = n_pages // ppb

    def page_spec(c):
        return pl.BlockSpec((None, None, page, n_heads, hd),
                            lambda b, s, pt: (layer, pt[b * n_pages + s * pps + c], 0, 0, 0))

    return pl.pallas_call(
        functools.partial(_page_mean_body, pages_per_step=pps, pages_per_block=ppb),
        grid_spec=pltpu.PrefetchScalarGridSpec(
            num_scalar_prefetch=1,
            grid=(bs, n_pages // pps),
            in_specs=[page_spec(c) for c in range(pps)],
            out_specs=pl.BlockSpec((None, pps // ppb, n_heads, hd), lambda b, s, pt: (b, s, 0, 0)),
        ),
        out_shape=jax.ShapeDtypeStruct((bs, n_blocks, n_heads, hd), F32),
        compiler_params=_params(("arbitrary", "arbitrary"), 48),
        name="page_means",
    )(page_table.reshape(-1), *([cache_k] * pps))


def _select_body(km_ref, q_ref, o_ref, *, n_heads):
    n_blocks = km_ref.shape[1]
    res = jnp.zeros(o_ref.shape, I32)
    row = lax.broadcasted_iota(I32, o_ref.shape, 0)
    lane = lax.broadcasted_iota(I32, o_ref.shape, 1)
    blk_id = lax.broadcasted_iota(I32, (n_blocks, 1), 0)
    for h in range(n_heads):
        gate = jnp.sum(km_ref[h] * q_ref[h:h + 1, :], axis=1, keepdims=True)
        for k in range(MOBA_TOPK):
            best = jnp.max(gate, axis=0, keepdims=True)
            idx = jnp.min(jnp.where(gate == best, blk_id, n_blocks), axis=0, keepdims=True)
            res = jnp.where((row == k) & (lane == h), idx, res)
            gate = jnp.where(blk_id == idx, -jnp.inf, gate)
    o_ref[...] = res


def _select_blocks(kmean_t, q_s):
    bs, n_heads, n_blocks, hd = kmean_t.shape
    return pl.pallas_call(
        functools.partial(_select_body, n_heads=n_heads),
        grid=(bs,),
        in_specs=[pl.BlockSpec((None, n_heads, n_blocks, hd), lambda b: (b, 0, 0, 0)),
                  pl.BlockSpec((None, n_heads, hd), lambda b: (b, 0, 0))],
        out_specs=pl.BlockSpec((None, SUBLANES, LANES), lambda b: (b, 0, 0)),
        out_shape=jax.ShapeDtypeStruct((bs, SUBLANES, LANES), I32),
        compiler_params=_params(("arbitrary",), 32),
        name="moba_select",
    )(kmean_t, q_s.reshape(bs, n_heads, hd))


def _decode_attn_body(ph_ref, q_ref, kn_ref, vn_ref, ck_hbm, cv_hbm, o_ref, kbuf, vbuf, sem,
                      *, layer, bs, n_heads, n_sel_pages, scale):
    t = pl.program_id(0)
    n_steps = pl.num_programs(0)

    def page_copies(step, slot):
        h, b = step // bs, step % bs
        out = []
        for r in range(n_sel_pages):
            pg = ph_ref[(b * n_heads + h) * n_sel_pages + r]
            out.append(pltpu.make_async_copy(ck_hbm.at[layer, pg, :, h, :], kbuf.at[slot, r], sem.at[slot]))
            out.append(pltpu.make_async_copy(cv_hbm.at[layer, pg, :, h, :], vbuf.at[slot, r], sem.at[slot]))
        return out

    @pl.when(t == 0)
    def _first():
        for c in page_copies(t, 0):
            c.start()

    @pl.when(t + 1 < n_steps)
    def _prefetch():
        for c in page_copies(t + 1, (t + 1) % 2):
            c.start()

    slot = t % 2
    for c in page_copies(t, slot):
        c.wait()

    row = pl.ds(t % bs, 1)
    q = q_ref[row, :]
    s_new = jnp.sum(q * kn_ref[row, :], axis=1, keepdims=True) * scale
    scores = [jnp.sum(kbuf[slot, r] * q, axis=1, keepdims=True) * scale for r in range(n_sel_pages)]
    top = s_new
    for s in scores:
        top = jnp.maximum(top, jnp.max(s, axis=0, keepdims=True))
    p_new = jnp.exp(s_new - top)
    denom = p_new
    acc = p_new * vn_ref[row, :]
    for r, s in enumerate(scores):
        p = jnp.exp(s - top)
        denom = denom + jnp.sum(p, axis=0, keepdims=True)
        acc = acc + jnp.sum(p * vbuf[slot, r], axis=0, keepdims=True)
    o_ref[row, :] = acc / denom


def _decode_attn(q_s, k_s, v_s, cache_k, cache_v, layer, phys_pages, *, n_heads, hd):
    bs = q_s.shape[0]
    page = cache_k.shape[2]
    nsp = phys_pages.shape[-1]
    head_rows = pl.BlockSpec((bs, hd), lambda t, ph: (0, t // bs))
    anywhere = pl.BlockSpec(memory_space=pl.ANY)
    return pl.pallas_call(
        functools.partial(_decode_attn_body, layer=layer, bs=bs, n_heads=n_heads, n_sel_pages=nsp,
                          scale=hd ** -0.5),
        grid_spec=pltpu.PrefetchScalarGridSpec(
            num_scalar_prefetch=1,
            grid=(n_heads * bs,),
            in_specs=[head_rows, head_rows, head_rows, anywhere, anywhere],
            out_specs=head_rows,
            scratch_shapes=[pltpu.VMEM((2, nsp, page, hd), F32), pltpu.VMEM((2, nsp, page, hd), F32),
                            pltpu.SemaphoreType.DMA((2,))],
        ),
        out_shape=jax.ShapeDtypeStruct((bs, n_heads * hd), F32),
        compiler_params=_params(("arbitrary",), 32),
        name="moba_decode",
    )(phys_pages.reshape(-1), q_s, k_s, v_s, cache_k, cache_v)


def _gelu_tanh(x):
    return 0.5 * x * (1.0 + jnp.tanh(np.sqrt(2.0 / np.pi).astype(np.float32) * (x + 0.044715 * (x * x * x))))


def _softplus(x):
    return jnp.maximum(x, 0.0) + jnp.log1p(jnp.exp(-jnp.abs(x)))


def _lru_gates(uc, wa_ref, wx_ref, ba, bx, lam, n_heads, rb):
    ucb = uc.astype(BF16)
    a_parts, b_parts = [], []
    sp = _softplus(-lam)
    for h in range(n_heads):
        cols = slice(h * rb, (h + 1) * rb)
        r = jax.nn.sigmoid(jnp.dot(ucb[:, cols], wa_ref[h].astype(BF16), preferred_element_type=F32) + ba[:, cols])
        g = jax.nn.sigmoid(jnp.dot(ucb[:, cols], wx_ref[h].astype(BF16), preferred_element_type=F32) + bx[:, cols])
        log_a = -LRU_C * r * sp[:, cols]
        a_parts.append(jnp.exp(log_a))
        b_parts.append(jnp.sqrt(1.0 - jnp.exp(2.0 * log_a)) * (g * uc[:, cols]))
    return jnp.concatenate(a_parts, axis=1), jnp.concatenate(b_parts, axis=1)


def _lru_body(g_ref, u_ref, wc_ref, bc_ref, wa_ref, wx_ref, ba_ref, bx_ref, lam_ref,
              c0_ref, c1_ref, c2_ref, h0_ref,
              y_ref, newc_ref, newh_ref, us_ref, hs_ref,
              carry_ref, hcarry_ref, a_buf, b_buf, *, tiles_per_seq, n_prompt_tiles, bs, n_heads, rb):
    i = pl.program_id(0)
    tm = y_ref.shape[0]
    w0, w1, w2, w3 = wc_ref[0:1, :], wc_ref[1:2, :], wc_ref[2:3, :], wc_ref[3:4, :]
    bias = bc_ref[...]
    gates = functools.partial(_lru_gates, wa_ref=wa_ref, wx_ref=wx_ref, ba=ba_ref[...], bx=bx_ref[...],
                              lam=lam_ref[...], n_heads=n_heads, rb=rb)

    @pl.when(i < n_prompt_tiles)
    def _prompt():
        u = u_ref[...]
        fresh = i % tiles_per_seq == 0
        c0, c1, c2 = (jnp.where(fresh, 0.0, carry_ref[r:r + 1, :]) for r in range(3))
        p1 = _shift_rows(u, 1, (c2,))
        p2 = _shift_rows(u, 2, (c1, c2))
        p3 = _shift_rows(u, 3, (c0, c1, c2))
        uc = w0 * p3 + w1 * p2 + w2 * p1 + w3 * u + bias
        a, b = gates(uc)
        a_buf[...] = a
        b_buf[...] = b
        sub = lax.broadcasted_iota(I32, (SUBLANES, a.shape[1]), 0)

        def group(gi, h):
            rows = pl.ds(pl.multiple_of(gi * SUBLANES, SUBLANES), SUBLANES)
            ag, bg = a_buf[rows, :], b_buf[rows, :]
            for d in (1, 2, 4):
                bg = bg + ag * jnp.where(sub >= d, pltpu.roll(bg, d, 0), 0.0)
                ag = ag * jnp.where(sub >= d, pltpu.roll(ag, d, 0), 1.0)
            hg = ag * h + bg
            b_buf[rows, :] = hg
            return hg[SUBLANES - 1:SUBLANES, :]

        h_last = lax.fori_loop(0, tm // SUBLANES, group, jnp.where(fresh, 0.0, hcarry_ref[0:1, :]))
        hcarry_ref[0:1, :] = h_last
        y_ref[...] = (b_buf[...] * _gelu_tanh(g_ref[...])).astype(y_ref.dtype)
        carry_ref[0:3, :] = u[tm - 3:tm, :]
        newc_ref[...] = u[tm - 3:tm, :]
        newh_ref[...] = h_last

    @pl.when(i == n_prompt_tiles)
    def _sample():
        u = u_ref[0:bs, :]
        uc = w0 * c0_ref[...] + w1 * c1_ref[...] + w2 * c2_ref[...] + w3 * u + bias
        a, b = gates(uc)
        h = a * h0_ref[...] + b
        y_ref[...] = jnp.zeros(y_ref.shape, y_ref.dtype)
        y_ref[0:bs, :] = (h * _gelu_tanh(g_ref[0:bs, :])).astype(y_ref.dtype)
        us_ref[...] = u
        hs_ref[...] = h


def _rglru(proj, w_conv, b_conv, w_gate_a, b_gate_a, w_gate_x, b_gate_x, lam, layer, conv_state, h0,
           *, bp, seq, bs):
    m_tot = proj.shape[0]
    dr = proj.shape[1] // 2
    n_heads, rb = w_gate_a.shape[1], w_gate_a.shape[2]
    tm = TOKEN_TILE
    tps = seq // tm
    npt = bp * tps
    width = w_conv.shape[1]
    col = lambda c: pl.BlockSpec((tm, dr), lambda i: (i, c))
    vec = pl.BlockSpec((None, 1, dr), lambda i: (layer, 0, 0))
    gw = pl.BlockSpec((None, n_heads, rb, rb), lambda i: (layer, 0, 0, 0))
    st = pl.BlockSpec((bs, dr), lambda i: (0, 0))
    pb = lambda rows: pl.BlockSpec((None, rows, dr), lambda i: (jnp.minimum(i // tps, bp - 1), 0, 0))
    return pl.pallas_call(
        functools.partial(_lru_body, tiles_per_seq=tps, n_prompt_tiles=npt, bs=bs, n_heads=n_heads, rb=rb),
        grid=(m_tot // tm,),
        in_specs=[col(0), col(1), pl.BlockSpec((None, width, dr), lambda i: (layer, 0, 0)), vec,
                  gw, gw, vec, vec, vec, st, st, st, st],
        out_specs=[pl.BlockSpec((tm, dr), lambda i: (i, 0)), pb(width - 1), pb(1), st, st],
        out_shape=[jax.ShapeDtypeStruct((m_tot, dr), BF16),
                   jax.ShapeDtypeStruct((bp, width - 1, dr), F32),
                   jax.ShapeDtypeStruct((bp, 1, dr), F32),
                   jax.ShapeDtypeStruct((bs, dr), F32),
                   jax.ShapeDtypeStruct((bs, dr), F32)],
        scratch_shapes=[pltpu.VMEM((SUBLANES, dr), F32), pltpu.VMEM((SUBLANES, dr), F32),
                        pltpu.VMEM((tm, dr), F32), pltpu.VMEM((tm, dr), F32)],
        compiler_params=_params(("arbitrary",), 48),
        name="rglru",
    )(proj, proj, w_conv, b_conv.reshape(-1, 1, dr), w_gate_a, w_gate_x,
      b_gate_a.reshape(-1, 1, dr), b_gate_x.reshape(-1, 1, dr), lam.reshape(-1, 1, dr),
      conv_state[:, 0], conv_state[:, 1], conv_state[:, 2], h0)


def _ffn_body(e_ref, start_ref, nt_ref, nused_ref, xs_hbm, wg_ref, wu_ref, wd_ref, o_hbm,
              xbuf, acc, wgb, wub, wdb, sem_in, sem_out, *, tm, nf):
    del e_ref
    s, f = pl.program_id(0), pl.program_id(1)
    nt = nt_ref[s]
    start = start_ref[s]

    def copy_in(r):
        return pltpu.make_async_copy(xs_hbm.at[pl.ds(pl.multiple_of(start + r * tm, tm), tm), :],
                                     xbuf.at[pl.ds(pl.multiple_of(r * tm, tm), tm), :], sem_in)

    def copy_out(r):
        return pltpu.make_async_copy(acc.at[pl.ds(pl.multiple_of(r * tm, tm), tm), :],
                                     o_hbm.at[pl.ds(pl.multiple_of(start + r * tm, tm), tm), :], sem_out)

    def for_tiles(fn):
        def body(r, c):
            fn(r)
            return c
        lax.fori_loop(0, nt, body, 0)

    @pl.when(nt > 0)
    def _run():
        @pl.when(f == 0)
        def _load():
            for_tiles(lambda r: copy_in(r).start())

            def clear(r):
                acc[pl.ds(pl.multiple_of(r * tm, tm), tm), :] = jnp.zeros((tm, acc.shape[1]), F32)

            for_tiles(clear)
            for_tiles(lambda r: copy_in(r).wait())

        def rows_step(first_tile, n_tiles, cast_weights=False):
            rows = pl.ds(pl.multiple_of(first_tile * tm, tm), n_tiles * tm)
            x = xbuf[rows, :]
            if cast_weights:
                wgb[...] = wg_ref[...].astype(BF16)
            g = jnp.dot(x, wgb[...], preferred_element_type=F32)
            if cast_weights:
                wub[...] = wu_ref[...].astype(BF16)
            u = jnp.dot(x, wub[...], preferred_element_type=F32)
            h = (g * jax.nn.sigmoid(g) * u).astype(BF16)
            if cast_weights:
                wdb[...] = wd_ref[...].astype(BF16)
            acc[rows, :] += jnp.dot(h, wdb[...], preferred_element_type=F32)

            @pl.when(f == nf - 1)
            def _store():
                for r in range(n_tiles):
                    copy_out(first_tile + r).start()

        def pair(i, c):
            rows_step(2 * i, 2)
            return c

        n_pairs = nt // 2

        @pl.when(n_pairs > 0)
        def _first_pair():
            rows_step(0, 2, cast_weights=True)

        @pl.when(n_pairs == 0)
        def _only_single():
            rows_step(0, 1, cast_weights=True)

        lax.fori_loop(1, n_pairs, pair, 0)

        @pl.when((nt % 2 == 1) & (n_pairs > 0))
        def _last_single():
            rows_step(nt - 1, 1)

        @pl.when(f == nf - 1)
        def _drain_stores():
            for_tiles(lambda r: copy_out(r).wait())

    @pl.when((s == pl.num_programs(0) - 1) & (f == nf - 1))
    def _zero_unused():
        acc[0:tm, :] = jnp.zeros((tm, acc.shape[1]), F32)

        def zero_tile(t, c):
            cp = pltpu.make_async_copy(acc.at[0:tm, :], o_hbm.at[pl.ds(pl.multiple_of(t * tm, tm), tm), :], sem_out)
            cp.start()
            cp.wait()
            return c

        lax.fori_loop(nused_ref[0], o_hbm.shape[0] // tm, zero_tile, 0)


def _ffn(xs, w_gate, w_up, w_down, layer, sched_e, sched_start, sched_nt, n_used_tiles):
    s_rows, d = xs.shape
    dff = w_gate.shape[3]
    tm, tf, rt = TOKEN_TILE, FFN_COL_TILE, FFN_SUPER_TILES
    nf = dff // tf
    n_super = sched_e.shape[0]

    def fcol(s, f, nt_ref):
        return jnp.where(nt_ref[s] > 0, f, nf - 1)

    up_spec = pl.BlockSpec((None, None, d, tf), lambda s, f, e, st, nt, nu: (layer, e[s], 0, fcol(s, f, nt)))
    down_spec = pl.BlockSpec((None, None, tf, d), lambda s, f, e, st, nt, nu: (layer, e[s], fcol(s, f, nt), 0))
    return pl.pallas_call(
        functools.partial(_ffn_body, tm=tm, nf=nf),
        grid_spec=pltpu.PrefetchScalarGridSpec(
            num_scalar_prefetch=4,
            grid=(n_super, nf),
            in_specs=[pl.BlockSpec(memory_space=pl.ANY), up_spec, up_spec, down_spec],
            out_specs=pl.BlockSpec(memory_space=pl.ANY),
            scratch_shapes=[pltpu.VMEM((rt * tm, d), BF16), pltpu.VMEM((rt * tm, d), F32),
                            pltpu.VMEM((d, tf), BF16), pltpu.VMEM((d, tf), BF16), pltpu.VMEM((tf, d), BF16),
                            pltpu.SemaphoreType.DMA(()), pltpu.SemaphoreType.DMA(())],
        ),
        out_shape=jax.ShapeDtypeStruct((s_rows, d), F32),
        compiler_params=_params(("arbitrary", "arbitrary"), 56),
        name="swiglu_ffn",
    )(sched_e, sched_start, sched_nt, n_used_tiles, xs, w_gate, w_up, w_down)


def _dense_schedule(n_tiles):
    rt = FFN_SUPER_TILES
    n_super = -(-n_tiles // rt)
    start = np.arange(n_super, dtype=np.int32) * rt
    nt = np.minimum(rt, n_tiles - start).astype(np.int32)
    return (jnp.zeros((n_super,), I32), jnp.asarray(start * TOKEN_TILE, I32), jnp.asarray(nt, I32),
            jnp.full((1,), n_tiles, I32))


def _router_body(x_ref, w_ref, o_ref, *, n_exp):
    x, w = x_ref[...], w_ref[...]
    xh, wh = x.astype(BF16), w.astype(BF16)
    xl, wl = (x - xh.astype(F32)).astype(BF16), (w - wh.astype(F32)).astype(BF16)
    dot = functools.partial(jnp.dot, preferred_element_type=F32)
    logits = dot(xh, wh) + (dot(xh, wl) + dot(xl, wh))
    lane = lax.broadcasted_iota(I32, logits.shape, 1)
    lg = jnp.where(lane < n_exp, logits, -jnp.inf)
    m1 = jnp.max(lg, axis=1, keepdims=True)
    i1 = jnp.min(jnp.where(lg == m1, lane, LANES), axis=1, keepdims=True)
    lg = jnp.where(lane == i1, -jnp.inf, lg)
    m2 = jnp.max(lg, axis=1, keepdims=True)
    i2 = jnp.min(jnp.where(lg == m2, lane, LANES), axis=1, keepdims=True)
    e = jnp.exp(m2 - m1)
    g1 = 1.0 / (1.0 + e)
    g2 = e / (1.0 + e)
    o_ref[...] = jnp.where(lane == 0, i1.astype(F32),
                           jnp.where(lane == 1, i2.astype(F32),
                                     jnp.where(lane == 2, g1, jnp.where(lane == 3, g2, 0.0))))


def _router(x, w_router, layer):
    m, d = x.shape
    n_exp = w_router.shape[2]
    w_pad = jnp.pad(w_router, ((0, 0), (0, 0), (0, LANES - n_exp)))
    tm = TOKEN_TILE
    return pl.pallas_call(
        functools.partial(_router_body, n_exp=n_exp),
        grid=(m // tm,),
        in_specs=[pl.BlockSpec((tm, d), lambda i: (i, 0)),
                  pl.BlockSpec((None, d, LANES), lambda i: (layer, 0, 0))],
        out_specs=pl.BlockSpec((tm, LANES), lambda i: (i, 0)),
        out_shape=jax.ShapeDtypeStruct((m, LANES), F32),
        compiler_params=_params(("arbitrary",), 32),
        name="moe_router",
    )(x, w_pad)


def _moe_plan(route, n_valid, n_exp):
    m_tot = route.shape[0]
    tm, rt = TOKEN_TILE, FFN_SUPER_TILES
    n_tiles_max = (MOE_TOP_K * n_valid + n_exp * (tm - 1)) // tm
    s_rows = n_tiles_max * tm
    n_super = n_tiles_max // rt + n_exp
    e1, e2 = route[:, 0].astype(I32), route[:, 1].astype(I32)
    tok = jnp.arange(m_tot, dtype=I32)
    valid = tok < n_valid
    ids = jnp.arange(n_exp, dtype=I32)[None, :]
    oh1 = (e1[:, None] == ids) & valid[:, None]
    oh2 = (e2[:, None] == ids) & valid[:, None]
    oh = oh1.astype(I32) + oh2.astype(I32)
    csum = jnp.cumsum(oh, axis=0)
    before = csum - oh
    counts = csum[-1]
    tiles_e = (counts + tm - 1) // tm
    tile_end = jnp.cumsum(tiles_e)
    tile_off = tile_end - tiles_e
    row_off = tile_off * tm
    pos1 = jnp.sum(jnp.where(oh1, before + row_off[None, :], 0), axis=1)
    pos2 = jnp.sum(jnp.where(oh2, before + row_off[None, :], 0), axis=1)
    drop = jnp.where(valid, 0, s_rows)
    slot_token = jnp.zeros((s_rows,), I32).at[jnp.concatenate([pos1 + drop, pos2 + drop])].set(
        jnp.concatenate([tok, tok]), mode="drop")
    n_used_tiles = tile_end[-1:]
    super_e = -(-tiles_e // rt)
    super_end = jnp.cumsum(super_e)
    super_off = super_end - super_e
    n_used_super = super_end[-1]
    sidx = jnp.arange(n_super, dtype=I32)
    live = sidx < n_used_super
    sclamp = jnp.minimum(sidx, n_used_super - 1)
    exp_of = jnp.sum((sclamp[:, None] >= super_end[None, :]).astype(I32), axis=1)
    k = sclamp - super_off[exp_of]
    sched_start = (tile_off[exp_of] + k * rt) * tm
    sched_nt = jnp.where(live, jnp.clip(tiles_e[exp_of] - k * rt, 0, rt), 0)
    return pos1, pos2, slot_token, n_used_tiles, (exp_of, sched_start, sched_nt), s_rows


def _gather_body(tok_ref, nused_ref, x_hbm, o_ref, buf, sem, *, tm):
    t = pl.program_id(0)
    n_used = nused_ref[0]

    def issue_tile(tile, slot):
        def issue(g, c):
            for r in range(DMA_ISSUE_UNROLL):
                row = g * DMA_ISSUE_UNROLL + r
                pltpu.make_async_copy(x_hbm.at[pl.ds(tok_ref[tile * tm + row], 1), :],
                                      buf.at[slot, pl.ds(row, 1), :], sem.at[slot]).start()
            return c

        lax.fori_loop(0, tm // DMA_ISSUE_UNROLL, issue, 0)

    @pl.when((t == 0) & (n_used > 0))
    def _first():
        issue_tile(0, 0)

    @pl.when(t + 1 < n_used)
    def _prefetch():
        issue_tile(t + 1, (t + 1) % 2)

    @pl.when(t < n_used)
    def _used():
        slot = t % 2
        pltpu.make_async_copy(x_hbm.at[pl.ds(0, tm), :], buf.at[slot], sem.at[slot]).wait()
        o_ref[...] = buf[slot].astype(o_ref.dtype)

    @pl.when(t >= n_used)
    def _unused():
        o_ref[...] = jnp.zeros(o_ref.shape, o_ref.dtype)


def _gather_rows(x, slot_token, n_used_tiles):
    s_rows = slot_token.shape[0]
    d = x.shape[1]
    tm = TOKEN_TILE
    return pl.pallas_call(
        functools.partial(_gather_body, tm=tm),
        grid_spec=pltpu.PrefetchScalarGridSpec(
            num_scalar_prefetch=2,
            grid=(s_rows // tm,),
            in_specs=[pl.BlockSpec(memory_space=pl.ANY)],
            out_specs=pl.BlockSpec((tm, d), lambda t, tok, nu: (t, 0)),
            scratch_shapes=[pltpu.VMEM((2, tm, d), F32), pltpu.SemaphoreType.DMA((2,))],
        ),
        out_shape=jax.ShapeDtypeStruct((s_rows, d), BF16),
        compiler_params=_params(("arbitrary",), 32),
        name="moe_gather",
    )(slot_token, n_used_tiles, x)


def _combine_body(p1_ref, p2_ref, x_ref, route_ref, g_ref, b_ref, y_hbm, o_ref, ob_ref, buf, sem,
                  *, tm, alpha, n_prompt_tiles):
    t = pl.program_id(0)

    def issue_tile(tile, slot):
        def issue(g, c):
            for r in range(DMA_ISSUE_UNROLL):
                row = g * DMA_ISSUE_UNROLL + r
                for which, p_ref in enumerate((p1_ref, p2_ref)):
                    pltpu.make_async_copy(y_hbm.at[pl.ds(p_ref[tile * tm + row], 1), :],
                                          buf.at[slot, which, pl.ds(row, 1), :], sem.at[slot]).start()
            return c

        lax.fori_loop(0, tm // DMA_ISSUE_UNROLL, issue, 0)

    @pl.when(t == 0)
    def _first():
        issue_tile(0, 0)

    @pl.when(t + 1 < pl.num_programs(0))
    def _prefetch():
        issue_tile(t + 1, (t + 1) % 2)

    slot = t % 2
    for which in range(2):
        pltpu.make_async_copy(y_hbm.at[pl.ds(0, tm), :], buf.at[slot, which], sem.at[slot]).wait()
    route = route_ref[...]
    y = route[:, 2:3] * buf[slot, 0] + route[:, 3:4] * buf[slot, 1]
    out = _layer_norm(alpha * x_ref[...] + y, g_ref[...], b_ref[...])
    if n_prompt_tiles is None:
        o_ref[...] = out
        ob_ref[...] = out.astype(BF16)
    else:
        @pl.when(t < n_prompt_tiles)
        def _prompt():
            o_ref[...] = out

        @pl.when(t == n_prompt_tiles)
        def _sample():
            ob_ref[...] = out


def _combine_ln(x, route, y_slots, pos1, pos2, g_stack, b_stack, layer, alpha, n_prompt_rows=None):
    m, d = x.shape
    tm = TOKEN_TILE
    row = pl.BlockSpec((tm, d), lambda t, p1, p2: (t, 0))
    par = pl.BlockSpec((None, 1, d), lambda t, p1, p2: (layer, 0, 0))
    if n_prompt_rows is None:
        npt = None
        out_specs = [row, row]
        out_shape = [jax.ShapeDtypeStruct((m, d), F32), jax.ShapeDtypeStruct((m, d), BF16)]
    else:
        npt = n_prompt_rows // tm
        assert n_prompt_rows % tm == 0 and m == n_prompt_rows + tm
        out_specs = [pl.BlockSpec((tm, d), lambda t, p1, p2: (jnp.minimum(t, npt - 1), 0)),
                     pl.BlockSpec((tm, d), lambda t, p1, p2: (0, 0))]
        out_shape = [jax.ShapeDtypeStruct((n_prompt_rows, d), F32), jax.ShapeDtypeStruct((tm, d), F32)]
    return pl.pallas_call(
        functools.partial(_combine_body, tm=tm, alpha=alpha, n_prompt_tiles=npt),
        grid_spec=pltpu.PrefetchScalarGridSpec(
            num_scalar_prefetch=2,
            grid=(m // tm,),
            in_specs=[row, pl.BlockSpec((tm, LANES), lambda t, p1, p2: (t, 0)), par, par,
                      pl.BlockSpec(memory_space=pl.ANY)],
            out_specs=out_specs,
            scratch_shapes=[pltpu.VMEM((2, 2, tm, d), F32), pltpu.SemaphoreType.DMA((2,))],
        ),
        out_shape=out_shape,
        compiler_params=_params(("arbitrary",), 40),
        name="moe_combine_ln",
    )(pos1, pos2, x, route, g_stack.reshape(-1, 1, d), b_stack.reshape(-1, 1, d), y_slots)


def kernel(x_prompt, x_sample, cache_k, cache_v, page_table, state_conv, state_lru_conv, state_lru_h, ln1_g, ln1_b, ln2_g, ln2_b, sc_w_in, sc_w_conv, sc_w_out, attn_w_qkv, attn_w_o, lru_w_in, lru_w_conv, lru_b_conv, lru_w_gate_a, lru_b_gate_a, lru_w_gate_x, lru_b_gate_x, lru_lambda, lru_w_out, ffn_w_gate, ffn_w_up, ffn_w_down, moe_w_router, moe_w_gate, moe_w_up, moe_w_down):
    bp, seq, d = x_prompt.shape
    bs, dec_seq, _ = x_sample.shape
    depth = ln1_g.shape[0]
    n_heads, hd = cache_k.shape[3], cache_k.shape[4]
    page = cache_k.shape[2]
    past_len = page_table.shape[1] * page
    n_exp = moe_w_router.shape[2]
    assert dec_seq == 1 and bs <= TOKEN_TILE
    assert seq % MOBA_BLOCK == 0 and seq % TOKEN_TILE == 0 and past_len % MOBA_BLOCK == 0
    assert MOBA_BLOCK % page == 0 and past_len // MOBA_BLOCK >= MOBA_TOPK
    alpha = float((2.0 * depth) ** 0.25)
    mp = bp * seq
    n_valid = mp + bs
    m_tot = -(-n_valid // TOKEN_TILE) * TOKEN_TILE
    ppb = MOBA_BLOCK // page

    x = jnp.concatenate([x_prompt.reshape(mp, d), x_sample.reshape(bs, d),
                         jnp.zeros((m_tot - n_valid, d), x_prompt.dtype)], axis=0)
    xb = x.astype(BF16)
    dense_sched = _dense_schedule(m_tot // TOKEN_TILE)

    conv_p, conv_s, kp_l, vp_l, ks_l, vs_l, lc_p, lc_s, lh_p, lh_s = ([] for _ in range(10))
    final = None
    for i in range(depth):
        kind, m = i % 3, i // 3
        if kind == 0:
            proj = _matmul(xb, sc_w_in, m)
            y, new_p, u_s = _short_conv(proj, sc_w_conv, m, state_conv[m, :, 0], state_conv[m, :, 1],
                                        bp=bp, seq=seq, bs=bs)
            conv_p.append(new_p)
            conv_s.append(jnp.stack([state_conv[m, :, 1], u_s], axis=1))
            mixed, w_out = y, sc_w_out
        elif kind == 1:
            qkv = _matmul(xb, attn_w_qkv, m)
            q, k_p, v_p, k_s, v_s = _rope(qkv, n_heads=n_heads, hd=hd, bp=bp, seq=seq, bs=bs, past_len=past_len)
            kp_l.append(k_p.reshape(bp, seq, n_heads, hd))
            vp_l.append(v_p.reshape(bp, seq, n_heads, hd))
            ks_l.append(k_s.reshape(bs, 1, n_heads, hd))
            vs_l.append(v_s.reshape(bs, 1, n_heads, hd))
            o_p = _attn_prompt(q, k_p, v_p, bp=bp, seq=seq, n_heads=n_heads, hd=hd)
            q_s = q[mp:mp + bs]
            kmean = _page_means(cache_k, m, page_table)
            sel = _select_blocks(kmean.transpose(0, 2, 1, 3), q_s)
            sel = sel[:, :MOBA_TOPK, :n_heads].transpose(0, 2, 1)
            logical = sel[..., None] * ppb + jnp.arange(ppb, dtype=I32)
            phys = jnp.take_along_axis(page_table, logical.reshape(bs, -1), axis=1)
            o_s = _decode_attn(q_s, k_s, v_s, cache_k, cache_v, m, phys.reshape(bs, n_heads, -1),
                               n_heads=n_heads, hd=hd)
            o = jnp.concatenate([o_p, o_s.astype(BF16), jnp.zeros((m_tot - n_valid, n_heads * hd), BF16)], axis=0)
            mixed, w_out = o, attn_w_o
        else:
            proj = _matmul(xb, lru_w_in, m)
            y, new_c, new_h, u_s, h_s = _rglru(proj, lru_w_conv, lru_b_conv, lru_w_gate_a, lru_b_gate_a,
                                               lru_w_gate_x, lru_b_gate_x, lru_lambda, m,
                                               state_lru_conv[m], state_lru_h[m], bp=bp, seq=seq, bs=bs)
            lc_p.append(new_c)
            lc_s.append(jnp.concatenate([state_lru_conv[m, :, 1:], u_s[:, None, :]], axis=1))
            lh_p.append(new_h.reshape(bp, -1))
            lh_s.append(h_s)
            mixed, w_out = y, lru_w_out
        x, xb = _matmul_add_ln(mixed, w_out, m, x, ln1_g, ln1_b, i, alpha)
        j = i // 2
        if i % 2 == 0:
            f = _ffn(xb, ffn_w_gate[:, None], ffn_w_up[:, None], ffn_w_down[:, None], j, *dense_sched)
            x, xb = _add_ln(x, f, ln2_g, ln2_b, i, alpha)
        else:
            route = _router(x, moe_w_router, j)
            pos1, pos2, slot_token, n_used, sched, _ = _moe_plan(route, n_valid, n_exp)
            xs = _gather_rows(x, slot_token, n_used)
            ys = _ffn(xs, moe_w_gate, moe_w_up, moe_w_down, j, *sched, n_used)
            if i == depth - 1 and m_tot == mp + TOKEN_TILE:
                final = _combine_ln(x, route, ys, pos1, pos2, ln2_g, ln2_b, i, alpha, n_prompt_rows=mp)
            else:
                x, xb = _combine_ln(x, route, ys, pos1, pos2, ln2_g, ln2_b, i, alpha)
    x_p, x_s = final if final is not None else (x[:mp], x[mp:])
    return (x_p.reshape(bp, seq, d), x_s[:bs].reshape(bs, 1, d),
            jnp.stack(conv_p), jnp.stack(conv_s),
            jnp.stack(kp_l), jnp.stack(vp_l), jnp.stack(ks_l), jnp.stack(vs_l),
            jnp.stack(lc_p), jnp.stack(lc_s), jnp.stack(lh_p), jnp.stack(lh_s))
```

```python
import functools

import numpy as np
import jax
import jax.numpy as jnp
from jax import lax
from jax.experimental import pallas as pl
from jax.experimental.pallas import tpu as pltpu

F32, BF16, I32 = jnp.float32, jnp.bfloat16, jnp.int32

MOBA_BLOCK = 256
MOBA_TOPK = 3
ROPE_THETA = 500000.0
ROT_FRACTION = 4
LRU_C = 8.0
MOE_TOP_K = 2
LN_EPS = 1e-5
NEG_INF = -1e30
LOG2_E = 1.4426950408889634

LANES = 128
SUBLANES = 8
V7X_VMEM_BYTES = 64 * 1024 * 1024

TOKEN_TILE = 256
FFN_COL_TILE = 256
FFN_SUPER_TILES = 10
FFN_STEP_TILES = 4
DMA_ISSUE_UNROLL = 8


def _params(sem, vmem_mib, **kw):
    return pltpu.CompilerParams(dimension_semantics=sem, vmem_limit_bytes=vmem_mib * 1024 * 1024, **kw)


def _pick(total, candidates):
    for c in candidates:
        if total % c == 0:
            return c
    raise ValueError(f"no tile in {candidates} divides {total}")


def _mm_body(x_ref, w_ref, o_ref, wb_ref):
    @pl.when(pl.program_id(1) == 0)
    def _cast():
        wb_ref[...] = w_ref[...].astype(BF16)

    o_ref[...] = jnp.dot(x_ref[...], wb_ref[...], preferred_element_type=F32).astype(o_ref.dtype)


def _matmul(xb, w_stack, layer, out_dtype=F32):
    m, k = xb.shape
    n = w_stack.shape[2]
    tm = _pick(m, (768, 512, 256))
    tn = _pick(n, (1024, 512, 256, 128))
    return pl.pallas_call(
        _mm_body,
        grid=(n // tn, m // tm),
        in_specs=[pl.BlockSpec((tm, k), lambda j, i: (i, 0)),
                  pl.BlockSpec((None, k, tn), lambda j, i: (layer, 0, j))],
        out_specs=pl.BlockSpec((tm, tn), lambda j, i: (i, j)),
        out_shape=jax.ShapeDtypeStruct((m, n), out_dtype),
        scratch_shapes=[pltpu.VMEM((k, tn), BF16)],
        compiler_params=_params(("arbitrary", "arbitrary"), 48),
        name="matmul",
    )(xb, w_stack)


def _layer_norm(z, g, b):
    mu = jnp.mean(z, axis=-1, keepdims=True)
    zc = z - mu
    var = jnp.mean(zc * zc, axis=-1, keepdims=True)
    return zc * lax.rsqrt(var + LN_EPS) * g + b


def _mm_ln_body(y_ref, w_ref, x_ref, g_ref, b_ref, o_ref, ob_ref, wb_ref, *, alpha):
    @pl.when(pl.program_id(0) == 0)
    def _cast():
        wb_ref[...] = w_ref[...].astype(BF16)

    mix = jnp.dot(y_ref[...], wb_ref[...], preferred_element_type=F32)
    out = _layer_norm(alpha * x_ref[...] + mix, g_ref[...], b_ref[...])
    o_ref[...] = out
    ob_ref[...] = out.astype(BF16)


def _matmul_add_ln(yb, w_stack, w_layer, x, g_stack, b_stack, layer, alpha):
    m, k = yb.shape
    d = x.shape[1]
    assert w_stack.shape[2] == d
    tm = TOKEN_TILE
    row = lambda width: pl.BlockSpec((tm, width), lambda i: (i, 0))
    par = pl.BlockSpec((None, 1, d), lambda i: (layer, 0, 0))
    return pl.pallas_call(
        functools.partial(_mm_ln_body, alpha=alpha),
        grid=(m // tm,),
        in_specs=[row(k),
                  pl.BlockSpec((None, k, d), lambda i: (w_layer, 0, 0), pipeline_mode=pl.Buffered(1)),
                  row(d), par, par],
        out_specs=[row(d), row(d)],
        out_shape=[jax.ShapeDtypeStruct((m, d), F32), jax.ShapeDtypeStruct((m, d), BF16)],
        scratch_shapes=[pltpu.VMEM((k, d), BF16)],
        compiler_params=_params(("arbitrary",), 48),
        name="matmul_add_ln",
    )(yb, w_stack, x, g_stack.reshape(-1, 1, d), b_stack.reshape(-1, 1, d))


def _add_ln_body(x_ref, y_ref, g_ref, b_ref, o_ref, ob_ref, *, alpha):
    out = _layer_norm(alpha * x_ref[...] + y_ref[...], g_ref[...], b_ref[...])
    o_ref[...] = out
    ob_ref[...] = out.astype(BF16)


def _add_ln(x, y, g_stack, b_stack, layer, alpha):
    m, d = x.shape
    tm = TOKEN_TILE
    row = pl.BlockSpec((tm, d), lambda i: (i, 0))
    par = pl.BlockSpec((None, 1, d), lambda i: (layer, 0, 0))
    return pl.pallas_call(
        functools.partial(_add_ln_body, alpha=alpha),
        grid=(m // tm,),
        in_specs=[row, row, par, par],
        out_specs=[row, row],
        out_shape=[jax.ShapeDtypeStruct((m, d), F32), jax.ShapeDtypeStruct((m, d), BF16)],
        compiler_params=_params(("arbitrary",), 32),
        name="add_ln",
    )(x, y, g_stack.reshape(-1, 1, d), b_stack.reshape(-1, 1, d))


def _shift_rows(u, k, carry_rows):
    out = pltpu.roll(u, k, 0)
    row = lax.broadcasted_iota(I32, u.shape, 0)
    for r, c in enumerate(carry_rows):
        out = jnp.where(row == r, c, out)
    return out


def _sconv_body(b_ref, c_ref, h_ref, w_ref, s0_ref, s1_ref, y_ref, newp_ref, news_ref, carry_ref,
                *, tiles_per_seq, n_prompt_tiles, bs):
    i = pl.program_id(0)
    w0, w1, w2 = w_ref[0:1, :], w_ref[1:2, :], w_ref[2:3, :]
    tm = y_ref.shape[0]

    @pl.when(i < n_prompt_tiles)
    def _prompt():
        u = c_ref[...] * h_ref[...]
        fresh = i % tiles_per_seq == 0
        c0 = jnp.where(fresh, 0.0, carry_ref[0:1, :])
        c1 = jnp.where(fresh, 0.0, carry_ref[1:2, :])
        p1 = _shift_rows(u, 1, (c1,))
        p2 = _shift_rows(u, 2, (c0, c1))
        y = w0 * p2 + w1 * p1 + w2 * u
        y_ref[...] = (b_ref[...] * y).astype(y_ref.dtype)
        carry_ref[0:2, :] = u[tm - 2:tm, :]
        newp_ref[...] = u[tm - 2:tm, :]

    @pl.when(i == n_prompt_tiles)
    def _sample():
        u = c_ref[0:bs, :] * h_ref[0:bs, :]
        y = w0 * s0_ref[...] + w1 * s1_ref[...] + w2 * u
        y_ref[...] = jnp.zeros(y_ref.shape, y_ref.dtype)
        y_ref[0:bs, :] = (b_ref[0:bs, :] * y).astype(y_ref.dtype)
        news_ref[...] = u


def _short_conv(proj, w_conv, layer, s0, s1, *, bp, seq, bs):
    m_tot = proj.shape[0]
    d = proj.shape[1] // 3
    tm = TOKEN_TILE
    tps = seq // tm
    npt = bp * tps
    col = lambda c: pl.BlockSpec((tm, d), lambda i: (i, c))
    full = lambda shape: pl.BlockSpec(shape, lambda i: (0,) * len(shape))
    return pl.pallas_call(
        functools.partial(_sconv_body, tiles_per_seq=tps, n_prompt_tiles=npt, bs=bs),
        grid=(m_tot // tm,),
        in_specs=[col(0), col(1), col(2),
                  pl.BlockSpec((None, w_conv.shape[1], d), lambda i: (layer, 0, 0)),
                  full((bs, d)), full((bs, d))],
        out_specs=[pl.BlockSpec((tm, d), lambda i: (i, 0)),
                   pl.BlockSpec((None, 2, d), lambda i: (jnp.minimum(i // tps, bp - 1), 0, 0)),
                   full((bs, d))],
        out_shape=[jax.ShapeDtypeStruct((m_tot, d), BF16),
                   jax.ShapeDtypeStruct((bp, 2, d), F32),
                   jax.ShapeDtypeStruct((bs, d), F32)],
        scratch_shapes=[pltpu.VMEM((SUBLANES, d), F32)],
        compiler_params=_params(("arbitrary",), 40),
        name="short_conv",
    )(proj, proj, proj, w_conv, s0, s1)


def _rope_head(x, cos_f, sin_f, half):
    lane = lax.broadcasted_iota(I32, x.shape, 1)
    partner = jnp.where(lane < half, pltpu.roll(x, LANES - half, 1), pltpu.roll(x, half, 1))
    return x * cos_f + partner * sin_f


def _rope_body(q_ref, k_ref, v_ref, cos_ref, sin_ref, cos_s_ref, sin_s_ref,
               qo_ref, kp_ref, vp_ref, ks_ref, vs_ref, *, n_heads, hd, half, n_prompt_tiles, bs):
    i = pl.program_id(0)

    def rotate(x_ref, rows, cos_f, sin_f):
        return [
            _rope_head(x_ref[rows, h * hd:(h + 1) * hd], cos_f, sin_f, half) for h in range(n_heads)
        ]

    @pl.when(i < n_prompt_tiles)
    def _prompt():
        rows = slice(None)
        cos_f, sin_f = cos_ref[...], sin_ref[...]
        for h, (qh, kh) in enumerate(zip(rotate(q_ref, rows, cos_f, sin_f), rotate(k_ref, rows, cos_f, sin_f))):
            qo_ref[:, h * hd:(h + 1) * hd] = qh
            kp_ref[:, h * hd:(h + 1) * hd] = kh
        vp_ref[...] = v_ref[...]

    @pl.when(i == n_prompt_tiles)
    def _sample():
        rows = slice(0, bs)
        cos_f, sin_f = cos_s_ref[...], sin_s_ref[...]
        qo_ref[...] = jnp.zeros(qo_ref.shape, qo_ref.dtype)
        for h, (qh, kh) in enumerate(zip(rotate(q_ref, rows, cos_f, sin_f), rotate(k_ref, rows, cos_f, sin_f))):
            qo_ref[0:bs, h * hd:(h + 1) * hd] = qh
            ks_ref[:, h * hd:(h + 1) * hd] = kh
        vs_ref[...] = v_ref[0:bs, :]


def _rope_tables(pos, hd):
    rot = hd // ROT_FRACTION
    half = rot // 2
    inv_freq = ROPE_THETA ** (-jnp.arange(half, dtype=F32) * 2.0 / rot)
    ang = pos.astype(F32)[:, None] * inv_freq[None, :]
    cos, sin = jnp.cos(ang), jnp.sin(ang)
    n = pos.shape[0]
    cos_f = jnp.concatenate([cos, cos, jnp.ones((n, hd - rot), F32)], axis=1)
    sin_f = jnp.concatenate([-sin, sin, jnp.zeros((n, hd - rot), F32)], axis=1)
    return cos_f, sin_f


def _rope(qkv, *, n_heads, hd, bp, seq, bs, past_len):
    m_tot = qkv.shape[0]
    hdim = n_heads * hd
    tm = TOKEN_TILE
    tps = seq // tm
    npt = bp * tps
    mp = bp * seq
    cos_p, sin_p = _rope_tables(jnp.arange(seq, dtype=I32), hd)
    cos_s, sin_s = _rope_tables(jnp.full((1,), past_len, I32), hd)
    col = lambda c: pl.BlockSpec((tm, hdim), lambda i: (i, c))
    tab = pl.BlockSpec((tm, hd), lambda i: (i % tps, 0))
    one = pl.BlockSpec((1, hd), lambda i: (0, 0))
    prow = pl.BlockSpec((tm, hdim), lambda i: (jnp.minimum(i, npt - 1), 0))
    srow = pl.BlockSpec((bs, hdim), lambda i: (0, 0))
    return pl.pallas_call(
        functools.partial(_rope_body, n_heads=n_heads, hd=hd, half=hd // ROT_FRACTION // 2,
                          n_prompt_tiles=npt, bs=bs),
        grid=(m_tot // tm,),
        in_specs=[col(0), col(1), col(2), tab, tab, one, one],
        out_specs=[pl.BlockSpec((tm, hdim), lambda i: (i, 0)), prow, prow, srow, srow],
        out_shape=[jax.ShapeDtypeStruct((m_tot, hdim), F32),
                   jax.ShapeDtypeStruct((mp, hdim), F32), jax.ShapeDtypeStruct((mp, hdim), F32),
                   jax.ShapeDtypeStruct((bs, hdim), F32), jax.ShapeDtypeStruct((bs, hdim), F32)],
        compiler_params=_params(("arbitrary",), 48),
        name="rope",
    )(qkv, qkv, qkv, cos_p, sin_p, cos_s, sin_s)


def _nt_dot(a, b, **kw):
    return lax.dot_general(a, b, (((1,), (1,)), ((), ())), preferred_element_type=F32, **kw)


def _attn_body(q_ref, k_ref, v_ref, o_ref, kmean_ref, kb_ref, vt_ref, *, nb, scale):
    blk = MOBA_BLOCK
    kmean_ref[...] = jnp.zeros(kmean_ref.shape, F32)
    for n in range(nb):
        rows = slice(n * blk, (n + 1) * blk)
        kmean_ref[n:n + 1, :] = jnp.sum(k_ref[rows, :], axis=0, keepdims=True) * (1.0 / blk)
        vt_ref[:, rows] = v_ref[rows, :].T.astype(BF16)
    kb_ref[...] = k_ref[...].astype(BF16)
    gate_all = _nt_dot(kmean_ref[...], q_ref[...], precision=lax.Precision.HIGHEST)
    bid = lax.broadcasted_iota(I32, (kmean_ref.shape[0], blk), 0)
    ki = lax.broadcasted_iota(I32, (blk, blk), 0)
    qi = lax.broadcasted_iota(I32, (blk, blk), 1)

    for j in range(nb):
        cols = slice(j * blk, (j + 1) * blk)
        gate = jnp.where(bid < j, gate_all[:, cols], NEG_INF)
        rank = jnp.zeros(gate.shape, I32)
        for m in range(j):
            gm = gate[m:m + 1, :]
            beats = (gm > gate) | ((gm == gate) & (m < bid))
            rank = rank + jnp.where(beats, 1, 0)
        sel = jnp.where((bid < j) & (rank < MOBA_TOPK), 1.0, 0.0)
        keys = (j + 1) * blk
        s = _nt_dot(kb_ref[0:keys, :], q_ref[cols, :].astype(BF16)) * (scale * LOG2_E)
        slabs = [jnp.where(sel[n:n + 1, :] > 0.0, s[n * blk:(n + 1) * blk, :], NEG_INF) for n in range(j)]
        slabs.append(jnp.where(ki <= qi, s[j * blk:keys, :], NEG_INF))
        top = slabs[0]
        for sl in slabs[1:]:
            top = jnp.maximum(top, sl)
        top = jnp.max(top, axis=0, keepdims=True)
        p = [jnp.exp2(sl - top) for sl in slabs]
        tot = p[0]
        for pn in p[1:]:
            tot = tot + pn
        denom = jnp.sum(tot, axis=0, keepdims=True)
        pv = jnp.dot(vt_ref[:, 0:keys], jnp.concatenate(p, axis=0).astype(BF16),
                     preferred_element_type=F32)
        o_ref[cols, :] = (pv / denom).T.astype(o_ref.dtype)


def _attn_prompt(q, k_p, v_p, *, bp, seq, n_heads, hd):
    nb = seq // MOBA_BLOCK
    nbp = -(-nb // SUBLANES) * SUBLANES
    blk = pl.BlockSpec((seq, hd), lambda b, h: (b, h))
    return pl.pallas_call(
        functools.partial(_attn_body, nb=nb, scale=hd ** -0.5),
        grid=(bp, n_heads),
        in_specs=[blk, blk, blk],
        out_specs=blk,
        out_shape=jax.ShapeDtypeStruct((bp * seq, n_heads * hd), BF16),
        scratch_shapes=[pltpu.VMEM((nbp, hd), F32), pltpu.VMEM((seq, hd), BF16), pltpu.VMEM((hd, seq), BF16)],
        compiler_params=_params(("arbitrary", "arbitrary"), 48),
        name="moba_prompt",
    )(q, k_p, v_p)


def _page_mean_body(pt_ref, *refs, pages_per_step, pages_per_block):
    del pt_ref
    pages, o_ref = refs[:pages_per_step], refs[pages_per_step]
    inv = 1.0 / MOBA_BLOCK
    for c in range(pages_per_step // pages_per_block):
        tot = jnp.sum(pages[c * pages_per_block][...], axis=0)
        for r in range(1, pages_per_block):
            tot = tot + jnp.sum(pages[c * pages_per_block + r][...], axis=0)
        o_ref[c] = tot * inv


def _page_means(cache_k, layer, page_table):
    _, _, page, n_heads, hd = cache_k.shape
    bs, n_pages = page_table.shape
    ppb = MOBA_BLOCK // page
    pps = _pick(n_pages, (16, 8, 4, 2)) if ppb == 2 else ppb
    n_blocks = n_pages // ppb

    def page_spec(c):
        return pl.BlockSpec((None, None, page, n_heads, hd),
                            lambda b, s, pt: (layer, pt[b * n_pages + s * pps + c], 0, 0, 0))

    return pl.pallas_call(
        functools.partial(_page_mean_body, pages_per_step=pps, pages_per_block=ppb),
        grid_spec=pltpu.PrefetchScalarGridSpec(
            num_scalar_prefetch=1,
            grid=(bs, n_pages // pps),
            in_specs=[page_spec(c) for c in range(pps)],
            out_specs=pl.BlockSpec((None, pps // ppb, n_heads, hd), lambda b, s, pt: (b, s, 0, 0)),
        ),
        out_shape=jax.ShapeDtypeStruct((bs, n_blocks, n_heads, hd), F32),
        compiler_params=_params(("arbitrary", "arbitrary"), 48),
        name="page_means",
    )(page_table.reshape(-1), *([cache_k] * pps))


def _select_body(km_ref, q_ref, o_ref, *, n_heads):
    n_blocks = km_ref.shape[1]
    res = jnp.zeros(o_ref.shape, I32)
    row = lax.broadcasted_iota(I32, o_ref.shape, 0)
    lane = lax.broadcasted_iota(I32, o_ref.shape, 1)
    blk_id = lax.broadcasted_iota(I32, (n_blocks, 1), 0)
    for h in range(n_heads):
        gate = jnp.sum(km_ref[h] * q_ref[h:h + 1, :], axis=1, keepdims=True)
        for k in range(MOBA_TOPK):
            best = jnp.max(gate, axis=0, keepdims=True)
            idx = jnp.min(jnp.where(gate == best, blk_id, n_blocks), axis=0, keepdims=True)
            res = jnp.where((row == k) & (lane == h), idx, res)
            gate = jnp.where(blk_id == idx, -jnp.inf, gate)
    o_ref[...] = res


def _select_blocks(kmean_t, q_s):
    bs, n_heads, n_blocks, hd = kmean_t.shape
    return pl.pallas_call(
        functools.partial(_select_body, n_heads=n_heads),
        grid=(bs,),
        in_specs=[pl.BlockSpec((None, n_heads, n_blocks, hd), lambda b: (b, 0, 0, 0)),
                  pl.BlockSpec((None, n_heads, hd), lambda b: (b, 0, 0))],
        out_specs=pl.BlockSpec((None, SUBLANES, LANES), lambda b: (b, 0, 0)),
        out_shape=jax.ShapeDtypeStruct((bs, SUBLANES, LANES), I32),
        compiler_params=_params(("arbitrary",), 32),
        name="moba_select",
    )(kmean_t, q_s.reshape(bs, n_heads, hd))


def _decode_attn_body(ph_ref, q_ref, kn_ref, vn_ref, ck_hbm, cv_hbm, o_ref, kbuf, vbuf, sem,
                      *, layer, n_heads, hd, n_sel_pages, scale):
    b = pl.program_id(0)

    def start_copies(tok, slot):
        for h in range(n_heads):
            for r in range(n_sel_pages):
                pg = ph_ref[(tok * n_heads + h) * n_sel_pages + r]
                pltpu.make_async_copy(ck_hbm.at[layer, pg, :, h, :], kbuf.at[slot, h, r], sem.at[slot]).start()
                pltpu.make_async_copy(cv_hbm.at[layer, pg, :, h, :], vbuf.at[slot, h, r], sem.at[slot]).start()

    @pl.when(b == 0)
    def _first():
        start_copies(b, 0)

    @pl.when(b + 1 < pl.num_programs(0))
    def _prefetch():
        start_copies(b + 1, (b + 1) % 2)

    slot = b % 2
    pltpu.make_async_copy(kbuf.at[1 - slot], kbuf.at[slot], sem.at[slot]).wait()
    pltpu.make_async_copy(vbuf.at[1 - slot], vbuf.at[slot], sem.at[slot]).wait()

    row = pl.ds(b, 1)
    q_row, kn_row, vn_row = q_ref[row, :], kn_ref[row, :], vn_ref[row, :]
    out = []
    for h in range(n_heads):
        cols = slice(h * hd, (h + 1) * hd)
        q = q_row[:, cols]
        s_new = jnp.sum(q * kn_row[:, cols], axis=1, keepdims=True) * scale
        scores = [jnp.sum(kbuf[slot, h, r] * q, axis=1, keepdims=True) * scale for r in range(n_sel_pages)]
        top = s_new
        for s in scores:
            top = jnp.maximum(top, jnp.max(s, axis=0, keepdims=True))
        p_new = jnp.exp(s_new - top)
        denom = p_new
        acc = p_new * vn_row[:, cols]
        for r, s in enumerate(scores):
            p = jnp.exp(s - top)
            denom = denom + jnp.sum(p, axis=0, keepdims=True)
            acc = acc + jnp.sum(p * vbuf[slot, h, r], axis=0, keepdims=True)
        out.append(acc / denom)
    o_ref[row, :] = jnp.concatenate(out, axis=1)


def _decode_attn(q_s, k_s, v_s, cache_k, cache_v, layer, phys_pages, *, n_heads, hd):
    bs = q_s.shape[0]
    page = cache_k.shape[2]
    nsp = phys_pages.shape[-1]
    rows = pl.BlockSpec((bs, n_heads * hd), lambda b, ph: (0, 0))
    anywhere = pl.BlockSpec(memory_space=pl.ANY)
    return pl.pallas_call(
        functools.partial(_decode_attn_body, layer=layer, n_heads=n_heads, hd=hd, n_sel_pages=nsp,
                          scale=hd ** -0.5),
        grid_spec=pltpu.PrefetchScalarGridSpec(
            num_scalar_prefetch=1,
            grid=(bs,),
            in_specs=[rows, rows, rows, anywhere, anywhere],
            out_specs=rows,
            scratch_shapes=[pltpu.VMEM((2, n_heads, nsp, page, hd), F32),
                            pltpu.VMEM((2, n_heads, nsp, page, hd), F32),
                            pltpu.SemaphoreType.DMA((2,))],
        ),
        out_shape=jax.ShapeDtypeStruct((bs, n_heads * hd), F32),
        compiler_params=_params(("arbitrary",), 40),
        name="moba_decode",
    )(phys_pages.reshape(-1), q_s, k_s, v_s, cache_k, cache_v)


def _gelu_tanh(x):
    return 0.5 * x * (1.0 + jnp.tanh(np.sqrt(2.0 / np.pi).astype(np.float32) * (x + 0.044715 * (x * x * x))))


def _sigmoid(x):
    return 0.5 * (jnp.tanh(0.5 * x) + 1.0)


def _softplus(x):
    return jnp.maximum(x, 0.0) + jnp.log1p(jnp.exp(-jnp.abs(x)))


def _lru_gates(uc, wa_ref, wx_ref, ba, bx, lam, n_heads, rb):
    ucb = uc.astype(BF16)
    a_parts, b_parts = [], []
    sp = _softplus(-lam)
    for h in range(n_heads):
        cols = slice(h * rb, (h + 1) * rb)
        r = _sigmoid(jnp.dot(ucb[:, cols], wa_ref[h].astype(BF16), preferred_element_type=F32) + ba[:, cols])
        g = _sigmoid(jnp.dot(ucb[:, cols], wx_ref[h].astype(BF16), preferred_element_type=F32) + bx[:, cols])
        a = jnp.exp(-LRU_C * r * sp[:, cols])
        a_parts.append(a)
        b_parts.append(jnp.sqrt(1.0 - a * a) * (g * uc[:, cols]))
    return jnp.concatenate(a_parts, axis=1), jnp.concatenate(b_parts, axis=1)


def _lru_body(g_ref, u_ref, wc_ref, bc_ref, wa_ref, wx_ref, ba_ref, bx_ref, lam_ref,
              c0_ref, c1_ref, c2_ref, h0_ref,
              y_ref, newc_ref, newh_ref, us_ref, hs_ref,
              carry_ref, hcarry_ref, a_buf, b_buf, *, tiles_per_seq, n_prompt_tiles, bs, n_heads, rb):
    i = pl.program_id(0)
    tm = y_ref.shape[0]
    w0, w1, w2, w3 = wc_ref[0:1, :], wc_ref[1:2, :], wc_ref[2:3, :], wc_ref[3:4, :]
    bias = bc_ref[...]
    gates = functools.partial(_lru_gates, wa_ref=wa_ref, wx_ref=wx_ref, ba=ba_ref[...], bx=bx_ref[...],
                              lam=lam_ref[...], n_heads=n_heads, rb=rb)

    @pl.when(i < n_prompt_tiles)
    def _prompt():
        u = u_ref[...]
        fresh = i % tiles_per_seq == 0
        c0, c1, c2 = (jnp.where(fresh, 0.0, carry_ref[r:r + 1, :]) for r in range(3))
        p1 = _shift_rows(u, 1, (c2,))
        p2 = _shift_rows(u, 2, (c1, c2))
        p3 = _shift_rows(u, 3, (c0, c1, c2))
        uc = w0 * p3 + w1 * p2 + w2 * p1 + w3 * u + bias
        a, b = gates(uc)
        a_buf[...] = a
        b_buf[...] = b
        sub = lax.broadcasted_iota(I32, (SUBLANES, a.shape[1]), 0)

        def group(gi, h):
            rows = pl.ds(pl.multiple_of(gi * SUBLANES, SUBLANES), SUBLANES)
            ag, bg = a_buf[rows, :], b_buf[rows, :]
            for d in (1, 2, 4):
                bg = bg + ag * jnp.where(sub >= d, pltpu.roll(bg, d, 0), 0.0)
                ag = ag * jnp.where(sub >= d, pltpu.roll(ag, d, 0), 1.0)
            hg = ag * h + bg
            b_buf[rows, :] = hg
            return hg[SUBLANES - 1:SUBLANES, :]

        h_last = lax.fori_loop(0, tm // SUBLANES, group, jnp.where(fresh, 0.0, hcarry_ref[0:1, :]))
        hcarry_ref[0:1, :] = h_last
        y_ref[...] = (b_buf[...] * _gelu_tanh(g_ref[...])).astype(y_ref.dtype)
        carry_ref[0:3, :] = u[tm - 3:tm, :]
        newc_ref[...] = u[tm - 3:tm, :]
        newh_ref[...] = h_last

    @pl.when(i == n_prompt_tiles)
    def _sample():
        u = u_ref[0:bs, :]
        uc = w0 * c0_ref[...] + w1 * c1_ref[...] + w2 * c2_ref[...] + w3 * u + bias
        a, b = gates(uc)
        h = a * h0_ref[...] + b
        y_ref[...] = jnp.zeros(y_ref.shape, y_ref.dtype)
        y_ref[0:bs, :] = (h * _gelu_tanh(g_ref[0:bs, :])).astype(y_ref.dtype)
        us_ref[...] = u
        hs_ref[...] = h


def _rglru(proj, w_conv, b_conv, w_gate_a, b_gate_a, w_gate_x, b_gate_x, lam, layer, conv_state, h0,
           *, bp, seq, bs):
    m_tot = proj.shape[0]
    dr = proj.shape[1] // 2
    n_heads, rb = w_gate_a.shape[1], w_gate_a.shape[2]
    tm = TOKEN_TILE
    tps = seq // tm
    npt = bp * tps
    width = w_conv.shape[1]
    col = lambda c: pl.BlockSpec((tm, dr), lambda i: (i, c))
    vec = pl.BlockSpec((None, 1, dr), lambda i: (layer, 0, 0))
    gw = pl.BlockSpec((None, n_heads, rb, rb), lambda i: (layer, 0, 0, 0))
    st = pl.BlockSpec((bs, dr), lambda i: (0, 0))
    pb = lambda rows: pl.BlockSpec((None, rows, dr), lambda i: (jnp.minimum(i // tps, bp - 1), 0, 0))
    return pl.pallas_call(
        functools.partial(_lru_body, tiles_per_seq=tps, n_prompt_tiles=npt, bs=bs, n_heads=n_heads, rb=rb),
        grid=(m_tot // tm,),
        in_specs=[col(0), col(1), pl.BlockSpec((None, width, dr), lambda i: (layer, 0, 0)), vec,
                  gw, gw, vec, vec, vec, st, st, st, st],
        out_specs=[pl.BlockSpec((tm, dr), lambda i: (i, 0)), pb(width - 1), pb(1), st, st],
        out_shape=[jax.ShapeDtypeStruct((m_tot, dr), BF16),
                   jax.ShapeDtypeStruct((bp, width - 1, dr), F32),
                   jax.ShapeDtypeStruct((bp, 1, dr), F32),
                   jax.ShapeDtypeStruct((bs, dr), F32),
                   jax.ShapeDtypeStruct((bs, dr), F32)],
        scratch_shapes=[pltpu.VMEM((SUBLANES, dr), F32), pltpu.VMEM((SUBLANES, dr), F32),
                        pltpu.VMEM((tm, dr), F32), pltpu.VMEM((tm, dr), F32)],
        compiler_params=_params(("arbitrary",), 48),
        name="rglru",
    )(proj, proj, w_conv, b_conv.reshape(-1, 1, dr), w_gate_a, w_gate_x,
      b_gate_a.reshape(-1, 1, dr), b_gate_x.reshape(-1, 1, dr), lam.reshape(-1, 1, dr),
      conv_state[:, 0], conv_state[:, 1], conv_state[:, 2], h0)


def _ffn_body(e_ref, start_ref, nt_ref, nused_ref, xs_hbm, wg_ref, wu_ref, wd_ref, o_hbm,
              xbuf, acc, wgb, wub, wdb, sem_in, sem_out, *, tm, nf):
    del e_ref
    s, f = pl.program_id(0), pl.program_id(1)
    nt = nt_ref[s]
    start = start_ref[s]

    def copy_in(r):
        return pltpu.make_async_copy(xs_hbm.at[pl.ds(pl.multiple_of(start + r * tm, tm), tm), :],
                                     xbuf.at[pl.ds(pl.multiple_of(r * tm, tm), tm), :], sem_in)

    def copy_out(r):
        return pltpu.make_async_copy(acc.at[pl.ds(pl.multiple_of(r * tm, tm), tm), :],
                                     o_hbm.at[pl.ds(pl.multiple_of(start + r * tm, tm), tm), :], sem_out)

    def for_tiles(fn):
        def body(r, c):
            fn(r)
            return c
        lax.fori_loop(0, nt, body, 0)

    @pl.when(nt > 0)
    def _run():
        @pl.when(f == 0)
        def _load():
            for_tiles(lambda r: copy_in(r).start())

            def clear(r):
                acc[pl.ds(pl.multiple_of(r * tm, tm), tm), :] = jnp.zeros((tm, acc.shape[1]), F32)

            for_tiles(clear)
            for_tiles(lambda r: copy_in(r).wait())

        def rows_step(first_tile, n_tiles, cast_weights=False):
            rows = pl.ds(pl.multiple_of(first_tile * tm, tm), n_tiles * tm)
            x = xbuf[rows, :]
            if cast_weights:
                wgb[...] = wg_ref[...].astype(BF16)
            g = jnp.dot(x, wgb[...], preferred_element_type=F32)
            if cast_weights:
                wub[...] = wu_ref[...].astype(BF16)
            u = jnp.dot(x, wub[...], preferred_element_type=F32)
            h = (g * jax.nn.sigmoid(g) * u).astype(BF16)
            if cast_weights:
                wdb[...] = wd_ref[...].astype(BF16)
            acc[rows, :] += jnp.dot(h, wdb[...], preferred_element_type=F32)

            @pl.when(f == nf - 1)
            def _store():
                for r in range(n_tiles):
                    copy_out(first_tile + r).start()

        big = FFN_STEP_TILES
        n_big = nt // big
        rem = nt % big

        @pl.when(n_big > 0)
        def _first_big():
            rows_step(0, big, cast_weights=True)

        def big_step(i, c):
            rows_step(big * i, big)
            return c

        lax.fori_loop(1, n_big, big_step, 0)

        size = big // 2
        while size >= 1:
            present = (rem // size) % 2 == 1
            first = (n_big == 0) & (rem < 2 * size)
            offset = n_big * big + rem - rem % (2 * size)

            @pl.when(present & first)
            def _piece_first(size=size):
                rows_step(0, size, cast_weights=True)

            @pl.when(present & jnp.logical_not(first))
            def _piece(size=size, offset=offset):
                rows_step(offset, size)

            size //= 2

        @pl.when(f == nf - 1)
        def _drain_stores():
            for_tiles(lambda r: copy_out(r).wait())

    @pl.when((s == pl.num_programs(0) - 1) & (f == nf - 1))
    def _zero_unused():
        acc[0:tm, :] = jnp.zeros((tm, acc.shape[1]), F32)

        def zero_tile(t, c):
            cp = pltpu.make_async_copy(acc.at[0:tm, :], o_hbm.at[pl.ds(pl.multiple_of(t * tm, tm), tm), :], sem_out)
            cp.start()
            cp.wait()
            return c

        lax.fori_loop(nused_ref[0], o_hbm.shape[0] // tm, zero_tile, 0)


def _ffn(xs, w_gate, w_up, w_down, layer, sched_e, sched_start, sched_nt, n_used_tiles):
    s_rows, d = xs.shape
    dff = w_gate.shape[3]
    tm, tf, rt = TOKEN_TILE, FFN_COL_TILE, FFN_SUPER_TILES
    nf = dff // tf
    n_super = sched_e.shape[0]

    def fcol(s, f, nt_ref):
        return jnp.where(nt_ref[s] > 0, f, nf - 1)

    up_spec = pl.BlockSpec((None, None, d, tf), lambda s, f, e, st, nt, nu: (layer, e[s], 0, fcol(s, f, nt)))
    down_spec = pl.BlockSpec((None, None, tf, d), lambda s, f, e, st, nt, nu: (layer, e[s], fcol(s, f, nt), 0))
    return pl.pallas_call(
        functools.partial(_ffn_body, tm=tm, nf=nf),
        grid_spec=pltpu.PrefetchScalarGridSpec(
            num_scalar_prefetch=4,
            grid=(n_super, nf),
            in_specs=[pl.BlockSpec(memory_space=pl.ANY), up_spec, up_spec, down_spec],
            out_specs=pl.BlockSpec(memory_space=pl.ANY),
            scratch_shapes=[pltpu.VMEM((rt * tm, d), BF16), pltpu.VMEM((rt * tm, d), F32),
                            pltpu.VMEM((d, tf), BF16), pltpu.VMEM((d, tf), BF16), pltpu.VMEM((tf, d), BF16),
                            pltpu.SemaphoreType.DMA(()), pltpu.SemaphoreType.DMA(())],
        ),
        out_shape=jax.ShapeDtypeStruct((s_rows, d), F32),
        compiler_params=_params(("arbitrary", "arbitrary"), 56),
        name="swiglu_ffn",
    )(sched_e, sched_start, sched_nt, n_used_tiles, xs, w_gate, w_up, w_down)


def _dense_schedule(n_tiles):
    rt = FFN_SUPER_TILES
    n_super = -(-n_tiles // rt)
    start = np.arange(n_super, dtype=np.int32) * rt
    nt = np.minimum(rt, n_tiles - start).astype(np.int32)
    return (jnp.zeros((n_super,), I32), jnp.asarray(start * TOKEN_TILE, I32), jnp.asarray(nt, I32),
            jnp.full((1,), n_tiles, I32))


def _router_body(x_ref, w_ref, o_ref, *, n_exp):
    x, w = x_ref[...], w_ref[...]
    xh, wh = x.astype(BF16), w.astype(BF16)
    xl, wl = (x - xh.astype(F32)).astype(BF16), (w - wh.astype(F32)).astype(BF16)
    dot = functools.partial(jnp.dot, preferred_element_type=F32)
    logits = dot(xh, wh) + (dot(xh, wl) + dot(xl, wh))
    lane = lax.broadcasted_iota(I32, logits.shape, 1)
    lg = jnp.where(lane < n_exp, logits, -jnp.inf)
    m1 = jnp.max(lg, axis=1, keepdims=True)
    i1 = jnp.min(jnp.where(lg == m1, lane, LANES), axis=1, keepdims=True)
    lg = jnp.where(lane == i1, -jnp.inf, lg)
    m2 = jnp.max(lg, axis=1, keepdims=True)
    i2 = jnp.min(jnp.where(lg == m2, lane, LANES), axis=1, keepdims=True)
    e = jnp.exp(m2 - m1)
    g1 = 1.0 / (1.0 + e)
    g2 = e / (1.0 + e)
    o_ref[...] = jnp.where(lane == 0, i1.astype(F32),
                           jnp.where(lane == 1, i2.astype(F32),
                                     jnp.where(lane == 2, g1, jnp.where(lane == 3, g2, 0.0))))


def _router(x, w_router, layer):
    m, d = x.shape
    n_exp = w_router.shape[2]
    w_pad = jnp.pad(w_router, ((0, 0), (0, 0), (0, LANES - n_exp)))
    tm = TOKEN_TILE
    return pl.pallas_call(
        functools.partial(_router_body, n_exp=n_exp),
        grid=(m // tm,),
        in_specs=[pl.BlockSpec((tm, d), lambda i: (i, 0)),
                  pl.BlockSpec((None, d, LANES), lambda i: (layer, 0, 0))],
        out_specs=pl.BlockSpec((tm, LANES), lambda i: (i, 0)),
        out_shape=jax.ShapeDtypeStruct((m, LANES), F32),
        compiler_params=_params(("arbitrary",), 32),
        name="moe_router",
    )(x, w_pad)


def _moe_plan(route, n_valid, n_exp):
    m_tot = route.shape[0]
    tm, rt = TOKEN_TILE, FFN_SUPER_TILES
    n_tiles_max = (MOE_TOP_K * n_valid + n_exp * (tm - 1)) // tm
    s_rows = n_tiles_max * tm
    n_super = n_tiles_max // rt + n_exp
    e1, e2 = route[:, 0].astype(I32), route[:, 1].astype(I32)
    tok = jnp.arange(m_tot, dtype=I32)
    valid = tok < n_valid
    ids = jnp.arange(n_exp, dtype=I32)[None, :]
    oh1 = (e1[:, None] == ids) & valid[:, None]
    oh2 = (e2[:, None] == ids) & valid[:, None]
    oh = oh1.astype(I32) + oh2.astype(I32)
    csum = jnp.cumsum(oh, axis=0)
    before = csum - oh
    counts = csum[-1]
    tiles_e = (counts + tm - 1) // tm
    tile_end = jnp.cumsum(tiles_e)
    tile_off = tile_end - tiles_e
    row_off = tile_off * tm
    pos1 = jnp.sum(jnp.where(oh1, before + row_off[None, :], 0), axis=1)
    pos2 = jnp.sum(jnp.where(oh2, before + row_off[None, :], 0), axis=1)
    drop = jnp.where(valid, 0, s_rows)
    slot_token = jnp.zeros((s_rows,), I32).at[jnp.concatenate([pos1 + drop, pos2 + drop])].set(
        jnp.concatenate([tok, tok]), mode="drop")
    n_used_tiles = tile_end[-1:]
    super_e = -(-tiles_e // rt)
    super_end = jnp.cumsum(super_e)
    super_off = super_end - super_e
    n_used_super = super_end[-1]
    sidx = jnp.arange(n_super, dtype=I32)
    live = sidx < n_used_super
    sclamp = jnp.minimum(sidx, n_used_super - 1)
    exp_of = jnp.sum((sclamp[:, None] >= super_end[None, :]).astype(I32), axis=1)
    k = sclamp - super_off[exp_of]
    sched_start = (tile_off[exp_of] + k * rt) * tm
    sched_nt = jnp.where(live, jnp.clip(tiles_e[exp_of] - k * rt, 0, rt), 0)
    return pos1, pos2, slot_token, n_used_tiles, (exp_of, sched_start, sched_nt), s_rows


def _gather_body(tok_ref, nused_ref, x_hbm, o_ref, buf, sem, *, tm):
    t = pl.program_id(0)
    n_used = nused_ref[0]

    def issue_tile(tile, slot):
        def issue(g, c):
            for r in range(DMA_ISSUE_UNROLL):
                row = g * DMA_ISSUE_UNROLL + r
                pltpu.make_async_copy(x_hbm.at[pl.ds(tok_ref[tile * tm + row], 1), :],
                                      buf.at[slot, pl.ds(row, 1), :], sem.at[slot]).start()
            return c

        lax.fori_loop(0, tm // DMA_ISSUE_UNROLL, issue, 0)

    @pl.when((t == 0) & (n_used > 0))
    def _first():
        issue_tile(0, 0)

    @pl.when(t + 1 < n_used)
    def _prefetch():
        issue_tile(t + 1, (t + 1) % 2)

    @pl.when(t < n_used)
    def _used():
        slot = t % 2
        pltpu.make_async_copy(x_hbm.at[pl.ds(0, tm), :], buf.at[slot], sem.at[slot]).wait()
        o_ref[...] = buf[slot].astype(o_ref.dtype)

    @pl.when(t >= n_used)
    def _unused():
        o_ref[...] = jnp.zeros(o_ref.shape, o_ref.dtype)


def _gather_rows(x, slot_token, n_used_tiles):
    s_rows = slot_token.shape[0]
    d = x.shape[1]
    tm = TOKEN_TILE
    return pl.pallas_call(
        functools.partial(_gather_body, tm=tm),
        grid_spec=pltpu.PrefetchScalarGridSpec(
            num_scalar_prefetch=2,
            grid=(s_rows // tm,),
            in_specs=[pl.BlockSpec(memory_space=pl.ANY)],
            out_specs=pl.BlockSpec((tm, d), lambda t, tok, nu: (t, 0)),
            scratch_shapes=[pltpu.VMEM((2, tm, d), F32), pltpu.SemaphoreType.DMA((2,))],
        ),
        out_shape=jax.ShapeDtypeStruct((s_rows, d), BF16),
        compiler_params=_params(("arbitrary",), 32),
        name="moe_gather",
    )(slot_token, n_used_tiles, x)


def _combine_body(p1_ref, p2_ref, x_ref, route_ref, g_ref, b_ref, y_hbm, o_ref, ob_ref, buf, sem,
                  *, tm, alpha, n_prompt_tiles):
    t = pl.program_id(0)

    def issue_tile(tile, slot):
        def issue(g, c):
            for r in range(DMA_ISSUE_UNROLL):
                row = g * DMA_ISSUE_UNROLL + r
                for which, p_ref in enumerate((p1_ref, p2_ref)):
                    pltpu.make_async_copy(y_hbm.at[pl.ds(p_ref[tile * tm + row], 1), :],
                                          buf.at[slot, which, pl.ds(row, 1), :], sem.at[slot]).start()
            return c

        lax.fori_loop(0, tm // DMA_ISSUE_UNROLL, issue, 0)

    @pl.when(t == 0)
    def _first():
        issue_tile(0, 0)

    @pl.when(t + 1 < pl.num_programs(0))
    def _prefetch():
        issue_tile(t + 1, (t + 1) % 2)

    slot = t % 2
    for which in range(2):
        pltpu.make_async_copy(y_hbm.at[pl.ds(0, tm), :], buf.at[slot, which], sem.at[slot]).wait()
    route = route_ref[...]
    y = route[:, 2:3] * buf[slot, 0] + route[:, 3:4] * buf[slot, 1]
    out = _layer_norm(alpha * x_ref[...] + y, g_ref[...], b_ref[...])
    if n_prompt_tiles is None:
        o_ref[...] = out
        ob_ref[...] = out.astype(BF16)
    else:
        @pl.when(t < n_prompt_tiles)
        def _prompt():
            o_ref[...] = out

        @pl.when(t == n_prompt_tiles)
        def _sample():
            ob_ref[...] = out


def _combine_ln(x, route, y_slots, pos1, pos2, g_stack, b_stack, layer, alpha, n_prompt_rows=None):
    m, d = x.shape
    tm = TOKEN_TILE
    row = pl.BlockSpec((tm, d), lambda t, p1, p2: (t, 0))
    par = pl.BlockSpec((None, 1, d), lambda t, p1, p2: (layer, 0, 0))
    if n_prompt_rows is None:
        npt = None
        out_specs = [row, row]
        out_shape = [jax.ShapeDtypeStruct((m, d), F32), jax.ShapeDtypeStruct((m, d), BF16)]
    else:
        npt = n_prompt_rows // tm
        assert n_prompt_rows % tm == 0 and m == n_prompt_rows + tm
        out_specs = [pl.BlockSpec((tm, d), lambda t, p1, p2: (jnp.minimum(t, npt - 1), 0)),
                     pl.BlockSpec((tm, d), lambda t, p1, p2: (0, 0))]
        out_shape = [jax.ShapeDtypeStruct((n_prompt_rows, d), F32), jax.ShapeDtypeStruct((tm, d), F32)]
    return pl.pallas_call(
        functools.partial(_combine_body, tm=tm, alpha=alpha, n_prompt_tiles=npt),
        grid_spec=pltpu.PrefetchScalarGridSpec(
            num_scalar_prefetch=2,
            grid=(m // tm,),
            in_specs=[row, pl.BlockSpec((tm, LANES), lambda t, p1, p2: (t, 0)), par, par,
                      pl.BlockSpec(memory_space=pl.ANY)],
            out_specs=out_specs,
            scratch_shapes=[pltpu.VMEM((2, 2, tm, d), F32), pltpu.SemaphoreType.DMA((2,))],
        ),
        out_shape=out_shape,
        compiler_params=_params(("arbitrary",), 40),
        name="moe_combine_ln",
    )(pos1, pos2, x, route, g_stack.reshape(-1, 1, d), b_stack.reshape(-1, 1, d), y_slots)


def kernel(x_prompt, x_sample, cache_k, cache_v, page_table, state_conv, state_lru_conv, state_lru_h, ln1_g, ln1_b, ln2_g, ln2_b, sc_w_in, sc_w_conv, sc_w_out, attn_w_qkv, attn_w_o, lru_w_in, lru_w_conv, lru_b_conv, lru_w_gate_a, lru_b_gate_a, lru_w_gate_x, lru_b_gate_x, lru_lambda, lru_w_out, ffn_w_gate, ffn_w_up, ffn_w_down, moe_w_router, moe_w_gate, moe_w_up, moe_w_down):
    bp, seq, d = x_prompt.shape
    bs, dec_seq, _ = x_sample.shape
    depth = ln1_g.shape[0]
    n_heads, hd = cache_k.shape[3], cache_k.shape[4]
    page = cache_k.shape[2]
    past_len = page_table.shape[1] * page
    n_exp = moe_w_router.shape[2]
    assert dec_seq == 1 and bs <= TOKEN_TILE
    assert seq % MOBA_BLOCK == 0 and seq % TOKEN_TILE == 0 and past_len % MOBA_BLOCK == 0
    assert MOBA_BLOCK % page == 0 and past_len // MOBA_BLOCK >= MOBA_TOPK
    alpha = float((2.0 * depth) ** 0.25)
    mp = bp * seq
    n_valid = mp + bs
    m_tot = -(-n_valid // TOKEN_TILE) * TOKEN_TILE
    ppb = MOBA_BLOCK // page

    x = jnp.concatenate([x_prompt.reshape(mp, d), x_sample.reshape(bs, d),
                         jnp.zeros((m_tot - n_valid, d), x_prompt.dtype)], axis=0)
    xb = x.astype(BF16)
    dense_sched = _dense_schedule(m_tot // TOKEN_TILE)

    conv_p, conv_s, kp_l, vp_l, ks_l, vs_l, lc_p, lc_s, lh_p, lh_s = ([] for _ in range(10))
    final = None
    for i in range(depth):
        kind, m = i % 3, i // 3
        if kind == 0:
            proj = _matmul(xb, sc_w_in, m)
            y, new_p, u_s = _short_conv(proj, sc_w_conv, m, state_conv[m, :, 0], state_conv[m, :, 1],
                                        bp=bp, seq=seq, bs=bs)
            conv_p.append(new_p)
            conv_s.append(jnp.stack([state_conv[m, :, 1], u_s], axis=1))
            mixed, w_out = y, sc_w_out
        elif kind == 1:
            qkv = _matmul(xb, attn_w_qkv, m)
            q, k_p, v_p, k_s, v_s = _rope(qkv, n_heads=n_heads, hd=hd, bp=bp, seq=seq, bs=bs, past_len=past_len)
            kp_l.append(k_p.reshape(bp, seq, n_heads, hd))
            vp_l.append(v_p.reshape(bp, seq, n_heads, hd))
            ks_l.append(k_s.reshape(bs, 1, n_heads, hd))
            vs_l.append(v_s.reshape(bs, 1, n_heads, hd))
            o_p = _attn_prompt(q, k_p, v_p, bp=bp, seq=seq, n_heads=n_heads, hd=hd)
            q_s = q[mp:mp + bs]
            kmean = _page_means(cache_k, m, page_table)
            sel = _select_blocks(kmean.transpose(0, 2, 1, 3), q_s)
            sel = sel[:, :MOBA_TOPK, :n_heads].transpose(0, 2, 1)
            logical = sel[..., None] * ppb + jnp.arange(ppb, dtype=I32)
            phys = jnp.take_along_axis(page_table, logical.reshape(bs, -1), axis=1)
            o_s = _decode_attn(q_s, k_s, v_s, cache_k, cache_v, m, phys.reshape(bs, n_heads, -1),
                               n_heads=n_heads, hd=hd)
            o = jnp.concatenate([o_p, o_s.astype(BF16), jnp.zeros((m_tot - n_valid, n_heads * hd), BF16)], axis=0)
            mixed, w_out = o, attn_w_o
        else:
            proj = _matmul(xb, lru_w_in, m)
            y, new_c, new_h, u_s, h_s = _rglru(proj, lru_w_conv, lru_b_conv, lru_w_gate_a, lru_b_gate_a,
                                               lru_w_gate_x, lru_b_gate_x, lru_lambda, m,
                                               state_lru_conv[m], state_lru_h[m], bp=bp, seq=seq, bs=bs)
            lc_p.append(new_c)
            lc_s.append(jnp.concatenate([state_lru_conv[m, :, 1:], u_s[:, None, :]], axis=1))
            lh_p.append(new_h.reshape(bp, -1))
            lh_s.append(h_s)
            mixed, w_out = y, lru_w_out
        x, xb = _matmul_add_ln(mixed, w_out, m, x, ln1_g, ln1_b, i, alpha)
        j = i // 2
        if i % 2 == 0:
            f = _ffn(xb, ffn_w_gate[:, None], ffn_w_up[:, None], ffn_w_down[:, None], j, *dense_sched)
            x, xb = _add_ln(x, f, ln2_g, ln2_b, i, alpha)
        else:
            route = _router(x, moe_w_router, j)
            pos1, pos2, slot_token, n_used, sched, _ = _moe_plan(route, n_valid, n_exp)
            xs = _gather_rows(x, slot_token, n_used)
            ys = _ffn(xs, moe_w_gate, moe_w_up, moe_w_down, j, *sched, n_used)
            if i == depth - 1 and m_tot == mp + TOKEN_TILE:
                final = _combine_ln(x, route, ys, pos1, pos2, ln2_g, ln2_b, i, alpha, n_prompt_rows=mp)
            else:
                x, xb = _combine_ln(x, route, ys, pos1, pos2, ln2_g, ln2_b, i, alpha)
    x_p, x_s = final if final is not None else (x[:mp], x[mp:])
    return (x_p.reshape(bp, seq, d), x_s[:bs].reshape(bs, 1, d),
            jnp.stack(conv_p), jnp.stack(conv_s),
            jnp.stack(kp_l), jnp.stack(vp_l), jnp.stack(ks_l), jnp.stack(vs_l),
            jnp.stack(lc_p), jnp.stack(lc_s), jnp.stack(lh_p), jnp.stack(lh_s))
```

```python
import functools

import numpy as np
import jax
import jax.numpy as jnp
from jax import lax
from jax.experimental import pallas as pl
from jax.experimental.pallas import tpu as pltpu

F32, BF16, I32 = jnp.float32, jnp.bfloat16, jnp.int32

MOBA_BLOCK = 256
MOBA_TOPK = 3
ROPE_THETA = 500000.0
ROT_FRACTION = 4
LRU_C = 8.0
MOE_TOP_K = 2
LN_EPS = 1e-5
NEG_INF = -1e30
LOG2_E = 1.4426950408889634

LANES = 128
SUBLANES = 8
V7X_VMEM_BYTES = 64 * 1024 * 1024

TOKEN_TILE = 256
FFN_COL_TILE = 256
FFN_SUPER_TILES = 10
FFN_STEP_TILES = 4
DMA_ISSUE_UNROLL = 8


def _params(sem, vmem_mib, **kw):
    return pltpu.CompilerParams(dimension_semantics=sem, vmem_limit_bytes=vmem_mib * 1024 * 1024, **kw)


def _pick(total, candidates):
    for c in candidates:
        if total % c == 0:
            return c
    raise ValueError(f"no tile in {candidates} divides {total}")


def _mm_body(x_ref, w_ref, o_ref, wb_ref):
    @pl.when(pl.program_id(1) == 0)
    def _cast():
        wb_ref[...] = w_ref[...].astype(BF16)

    o_ref[...] = jnp.dot(x_ref[...], wb_ref[...], preferred_element_type=F32).astype(o_ref.dtype)


def _matmul(xb, w_stack, layer, out_dtype=F32):
    m, k = xb.shape
    n = w_stack.shape[2]
    tm = _pick(m, (768, 512, 256))
    tn = _pick(n, (1024, 512, 256, 128))
    return pl.pallas_call(
        _mm_body,
        grid=(n // tn, m // tm),
        in_specs=[pl.BlockSpec((tm, k), lambda j, i: (i, 0)),
                  pl.BlockSpec((None, k, tn), lambda j, i: (layer, 0, j))],
        out_specs=pl.BlockSpec((tm, tn), lambda j, i: (i, j)),
        out_shape=jax.ShapeDtypeStruct((m, n), out_dtype),
        scratch_shapes=[pltpu.VMEM((k, tn), BF16)],
        compiler_params=_params(("arbitrary", "arbitrary"), 48),
        name="matmul",
    )(xb, w_stack)


def _layer_norm(z, g, b):
    mu = jnp.mean(z, axis=-1, keepdims=True)
    zc = z - mu
    var = jnp.mean(zc * zc, axis=-1, keepdims=True)
    return zc * lax.rsqrt(var + LN_EPS) * g + b


def _route_top2(x, w, n_exp):
    xh, wh = x.astype(BF16), w.astype(BF16)
    xl, wl = (x - xh.astype(F32)).astype(BF16), (w - wh.astype(F32)).astype(BF16)
    dot = functools.partial(jnp.dot, preferred_element_type=F32)
    logits = dot(xh, wh) + (dot(xh, wl) + dot(xl, wh))
    lane = lax.broadcasted_iota(I32, logits.shape, 1)
    lg = jnp.where(lane < n_exp, logits, -jnp.inf)
    m1 = jnp.max(lg, axis=1, keepdims=True)
    i1 = jnp.min(jnp.where(lg == m1, lane, LANES), axis=1, keepdims=True)
    lg = jnp.where(lane == i1, -jnp.inf, lg)
    m2 = jnp.max(lg, axis=1, keepdims=True)
    i2 = jnp.min(jnp.where(lg == m2, lane, LANES), axis=1, keepdims=True)
    e = jnp.exp(m2 - m1)
    g1 = 1.0 / (1.0 + e)
    g2 = e / (1.0 + e)
    return jnp.where(lane == 0, i1.astype(F32),
                     jnp.where(lane == 1, i2.astype(F32),
                               jnp.where(lane == 2, g1, jnp.where(lane == 3, g2, 0.0))))


def _mm_ln_body(y_ref, w_ref, x_ref, g_ref, b_ref, *rest, alpha, n_exp):
    if n_exp is None:
        o_ref, ob_ref, wb_ref = rest
    else:
        wr_ref, o_ref, ob_ref, route_ref, wb_ref = rest

    @pl.when(pl.program_id(0) == 0)
    def _cast():
        wb_ref[...] = w_ref[...].astype(BF16)

    mix = jnp.dot(y_ref[...], wb_ref[...], preferred_element_type=F32)
    out = _layer_norm(alpha * x_ref[...] + mix, g_ref[...], b_ref[...])
    o_ref[...] = out
    ob_ref[...] = out.astype(BF16)
    if n_exp is not None:
        route_ref[...] = _route_top2(out, wr_ref[...], n_exp)


def _matmul_add_ln(yb, w_stack, w_layer, x, g_stack, b_stack, layer, alpha, w_router=None, router_layer=None):
    m, k = yb.shape
    d = x.shape[1]
    assert w_stack.shape[2] == d
    tm = TOKEN_TILE
    row = lambda width: pl.BlockSpec((tm, width), lambda i: (i, 0))
    par = pl.BlockSpec((None, 1, d), lambda i: (layer, 0, 0))
    in_specs = [row(k),
                pl.BlockSpec((None, k, d), lambda i: (w_layer, 0, 0), pipeline_mode=pl.Buffered(1)),
                row(d), par, par]
    out_specs = [row(d), row(d)]
    out_shape = [jax.ShapeDtypeStruct((m, d), F32), jax.ShapeDtypeStruct((m, d), BF16)]
    args = [yb, w_stack, x, g_stack.reshape(-1, 1, d), b_stack.reshape(-1, 1, d)]
    n_exp = None
    if w_router is not None:
        n_exp = w_router.shape[2]
        in_specs.append(pl.BlockSpec((None, d, LANES), lambda i: (router_layer, 0, 0)))
        out_specs.append(row(LANES))
        out_shape.append(jax.ShapeDtypeStruct((m, LANES), F32))
        args.append(jnp.pad(w_router, ((0, 0), (0, 0), (0, LANES - n_exp))))
    return pl.pallas_call(
        functools.partial(_mm_ln_body, alpha=alpha, n_exp=n_exp),
        grid=(m // tm,),
        in_specs=in_specs,
        out_specs=out_specs,
        out_shape=out_shape,
        scratch_shapes=[pltpu.VMEM((k, d), BF16)],
        compiler_params=_params(("arbitrary",), 48),
        name="matmul_add_ln",
    )(*args)


def _add_ln_body(x_ref, y_ref, g_ref, b_ref, o_ref, ob_ref, *, alpha):
    out = _layer_norm(alpha * x_ref[...] + y_ref[...], g_ref[...], b_ref[...])
    o_ref[...] = out
    ob_ref[...] = out.astype(BF16)


def _add_ln(x, y, g_stack, b_stack, layer, alpha):
    m, d = x.shape
    tm = TOKEN_TILE
    row = pl.BlockSpec((tm, d), lambda i: (i, 0))
    par = pl.BlockSpec((None, 1, d), lambda i: (layer, 0, 0))
    return pl.pallas_call(
        functools.partial(_add_ln_body, alpha=alpha),
        grid=(m // tm,),
        in_specs=[row, row, par, par],
        out_specs=[row, row],
        out_shape=[jax.ShapeDtypeStruct((m, d), F32), jax.ShapeDtypeStruct((m, d), BF16)],
        compiler_params=_params(("arbitrary",), 32),
        name="add_ln",
    )(x, y, g_stack.reshape(-1, 1, d), b_stack.reshape(-1, 1, d))


def _shift_rows(u, k, carry_rows):
    out = pltpu.roll(u, k, 0)
    row = lax.broadcasted_iota(I32, u.shape, 0)
    for r, c in enumerate(carry_rows):
        out = jnp.where(row == r, c, out)
    return out


def _sconv_body(b_ref, c_ref, h_ref, w_ref, s0_ref, s1_ref, y_ref, newp_ref, news_ref, carry_ref,
                *, tiles_per_seq, n_prompt_tiles, bs):
    i = pl.program_id(0)
    w0, w1, w2 = w_ref[0:1, :], w_ref[1:2, :], w_ref[2:3, :]
    tm = y_ref.shape[0]

    @pl.when(i < n_prompt_tiles)
    def _prompt():
        u = c_ref[...] * h_ref[...]
        fresh = i % tiles_per_seq == 0
        c0 = jnp.where(fresh, 0.0, carry_ref[0:1, :])
        c1 = jnp.where(fresh, 0.0, carry_ref[1:2, :])
        p1 = _shift_rows(u, 1, (c1,))
        p2 = _shift_rows(u, 2, (c0, c1))
        y = w0 * p2 + w1 * p1 + w2 * u
        y_ref[...] = (b_ref[...] * y).astype(y_ref.dtype)
        carry_ref[0:2, :] = u[tm - 2:tm, :]
        newp_ref[...] = u[tm - 2:tm, :]

    @pl.when(i == n_prompt_tiles)
    def _sample():
        u = c_ref[0:bs, :] * h_ref[0:bs, :]
        y = w0 * s0_ref[...] + w1 * s1_ref[...] + w2 * u
        y_ref[...] = jnp.zeros(y_ref.shape, y_ref.dtype)
        y_ref[0:bs, :] = (b_ref[0:bs, :] * y).astype(y_ref.dtype)
        news_ref[...] = u


def _short_conv(proj, w_conv, layer, s0, s1, *, bp, seq, bs):
    m_tot = proj.shape[0]
    d = proj.shape[1] // 3
    tm = TOKEN_TILE
    tps = seq // tm
    npt = bp * tps
    col = lambda c: pl.BlockSpec((tm, d), lambda i: (i, c))
    full = lambda shape: pl.BlockSpec(shape, lambda i: (0,) * len(shape))
    return pl.pallas_call(
        functools.partial(_sconv_body, tiles_per_seq=tps, n_prompt_tiles=npt, bs=bs),
        grid=(m_tot // tm,),
        in_specs=[col(0), col(1), col(2),
                  pl.BlockSpec((None, w_conv.shape[1], d), lambda i: (layer, 0, 0)),
                  full((bs, d)), full((bs, d))],
        out_specs=[pl.BlockSpec((tm, d), lambda i: (i, 0)),
                   pl.BlockSpec((None, 2, d), lambda i: (jnp.minimum(i // tps, bp - 1), 0, 0)),
                   full((bs, d))],
        out_shape=[jax.ShapeDtypeStruct((m_tot, d), BF16),
                   jax.ShapeDtypeStruct((bp, 2, d), F32),
                   jax.ShapeDtypeStruct((bs, d), F32)],
        scratch_shapes=[pltpu.VMEM((SUBLANES, d), F32)],
        compiler_params=_params(("arbitrary",), 40),
        name="short_conv",
    )(proj, proj, proj, w_conv, s0, s1)


def _rope_head(x, cos_f, sin_f, half):
    lane = lax.broadcasted_iota(I32, x.shape, 1)
    partner = jnp.where(lane < half, pltpu.roll(x, LANES - half, 1), pltpu.roll(x, half, 1))
    return x * cos_f + partner * sin_f


def _rope_body(q_ref, k_ref, v_ref, cos_ref, sin_ref, cos_s_ref, sin_s_ref,
               qo_ref, kp_ref, vp_ref, ks_ref, vs_ref, *, n_heads, hd, half, n_prompt_tiles, bs):
    i = pl.program_id(0)

    def rotate(x_ref, rows, cos_f, sin_f):
        return [
            _rope_head(x_ref[rows, h * hd:(h + 1) * hd], cos_f, sin_f, half) for h in range(n_heads)
        ]

    @pl.when(i < n_prompt_tiles)
    def _prompt():
        rows = slice(None)
        cos_f, sin_f = cos_ref[...], sin_ref[...]
        for h, (qh, kh) in enumerate(zip(rotate(q_ref, rows, cos_f, sin_f), rotate(k_ref, rows, cos_f, sin_f))):
            qo_ref[:, h * hd:(h + 1) * hd] = qh
            kp_ref[:, h * hd:(h + 1) * hd] = kh
        vp_ref[...] = v_ref[...]

    @pl.when(i == n_prompt_tiles)
    def _sample():
        rows = slice(0, bs)
        cos_f, sin_f = cos_s_ref[...], sin_s_ref[...]
        qo_ref[...] = jnp.zeros(qo_ref.shape, qo_ref.dtype)
        for h, (qh, kh) in enumerate(zip(rotate(q_ref, rows, cos_f, sin_f), rotate(k_ref, rows, cos_f, sin_f))):
            qo_ref[0:bs, h * hd:(h + 1) * hd] = qh
            ks_ref[:, h * hd:(h + 1) * hd] = kh
        vs_ref[...] = v_ref[0:bs, :]


def _rope_tables(pos, hd):
    rot = hd // ROT_FRACTION
    half = rot // 2
    inv_freq = ROPE_THETA ** (-jnp.arange(half, dtype=F32) * 2.0 / rot)
    ang = pos.astype(F32)[:, None] * inv_freq[None, :]
    cos, sin = jnp.cos(ang), jnp.sin(ang)
    n = pos.shape[0]
    cos_f = jnp.concatenate([cos, cos, jnp.ones((n, hd - rot), F32)], axis=1)
    sin_f = jnp.concatenate([-sin, sin, jnp.zeros((n, hd - rot), F32)], axis=1)
    return cos_f, sin_f


def _rope(qkv, *, n_heads, hd, bp, seq, bs, past_len):
    m_tot = qkv.shape[0]
    hdim = n_heads * hd
    tm = TOKEN_TILE
    tps = seq // tm
    npt = bp * tps
    mp = bp * seq
    cos_p, sin_p = _rope_tables(jnp.arange(seq, dtype=I32), hd)
    cos_s, sin_s = _rope_tables(jnp.full((1,), past_len, I32), hd)
    col = lambda c: pl.BlockSpec((tm, hdim), lambda i: (i, c))
    tab = pl.BlockSpec((tm, hd), lambda i: (i % tps, 0))
    one = pl.BlockSpec((1, hd), lambda i: (0, 0))
    prow = pl.BlockSpec((tm, hdim), lambda i: (jnp.minimum(i, npt - 1), 0))
    srow = pl.BlockSpec((bs, hdim), lambda i: (0, 0))
    return pl.pallas_call(
        functools.partial(_rope_body, n_heads=n_heads, hd=hd, half=hd // ROT_FRACTION // 2,
                          n_prompt_tiles=npt, bs=bs),
        grid=(m_tot // tm,),
        in_specs=[col(0), col(1), col(2), tab, tab, one, one],
        out_specs=[pl.BlockSpec((tm, hdim), lambda i: (i, 0)), prow, prow, srow, srow],
        out_shape=[jax.ShapeDtypeStruct((m_tot, hdim), F32),
                   jax.ShapeDtypeStruct((mp, hdim), F32), jax.ShapeDtypeStruct((mp, hdim), F32),
                   jax.ShapeDtypeStruct((bs, hdim), F32), jax.ShapeDtypeStruct((bs, hdim), F32)],
        compiler_params=_params(("arbitrary",), 48),
        name="rope",
    )(qkv, qkv, qkv, cos_p, sin_p, cos_s, sin_s)


def _nt_dot(a, b, **kw):
    return lax.dot_general(a, b, (((1,), (1,)), ((), ())), preferred_element_type=F32, **kw)


def _attn_body(q_ref, k_ref, v_ref, o_ref, kmean_ref, kb_ref, vt_ref, *, nb, scale):
    blk = MOBA_BLOCK
    kmean_ref[...] = jnp.zeros(kmean_ref.shape, F32)
    for n in range(nb):
        rows = slice(n * blk, (n + 1) * blk)
        kmean_ref[n:n + 1, :] = jnp.sum(k_ref[rows, :], axis=0, keepdims=True) * (1.0 / blk)
        vt_ref[:, rows] = v_ref[rows, :].T.astype(BF16)
    kb_ref[...] = k_ref[...].astype(BF16)
    gate_all = _nt_dot(kmean_ref[...], q_ref[...], precision=lax.Precision.HIGHEST)
    bid = lax.broadcasted_iota(I32, (kmean_ref.shape[0], blk), 0)
    ki = lax.broadcasted_iota(I32, (blk, blk), 0)
    qi = lax.broadcasted_iota(I32, (blk, blk), 1)

    for j in range(nb):
        cols = slice(j * blk, (j + 1) * blk)
        gate = jnp.where(bid < j, gate_all[:, cols], NEG_INF)
        rank = jnp.zeros(gate.shape, I32)
        for m in range(j):
            gm = gate[m:m + 1, :]
            beats = (gm > gate) | ((gm == gate) & (m < bid))
            rank = rank + jnp.where(beats, 1, 0)
        sel = jnp.where((bid < j) & (rank < MOBA_TOPK), 1.0, 0.0)
        keys = (j + 1) * blk
        s = _nt_dot(kb_ref[0:keys, :], q_ref[cols, :].astype(BF16)) * (scale * LOG2_E)
        slabs = [jnp.where(sel[n:n + 1, :] > 0.0, s[n * blk:(n + 1) * blk, :], NEG_INF) for n in range(j)]
        slabs.append(jnp.where(ki <= qi, s[j * blk:keys, :], NEG_INF))
        top = slabs[0]
        for sl in slabs[1:]:
            top = jnp.maximum(top, sl)
        top = jnp.max(top, axis=0, keepdims=True)
        p = [jnp.exp2(sl - top) for sl in slabs]
        tot = p[0]
        for pn in p[1:]:
            tot = tot + pn
        denom = jnp.sum(tot, axis=0, keepdims=True)
        pv = jnp.dot(vt_ref[:, 0:keys], jnp.concatenate(p, axis=0).astype(BF16),
                     preferred_element_type=F32)
        o_ref[cols, :] = (pv / denom).T.astype(o_ref.dtype)


def _attn_prompt(q, k_p, v_p, *, bp, seq, n_heads, hd):
    nb = seq // MOBA_BLOCK
    nbp = -(-nb // SUBLANES) * SUBLANES
    blk = pl.BlockSpec((seq, hd), lambda b, h: (b, h))
    return pl.pallas_call(
        functools.partial(_attn_body, nb=nb, scale=hd ** -0.5),
        grid=(bp, n_heads),
        in_specs=[blk, blk, blk],
        out_specs=blk,
        out_shape=jax.ShapeDtypeStruct((bp * seq, n_heads * hd), BF16),
        scratch_shapes=[pltpu.VMEM((nbp, hd), F32), pltpu.VMEM((seq, hd), BF16), pltpu.VMEM((hd, seq), BF16)],
        compiler_params=_params(("arbitrary", "arbitrary"), 48),
        name="moba_prompt",
    )(q, k_p, v_p)


def _page_mean_body(pt_ref, *refs, pages_per_step, pages_per_block):
    del pt_ref
    pages, o_ref = refs[:pages_per_step], refs[pages_per_step]
    inv = 1.0 / MOBA_BLOCK
    for c in range(pages_per_step // pages_per_block):
        tot = jnp.sum(pages[c * pages_per_block][...], axis=0)
        for r in range(1, pages_per_block):
            tot = tot + jnp.sum(pages[c * pages_per_block + r][...], axis=0)
        o_ref[c] = tot * inv


def _page_means(cache_k, layer, page_table):
    _, _, page, n_heads, hd = cache_k.shape
    bs, n_pages = page_table.shape
    ppb = MOBA_BLOCK // page
    pps = _pick(n_pages, (16, 8, 4, 2)) if ppb == 2 else ppb
    n_blocks = n_pages // ppb

    def page_spec(c):
        return pl.BlockSpec((None, None, page, n_heads, hd),
                            lambda b, s, pt: (layer, pt[b * n_pages + s * pps + c], 0, 0, 0))

    return pl.pallas_call(
        functools.partial(_page_mean_body, pages_per_step=pps, pages_per_block=ppb),
        grid_spec=pltpu.PrefetchScalarGridSpec(
            num_scalar_prefetch=1,
            grid=(bs, n_pages // pps),
            in_specs=[page_spec(c) for c in range(pps)],
            out_specs=pl.BlockSpec((None, pps // ppb, n_heads, hd), lambda b, s, pt: (b, s, 0, 0)),
        ),
        out_shape=jax.ShapeDtypeStruct((bs, n_blocks, n_heads, hd), F32),
        compiler_params=_params(("arbitrary", "arbitrary"), 48),
        name="page_means",
    )(page_table.reshape(-1), *([cache_k] * pps))


def _select_body(km_ref, q_ref, o_ref, *, n_heads):
    n_blocks = km_ref.shape[1]
    res = jnp.zeros(o_ref.shape, I32)
    row = lax.broadcasted_iota(I32, o_ref.shape, 0)
    lane = lax.broadcasted_iota(I32, o_ref.shape, 1)
    blk_id = lax.broadcasted_iota(I32, (n_blocks, 1), 0)
    for h in range(n_heads):
        gate = jnp.sum(km_ref[h] * q_ref[h:h + 1, :], axis=1, keepdims=True)
        for k in range(MOBA_TOPK):
            best = jnp.max(gate, axis=0, keepdims=True)
            idx = jnp.min(jnp.where(gate == best, blk_id, n_blocks), axis=0, keepdims=True)
            res = jnp.where((row == k) & (lane == h), idx, res)
            gate = jnp.where(blk_id == idx, -jnp.inf, gate)
    o_ref[...] = res


def _select_blocks(kmean_t, q_s):
    bs, n_heads, n_blocks, hd = kmean_t.shape
    return pl.pallas_call(
        functools.partial(_select_body, n_heads=n_heads),
        grid=(bs,),
        in_specs=[pl.BlockSpec((None, n_heads, n_blocks, hd), lambda b: (b, 0, 0, 0)),
                  pl.BlockSpec((None, n_heads, hd), lambda b: (b, 0, 0))],
        out_specs=pl.BlockSpec((None, SUBLANES, LANES), lambda b: (b, 0, 0)),
        out_shape=jax.ShapeDtypeStruct((bs, SUBLANES, LANES), I32),
        compiler_params=_params(("arbitrary",), 32),
        name="moba_select",
    )(kmean_t, q_s.reshape(bs, n_heads, hd))


def _decode_attn_body(ph_ref, q_ref, kn_ref, vn_ref, ck_hbm, cv_hbm, o_ref, kbuf, vbuf, sem,
                      *, layer, n_heads, hd, n_sel_pages, scale):
    b = pl.program_id(0)

    def start_copies(tok, slot):
        for h in range(n_heads):
            for r in range(n_sel_pages):
                pg = ph_ref[(tok * n_heads + h) * n_sel_pages + r]
                pltpu.make_async_copy(ck_hbm.at[layer, pg, :, h, :], kbuf.at[slot, h, r], sem.at[slot]).start()
                pltpu.make_async_copy(cv_hbm.at[layer, pg, :, h, :], vbuf.at[slot, h, r], sem.at[slot]).start()

    @pl.when(b == 0)
    def _first():
        start_copies(b, 0)

    @pl.when(b + 1 < pl.num_programs(0))
    def _prefetch():
        start_copies(b + 1, (b + 1) % 2)

    slot = b % 2
    pltpu.make_async_copy(kbuf.at[1 - slot], kbuf.at[slot], sem.at[slot]).wait()
    pltpu.make_async_copy(vbuf.at[1 - slot], vbuf.at[slot], sem.at[slot]).wait()

    row = pl.ds(b, 1)
    q_row, kn_row, vn_row = q_ref[row, :], kn_ref[row, :], vn_ref[row, :]
    out = []
    for h in range(n_heads):
        cols = slice(h * hd, (h + 1) * hd)
        q = q_row[:, cols]
        s_new = jnp.sum(q * kn_row[:, cols], axis=1, keepdims=True) * scale
        scores = [jnp.sum(kbuf[slot, h, r] * q, axis=1, keepdims=True) * scale for r in range(n_sel_pages)]
        top = s_new
        for s in scores:
            top = jnp.maximum(top, jnp.max(s, axis=0, keepdims=True))
        p_new = jnp.exp(s_new - top)
        denom = p_new
        acc = p_new * vn_row[:, cols]
        for r, s in enumerate(scores):
            p = jnp.exp(s - top)
            denom = denom + jnp.sum(p, axis=0, keepdims=True)
            acc = acc + jnp.sum(p * vbuf[slot, h, r], axis=0, keepdims=True)
        out.append(acc / denom)
    o_ref[row, :] = jnp.concatenate(out, axis=1)


def _decode_attn(q_s, k_s, v_s, cache_k, cache_v, layer, phys_pages, *, n_heads, hd):
    bs = q_s.shape[0]
    page = cache_k.shape[2]
    nsp = phys_pages.shape[-1]
    rows = pl.BlockSpec((bs, n_heads * hd), lambda b, ph: (0, 0))
    anywhere = pl.BlockSpec(memory_space=pl.ANY)
    return pl.pallas_call(
        functools.partial(_decode_attn_body, layer=layer, n_heads=n_heads, hd=hd, n_sel_pages=nsp,
                          scale=hd ** -0.5),
        grid_spec=pltpu.PrefetchScalarGridSpec(
            num_scalar_prefetch=1,
            grid=(bs,),
            in_specs=[rows, rows, rows, anywhere, anywhere],
            out_specs=rows,
            scratch_shapes=[pltpu.VMEM((2, n_heads, nsp, page, hd), F32),
                            pltpu.VMEM((2, n_heads, nsp, page, hd), F32),
                            pltpu.SemaphoreType.DMA((2,))],
        ),
        out_shape=jax.ShapeDtypeStruct((bs, n_heads * hd), F32),
        compiler_params=_params(("arbitrary",), 40),
        name="moba_decode",
    )(phys_pages.reshape(-1), q_s, k_s, v_s, cache_k, cache_v)


def _gelu_tanh(x):
    return 0.5 * x * (1.0 + jnp.tanh(np.sqrt(2.0 / np.pi).astype(np.float32) * (x + 0.044715 * (x * x * x))))


def _softplus(x):
    return jnp.maximum(x, 0.0) + jnp.log1p(jnp.exp(-jnp.abs(x)))


def _lru_gates(uc, wa_ref, wx_ref, ba, bx, lam, n_heads, rb):
    ucb = uc.astype(BF16)
    a_parts, b_parts = [], []
    sp = _softplus(-lam)
    for h in range(n_heads):
        cols = slice(h * rb, (h + 1) * rb)
        r = jax.nn.sigmoid(jnp.dot(ucb[:, cols], wa_ref[h].astype(BF16), preferred_element_type=F32) + ba[:, cols])
        g = jax.nn.sigmoid(jnp.dot(ucb[:, cols], wx_ref[h].astype(BF16), preferred_element_type=F32) + bx[:, cols])
        a = jnp.exp(-LRU_C * r * sp[:, cols])
        a_parts.append(a)
        b_parts.append(jnp.sqrt(1.0 - a * a) * (g * uc[:, cols]))
    return jnp.concatenate(a_parts, axis=1), jnp.concatenate(b_parts, axis=1)


def _lru_body(g_ref, u_ref, wc_ref, bc_ref, wa_ref, wx_ref, ba_ref, bx_ref, lam_ref,
              c0_ref, c1_ref, c2_ref, h0_ref,
              y_ref, newc_ref, newh_ref, us_ref, hs_ref,
              carry_ref, hcarry_ref, a_buf, b_buf, *, tiles_per_seq, n_prompt_tiles, bs, n_heads, rb):
    i = pl.program_id(0)
    tm = y_ref.shape[0]
    w0, w1, w2, w3 = wc_ref[0:1, :], wc_ref[1:2, :], wc_ref[2:3, :], wc_ref[3:4, :]
    bias = bc_ref[...]
    gates = functools.partial(_lru_gates, wa_ref=wa_ref, wx_ref=wx_ref, ba=ba_ref[...], bx=bx_ref[...],
                              lam=lam_ref[...], n_heads=n_heads, rb=rb)

    @pl.when(i < n_prompt_tiles)
    def _prompt():
        u = u_ref[...]
        fresh = i % tiles_per_seq == 0
        c0, c1, c2 = (jnp.where(fresh, 0.0, carry_ref[r:r + 1, :]) for r in range(3))
        p1 = _shift_rows(u, 1, (c2,))
        p2 = _shift_rows(u, 2, (c1, c2))
        p3 = _shift_rows(u, 3, (c0, c1, c2))
        uc = w0 * p3 + w1 * p2 + w2 * p1 + w3 * u + bias
        a, b = gates(uc)
        a_buf[...] = a
        b_buf[...] = b
        sub = lax.broadcasted_iota(I32, (SUBLANES, a.shape[1]), 0)

        def group(gi, h):
            rows = pl.ds(pl.multiple_of(gi * SUBLANES, SUBLANES), SUBLANES)
            ag, bg = a_buf[rows, :], b_buf[rows, :]
            for d in (1, 2, 4):
                bg = bg + ag * jnp.where(sub >= d, pltpu.roll(bg, d, 0), 0.0)
                ag = ag * jnp.where(sub >= d, pltpu.roll(ag, d, 0), 1.0)
            hg = ag * h + bg
            b_buf[rows, :] = hg
            return hg[SUBLANES - 1:SUBLANES, :]

        h_last = lax.fori_loop(0, tm // SUBLANES, group, jnp.where(fresh, 0.0, hcarry_ref[0:1, :]))
        hcarry_ref[0:1, :] = h_last
        y_ref[...] = (b_buf[...] * _gelu_tanh(g_ref[...])).astype(y_ref.dtype)
        carry_ref[0:3, :] = u[tm - 3:tm, :]
        newc_ref[...] = u[tm - 3:tm, :]
        newh_ref[...] = h_last

    @pl.when(i == n_prompt_tiles)
    def _sample():
        u = u_ref[0:bs, :]
        uc = w0 * c0_ref[...] + w1 * c1_ref[...] + w2 * c2_ref[...] + w3 * u + bias
        a, b = gates(uc)
        h = a * h0_ref[...] + b
        y_ref[...] = jnp.zeros(y_ref.shape, y_ref.dtype)
        y_ref[0:bs, :] = (h * _gelu_tanh(g_ref[0:bs, :])).astype(y_ref.dtype)
        us_ref[...] = u
        hs_ref[...] = h


def _rglru(proj, w_conv, b_conv, w_gate_a, b_gate_a, w_gate_x, b_gate_x, lam, layer, conv_state, h0,
           *, bp, seq, bs):
    m_tot = proj.shape[0]
    dr = proj.shape[1] // 2
    n_heads, rb = w_gate_a.shape[1], w_gate_a.shape[2]
    tm = TOKEN_TILE
    tps = seq // tm
    npt = bp * tps
    width = w_conv.shape[1]
    col = lambda c: pl.BlockSpec((tm, dr), lambda i: (i, c))
    vec = pl.BlockSpec((None, 1, dr), lambda i: (layer, 0, 0))
    gw = pl.BlockSpec((None, n_heads, rb, rb), lambda i: (layer, 0, 0, 0))
    st = pl.BlockSpec((bs, dr), lambda i: (0, 0))
    pb = lambda rows: pl.BlockSpec((None, rows, dr), lambda i: (jnp.minimum(i // tps, bp - 1), 0, 0))
    return pl.pallas_call(
        functools.partial(_lru_body, tiles_per_seq=tps, n_prompt_tiles=npt, bs=bs, n_heads=n_heads, rb=rb),
        grid=(m_tot // tm,),
        in_specs=[col(0), col(1), pl.BlockSpec((None, width, dr), lambda i: (layer, 0, 0)), vec,
                  gw, gw, vec, vec, vec, st, st, st, st],
        out_specs=[pl.BlockSpec((tm, dr), lambda i: (i, 0)), pb(width - 1), pb(1), st, st],
        out_shape=[jax.ShapeDtypeStruct((m_tot, dr), BF16),
                   jax.ShapeDtypeStruct((bp, width - 1, dr), F32),
                   jax.ShapeDtypeStruct((bp, 1, dr), F32),
                   jax.ShapeDtypeStruct((bs, dr), F32),
                   jax.ShapeDtypeStruct((bs, dr), F32)],
        scratch_shapes=[pltpu.VMEM((SUBLANES, dr), F32), pltpu.VMEM((SUBLANES, dr), F32),
                        pltpu.VMEM((tm, dr), F32), pltpu.VMEM((tm, dr), F32)],
        compiler_params=_params(("arbitrary",), 48),
        name="rglru",
    )(proj, proj, w_conv, b_conv.reshape(-1, 1, dr), w_gate_a, w_gate_x,
      b_gate_a.reshape(-1, 1, dr), b_gate_x.reshape(-1, 1, dr), lam.reshape(-1, 1, dr),
      conv_state[:, 0], conv_state[:, 1], conv_state[:, 2], h0)


def _ffn_body(e_ref, start_ref, nt_ref, nused_ref, xs_hbm, wg_ref, wu_ref, wd_ref, o_hbm,
              xbuf, acc, wgb, wub, wdb, sem_in, sem_out, *, tm, nf):
    del e_ref
    s, f = pl.program_id(0), pl.program_id(1)
    nt = nt_ref[s]
    start = start_ref[s]

    def copy_in(r):
        return pltpu.make_async_copy(xs_hbm.at[pl.ds(pl.multiple_of(start + r * tm, tm), tm), :],
                                     xbuf.at[pl.ds(pl.multiple_of(r * tm, tm), tm), :], sem_in)

    def copy_out(r):
        return pltpu.make_async_copy(acc.at[pl.ds(pl.multiple_of(r * tm, tm), tm), :],
                                     o_hbm.at[pl.ds(pl.multiple_of(start + r * tm, tm), tm), :], sem_out)

    def for_tiles(fn):
        def body(r, c):
            fn(r)
            return c
        lax.fori_loop(0, nt, body, 0)

    @pl.when(nt > 0)
    def _run():
        @pl.when(f == 0)
        def _load():
            for_tiles(lambda r: copy_in(r).start())

            def clear(r):
                acc[pl.ds(pl.multiple_of(r * tm, tm), tm), :] = jnp.zeros((tm, acc.shape[1]), F32)

            for_tiles(clear)
            for_tiles(lambda r: copy_in(r).wait())

        def rows_step(first_tile, n_tiles, cast_weights=False):
            rows = pl.ds(pl.multiple_of(first_tile * tm, tm), n_tiles * tm)
            x = xbuf[rows, :]
            if cast_weights:
                wgb[...] = wg_ref[...].astype(BF16)
            g = jnp.dot(x, wgb[...], preferred_element_type=F32)
            if cast_weights:
                wub[...] = wu_ref[...].astype(BF16)
            u = jnp.dot(x, wub[...], preferred_element_type=F32)
            h = (g * jax.nn.sigmoid(g) * u).astype(BF16)
            if cast_weights:
                wdb[...] = wd_ref[...].astype(BF16)
            acc[rows, :] += jnp.dot(h, wdb[...], preferred_element_type=F32)

            @pl.when(f == nf - 1)
            def _store():
                for r in range(n_tiles):
                    copy_out(first_tile + r).start()

        big = FFN_STEP_TILES
        n_big = nt // big
        rem = nt % big

        @pl.when(n_big > 0)
        def _first_big():
            rows_step(0, big, cast_weights=True)

        def big_step(i, c):
            rows_step(big * i, big)
            return c

        lax.fori_loop(1, n_big, big_step, 0)

        size = big // 2
        while size >= 1:
            present = (rem // size) % 2 == 1
            first = (n_big == 0) & (rem < 2 * size)
            offset = n_big * big + rem - rem % (2 * size)

            @pl.when(present & first)
            def _piece_first(size=size):
                rows_step(0, size, cast_weights=True)

            @pl.when(present & jnp.logical_not(first))
            def _piece(size=size, offset=offset):
                rows_step(offset, size)

            size //= 2

        @pl.when(f == nf - 1)
        def _drain_stores():
            for_tiles(lambda r: copy_out(r).wait())

    @pl.when((s == pl.num_programs(0) - 1) & (f == nf - 1))
    def _zero_unused():
        acc[0:tm, :] = jnp.zeros((tm, acc.shape[1]), F32)

        def zero_tile(t, c):
            cp = pltpu.make_async_copy(acc.at[0:tm, :], o_hbm.at[pl.ds(pl.multiple_of(t * tm, tm), tm), :], sem_out)
            cp.start()
            cp.wait()
            return c

        lax.fori_loop(nused_ref[0], o_hbm.shape[0] // tm, zero_tile, 0)


def _ffn(xs, w_gate, w_up, w_down, layer, sched_e, sched_start, sched_nt, n_used_tiles):
    s_rows, d = xs.shape
    dff = w_gate.shape[3]
    tm, tf, rt = TOKEN_TILE, FFN_COL_TILE, FFN_SUPER_TILES
    nf = dff // tf
    n_super = sched_e.shape[0]

    def fcol(s, f, nt_ref):
        return jnp.where(nt_ref[s] > 0, f, nf - 1)

    up_spec = pl.BlockSpec((None, None, d, tf), lambda s, f, e, st, nt, nu: (layer, e[s], 0, fcol(s, f, nt)))
    down_spec = pl.BlockSpec((None, None, tf, d), lambda s, f, e, st, nt, nu: (layer, e[s], fcol(s, f, nt), 0))
    return pl.pallas_call(
        functools.partial(_ffn_body, tm=tm, nf=nf),
        grid_spec=pltpu.PrefetchScalarGridSpec(
            num_scalar_prefetch=4,
            grid=(n_super, nf),
            in_specs=[pl.BlockSpec(memory_space=pl.ANY), up_spec, up_spec, down_spec],
            out_specs=pl.BlockSpec(memory_space=pl.ANY),
            scratch_shapes=[pltpu.VMEM((rt * tm, d), BF16), pltpu.VMEM((rt * tm, d), F32),
                            pltpu.VMEM((d, tf), BF16), pltpu.VMEM((d, tf), BF16), pltpu.VMEM((tf, d), BF16),
                            pltpu.SemaphoreType.DMA(()), pltpu.SemaphoreType.DMA(())],
        ),
        out_shape=jax.ShapeDtypeStruct((s_rows, d), F32),
        compiler_params=_params(("arbitrary", "arbitrary"), 56),
        name="swiglu_ffn",
    )(sched_e, sched_start, sched_nt, n_used_tiles, xs, w_gate, w_up, w_down)


def _dense_schedule(n_tiles):
    rt = FFN_SUPER_TILES
    n_super = -(-n_tiles // rt)
    start = np.arange(n_super, dtype=np.int32) * rt
    nt = np.minimum(rt, n_tiles - start).astype(np.int32)
    return (jnp.zeros((n_super,), I32), jnp.asarray(start * TOKEN_TILE, I32), jnp.asarray(nt, I32),
            jnp.full((1,), n_tiles, I32))


def _slot_token_body(p1_ref, p2_ref, o_ref, *, n_valid, n_slots):
    def clear(g, c):
        for r in range(DMA_ISSUE_UNROLL):
            o_ref[g * DMA_ISSUE_UNROLL + r] = 0
        return c

    def place(t):
        o_ref[p1_ref[t]] = t
        o_ref[p2_ref[t]] = t

    def place_group(g, c):
        for r in range(DMA_ISSUE_UNROLL):
            place(g * DMA_ISSUE_UNROLL + r)
        return c

    def place_one(t, c):
        place(t)
        return c

    lax.fori_loop(0, n_slots // DMA_ISSUE_UNROLL, clear, 0)
    n_groups = n_valid // DMA_ISSUE_UNROLL
    lax.fori_loop(0, n_groups, place_group, 0)
    lax.fori_loop(n_groups * DMA_ISSUE_UNROLL, n_valid, place_one, 0)


def _slot_tokens(pos1, pos2, n_valid, n_slots):
    smem = pl.BlockSpec(memory_space=pltpu.SMEM)
    return pl.pallas_call(
        functools.partial(_slot_token_body, n_valid=n_valid, n_slots=n_slots),
        in_specs=[smem, smem],
        out_specs=smem,
        out_shape=jax.ShapeDtypeStruct((n_slots,), I32),
        name="moe_slot_tokens",
    )(pos1, pos2)


def _moe_plan(route, n_valid, n_exp):
    m_tot = route.shape[0]
    tm, rt = TOKEN_TILE, FFN_SUPER_TILES
    n_tiles_max = (MOE_TOP_K * n_valid + n_exp * (tm - 1)) // tm
    s_rows = n_tiles_max * tm
    n_super = n_tiles_max // rt + n_exp
    e1, e2 = route[:, 0].astype(I32), route[:, 1].astype(I32)
    tok = jnp.arange(m_tot, dtype=I32)
    valid = tok < n_valid
    ids = jnp.arange(n_exp, dtype=I32)[None, :]
    oh1 = (e1[:, None] == ids) & valid[:, None]
    oh2 = (e2[:, None] == ids) & valid[:, None]
    oh = oh1.astype(I32) + oh2.astype(I32)
    csum = jnp.cumsum(oh, axis=0)
    before = csum - oh
    counts = csum[-1]
    tiles_e = (counts + tm - 1) // tm
    tile_end = jnp.cumsum(tiles_e)
    tile_off = tile_end - tiles_e
    row_off = tile_off * tm
    pos1 = jnp.sum(jnp.where(oh1, before + row_off[None, :], 0), axis=1)
    pos2 = jnp.sum(jnp.where(oh2, before + row_off[None, :], 0), axis=1)
    slot_token = _slot_tokens(pos1, pos2, n_valid, s_rows)
    n_used_tiles = tile_end[-1:]
    super_e = -(-tiles_e // rt)
    super_end = jnp.cumsum(super_e)
    super_off = super_end - super_e
    n_used_super = super_end[-1]
    sidx = jnp.arange(n_super, dtype=I32)
    live = sidx < n_used_super
    sclamp = jnp.minimum(sidx, n_used_super - 1)
    exp_of = jnp.sum((sclamp[:, None] >= super_end[None, :]).astype(I32), axis=1)
    k = sclamp - super_off[exp_of]
    sched_start = (tile_off[exp_of] + k * rt) * tm
    sched_nt = jnp.where(live, jnp.clip(tiles_e[exp_of] - k * rt, 0, rt), 0)
    return pos1, pos2, slot_token, n_used_tiles, (exp_of, sched_start, sched_nt), s_rows


def _gather_body(tok_ref, nused_ref, x_hbm, o_ref, buf, sem, *, tm):
    t = pl.program_id(0)
    n_used = nused_ref[0]

    def issue_tile(tile, slot):
        def issue(g, c):
            for r in range(DMA_ISSUE_UNROLL):
                row = g * DMA_ISSUE_UNROLL + r
                pltpu.make_async_copy(x_hbm.at[pl.ds(tok_ref[tile * tm + row], 1), :],
                                      buf.at[slot, pl.ds(row, 1), :], sem.at[slot]).start()
            return c

        lax.fori_loop(0, tm // DMA_ISSUE_UNROLL, issue, 0)

    @pl.when((t == 0) & (n_used > 0))
    def _first():
        issue_tile(0, 0)

    @pl.when(t + 1 < n_used)
    def _prefetch():
        issue_tile(t + 1, (t + 1) % 2)

    @pl.when(t < n_used)
    def _used():
        slot = t % 2
        pltpu.make_async_copy(x_hbm.at[pl.ds(0, tm), :], buf.at[slot], sem.at[slot]).wait()
        o_ref[...] = buf[slot].astype(o_ref.dtype)

    @pl.when(t >= n_used)
    def _unused():
        o_ref[...] = jnp.zeros(o_ref.shape, o_ref.dtype)


def _gather_rows(x, slot_token, n_used_tiles):
    s_rows = slot_token.shape[0]
    d = x.shape[1]
    tm = TOKEN_TILE
    return pl.pallas_call(
        functools.partial(_gather_body, tm=tm),
        grid_spec=pltpu.PrefetchScalarGridSpec(
            num_scalar_prefetch=2,
            grid=(s_rows // tm,),
            in_specs=[pl.BlockSpec(memory_space=pl.ANY)],
            out_specs=pl.BlockSpec((tm, d), lambda t, tok, nu: (t, 0)),
            scratch_shapes=[pltpu.VMEM((2, tm, d), F32), pltpu.SemaphoreType.DMA((2,))],
        ),
        out_shape=jax.ShapeDtypeStruct((s_rows, d), BF16),
        compiler_params=_params(("arbitrary",), 32),
        name="moe_gather",
    )(slot_token, n_used_tiles, x)


def _combine_body(p1_ref, p2_ref, x_ref, route_ref, g_ref, b_ref, y_hbm, o_ref, ob_ref, buf, sem,
                  *, tm, alpha, n_prompt_tiles):
    t = pl.program_id(0)

    def issue_tile(tile, slot):
        def issue(g, c):
            for r in range(DMA_ISSUE_UNROLL):
                row = g * DMA_ISSUE_UNROLL + r
                for which, p_ref in enumerate((p1_ref, p2_ref)):
                    pltpu.make_async_copy(y_hbm.at[pl.ds(p_ref[tile * tm + row], 1), :],
                                          buf.at[slot, which, pl.ds(row, 1), :], sem.at[slot]).start()
            return c

        lax.fori_loop(0, tm // DMA_ISSUE_UNROLL, issue, 0)

    @pl.when(t == 0)
    def _first():
        issue_tile(0, 0)

    @pl.when(t + 1 < pl.num_programs(0))
    def _prefetch():
        issue_tile(t + 1, (t + 1) % 2)

    slot = t % 2
    for which in range(2):
        pltpu.make_async_copy(y_hbm.at[pl.ds(0, tm), :], buf.at[slot, which], sem.at[slot]).wait()
    route = route_ref[...]
    y = route[:, 2:3] * buf[slot, 0] + route[:, 3:4] * buf[slot, 1]
    out = _layer_norm(alpha * x_ref[...] + y, g_ref[...], b_ref[...])
    if n_prompt_tiles is None:
        o_ref[...] = out
        ob_ref[...] = out.astype(BF16)
    else:
        @pl.when(t < n_prompt_tiles)
        def _prompt():
            o_ref[...] = out

        @pl.when(t == n_prompt_tiles)
        def _sample():
            ob_ref[...] = out


def _combine_ln(x, route, y_slots, pos1, pos2, g_stack, b_stack, layer, alpha, n_prompt_rows=None):
    m, d = x.shape
    tm = TOKEN_TILE
    row = pl.BlockSpec((tm, d), lambda t, p1, p2: (t, 0))
    par = pl.BlockSpec((None, 1, d), lambda t, p1, p2: (layer, 0, 0))
    if n_prompt_rows is None:
        npt = None
        out_specs = [row, row]
        out_shape = [jax.ShapeDtypeStruct((m, d), F32), jax.ShapeDtypeStruct((m, d), BF16)]
    else:
        npt = n_prompt_rows // tm
        assert n_prompt_rows % tm == 0 and m == n_prompt_rows + tm
        out_specs = [pl.BlockSpec((tm, d), lambda t, p1, p2: (jnp.minimum(t, npt - 1), 0)),
                     pl.BlockSpec((tm, d), lambda t, p1, p2: (0, 0))]
        out_shape = [jax.ShapeDtypeStruct((n_prompt_rows, d), F32), jax.ShapeDtypeStruct((tm, d), F32)]
    return pl.pallas_call(
        functools.partial(_combine_body, tm=tm, alpha=alpha, n_prompt_tiles=npt),
        grid_spec=pltpu.PrefetchScalarGridSpec(
            num_scalar_prefetch=2,
            grid=(m // tm,),
            in_specs=[row, pl.BlockSpec((tm, LANES), lambda t, p1, p2: (t, 0)), par, par,
                      pl.BlockSpec(memory_space=pl.ANY)],
            out_specs=out_specs,
            scratch_shapes=[pltpu.VMEM((2, 2, tm, d), F32), pltpu.SemaphoreType.DMA((2,))],
        ),
        out_shape=out_shape,
        compiler_params=_params(("arbitrary",), 40),
        name="moe_combine_ln",
    )(pos1, pos2, x, route, g_stack.reshape(-1, 1, d), b_stack.reshape(-1, 1, d), y_slots)


def kernel(x_prompt, x_sample, cache_k, cache_v, page_table, state_conv, state_lru_conv, state_lru_h, ln1_g, ln1_b, ln2_g, ln2_b, sc_w_in, sc_w_conv, sc_w_out, attn_w_qkv, attn_w_o, lru_w_in, lru_w_conv, lru_b_conv, lru_w_gate_a, lru_b_gate_a, lru_w_gate_x, lru_b_gate_x, lru_lambda, lru_w_out, ffn_w_gate, ffn_w_up, ffn_w_down, moe_w_router, moe_w_gate, moe_w_up, moe_w_down):
    bp, seq, d = x_prompt.shape
    bs, dec_seq, _ = x_sample.shape
    depth = ln1_g.shape[0]
    n_heads, hd = cache_k.shape[3], cache_k.shape[4]
    page = cache_k.shape[2]
    past_len = page_table.shape[1] * page
    n_exp = moe_w_router.shape[2]
    assert dec_seq == 1 and bs <= TOKEN_TILE
    assert seq % MOBA_BLOCK == 0 and seq % TOKEN_TILE == 0 and past_len % MOBA_BLOCK == 0
    assert MOBA_BLOCK % page == 0 and past_len // MOBA_BLOCK >= MOBA_TOPK
    alpha = float((2.0 * depth) ** 0.25)
    mp = bp * seq
    n_valid = mp + bs
    m_tot = -(-n_valid // TOKEN_TILE) * TOKEN_TILE
    ppb = MOBA_BLOCK // page

    x = jnp.concatenate([x_prompt.reshape(mp, d), x_sample.reshape(bs, d),
                         jnp.zeros((m_tot - n_valid, d), x_prompt.dtype)], axis=0)
    xb = x.astype(BF16)
    dense_sched = _dense_schedule(m_tot // TOKEN_TILE)

    conv_p, conv_s, kp_l, vp_l, ks_l, vs_l, lc_p, lc_s, lh_p, lh_s = ([] for _ in range(10))
    final = None
    for i in range(depth):
        kind, m = i % 3, i // 3
        if kind == 0:
            proj = _matmul(xb, sc_w_in, m)
            y, new_p, u_s = _short_conv(proj, sc_w_conv, m, state_conv[m, :, 0], state_conv[m, :, 1],
                                        bp=bp, seq=seq, bs=bs)
            conv_p.append(new_p)
            conv_s.append(jnp.stack([state_conv[m, :, 1], u_s], axis=1))
            mixed, w_out = y, sc_w_out
        elif kind == 1:
            qkv = _matmul(xb, attn_w_qkv, m)
            q, k_p, v_p, k_s, v_s = _rope(qkv, n_heads=n_heads, hd=hd, bp=bp, seq=seq, bs=bs, past_len=past_len)
            kp_l.append(k_p.reshape(bp, seq, n_heads, hd))
            vp_l.append(v_p.reshape(bp, seq, n_heads, hd))
            ks_l.append(k_s.reshape(bs, 1, n_heads, hd))
            vs_l.append(v_s.reshape(bs, 1, n_heads, hd))
            o_p = _attn_prompt(q, k_p, v_p, bp=bp, seq=seq, n_heads=n_heads, hd=hd)
            q_s = q[mp:mp + bs]
            kmean = _page_means(cache_k, m, page_table)
            sel = _select_blocks(kmean.transpose(0, 2, 1, 3), q_s)
            sel = sel[:, :MOBA_TOPK, :n_heads].transpose(0, 2, 1)
            logical = sel[..., None] * ppb + jnp.arange(ppb, dtype=I32)
            phys = jnp.take_along_axis(page_table, logical.reshape(bs, -1), axis=1)
            o_s = _decode_attn(q_s, k_s, v_s, cache_k, cache_v, m, phys.reshape(bs, n_heads, -1),
                               n_heads=n_heads, hd=hd)
            o = jnp.concatenate([o_p, o_s.astype(BF16), jnp.zeros((m_tot - n_valid, n_heads * hd), BF16)], axis=0)
            mixed, w_out = o, attn_w_o
        else:
            proj = _matmul(xb, lru_w_in, m)
            y, new_c, new_h, u_s, h_s = _rglru(proj, lru_w_conv, lru_b_conv, lru_w_gate_a, lru_b_gate_a,
                                               lru_w_gate_x, lru_b_gate_x, lru_lambda, m,
                                               state_lru_conv[m], state_lru_h[m], bp=bp, seq=seq, bs=bs)
            lc_p.append(new_c)
            lc_s.append(jnp.concatenate([state_lru_conv[m, :, 1:], u_s[:, None, :]], axis=1))
            lh_p.append(new_h.reshape(bp, -1))
            lh_s.append(h_s)
            mixed, w_out = y, lru_w_out
        if i % 2 == 0:
            x, xb = _matmul_add_ln(mixed, w_out, m, x, ln1_g, ln1_b, i, alpha)
        else:
            x, xb, route = _matmul_add_ln(mixed, w_out, m, x, ln1_g, ln1_b, i, alpha,
                                          w_router=moe_w_router, router_layer=i // 2)
        j = i // 2
        if i % 2 == 0:
            f = _ffn(xb, ffn_w_gate[:, None], ffn_w_up[:, None], ffn_w_down[:, None], j, *dense_sched)
            x, xb = _add_ln(x, f, ln2_g, ln2_b, i, alpha)
        else:
            pos1, pos2, slot_token, n_used, sched, _ = _moe_plan(route, n_valid, n_exp)
            xs = _gather_rows(x, slot_token, n_used)
            ys = _ffn(xs, moe_w_gate, moe_w_up, moe_w_down, j, *sched, n_used)
            if i == depth - 1 and m_tot == mp + TOKEN_TILE:
                final = _combine_ln(x, route, ys, pos1, pos2, ln2_g, ln2_b, i, alpha, n_prompt_rows=mp)
            else:
                x, xb = _combine_ln(x, route, ys, pos1, pos2, ln2_g, ln2_b, i, alpha)
    x_p, x_s = final if final is not None else (x[:mp], x[mp:])
    return (x_p.reshape(bp, seq, d), x_s[:bs].reshape(bs, 1, d),
            jnp.stack(conv_p), jnp.stack(conv_s),
            jnp.stack(kp_l), jnp.stack(vp_l), jnp.stack(ks_l), jnp.stack(vs_l),
            jnp.stack(lc_p), jnp.stack(lc_s), jnp.stack(lh_p), jnp.stack(lh_s))
```

```python
import functools

import numpy as np
import jax
import jax.numpy as jnp
from jax import lax
from jax.experimental import pallas as pl
from jax.experimental.pallas import tpu as pltpu

F32, BF16, I32 = jnp.float32, jnp.bfloat16, jnp.int32

MOBA_BLOCK = 256
MOBA_TOPK = 3
ROPE_THETA = 500000.0
ROT_FRACTION = 4
LRU_C = 8.0
MOE_TOP_K = 2
LN_EPS = 1e-5
NEG_INF = -1e30
LOG2_E = 1.4426950408889634

LANES = 128
SUBLANES = 8
V7X_VMEM_BYTES = 64 * 1024 * 1024

TOKEN_TILE = 256
FFN_COL_TILE = 256
FFN_SUPER_TILES = 10
FFN_STEP_TILES = 4
DMA_ISSUE_UNROLL = 8


def _params(sem, vmem_mib, **kw):
    return pltpu.CompilerParams(dimension_semantics=sem, vmem_limit_bytes=vmem_mib * 1024 * 1024, **kw)


def _pick(total, candidates):
    for c in candidates:
        if total % c == 0:
            return c
    raise ValueError(f"no tile in {candidates} divides {total}")


def _mm_body(x_ref, w_ref, o_ref, wb_ref):
    @pl.when(pl.program_id(1) == 0)
    def _cast():
        wb_ref[...] = w_ref[...].astype(BF16)

    o_ref[...] = jnp.dot(x_ref[...], wb_ref[...], preferred_element_type=F32).astype(o_ref.dtype)


def _matmul(xb, w_stack, layer, out_dtype=F32):
    m, k = xb.shape
    n = w_stack.shape[2]
    tm = _pick(m, (768, 512, 256))
    tn = _pick(n, (1024, 512, 256, 128))
    return pl.pallas_call(
        _mm_body,
        grid=(n // tn, m // tm),
        in_specs=[pl.BlockSpec((tm, k), lambda j, i: (i, 0)),
                  pl.BlockSpec((None, k, tn), lambda j, i: (layer, 0, j))],
        out_specs=pl.BlockSpec((tm, tn), lambda j, i: (i, j)),
        out_shape=jax.ShapeDtypeStruct((m, n), out_dtype),
        scratch_shapes=[pltpu.VMEM((k, tn), BF16)],
        compiler_params=_params(("arbitrary", "arbitrary"), 48),
        name="matmul",
    )(xb, w_stack)


def _layer_norm(z, g, b):
    mu = jnp.mean(z, axis=-1, keepdims=True)
    zc = z - mu
    var = jnp.mean(zc * zc, axis=-1, keepdims=True)
    return zc * lax.rsqrt(var + LN_EPS) * g + b


def _route_top2(x, w, n_exp):
    xh, wh = x.astype(BF16), w.astype(BF16)
    xl, wl = (x - xh.astype(F32)).astype(BF16), (w - wh.astype(F32)).astype(BF16)
    dot = functools.partial(jnp.dot, preferred_element_type=F32)
    logits = dot(xh, wh) + (dot(xh, wl) + dot(xl, wh))
    lane = lax.broadcasted_iota(I32, logits.shape, 1)
    lg = jnp.where(lane < n_exp, logits, -jnp.inf)
    m1 = jnp.max(lg, axis=1, keepdims=True)
    i1 = jnp.min(jnp.where(lg == m1, lane, LANES), axis=1, keepdims=True)
    lg = jnp.where(lane == i1, -jnp.inf, lg)
    m2 = jnp.max(lg, axis=1, keepdims=True)
    i2 = jnp.min(jnp.where(lg == m2, lane, LANES), axis=1, keepdims=True)
    e = jnp.exp(m2 - m1)
    g1 = 1.0 / (1.0 + e)
    g2 = e / (1.0 + e)
    return jnp.where(lane == 0, i1.astype(F32),
                     jnp.where(lane == 1, i2.astype(F32),
                               jnp.where(lane == 2, g1, jnp.where(lane == 3, g2, 0.0))))


def _mm_ln_body(y_ref, w_ref, x_ref, g_ref, b_ref, *rest, alpha, n_exp, n_head_tiles):
    rest = list(rest)
    wr_ref = rest.pop(0) if n_exp is not None else None
    tail_ref = rest.pop(0) if n_head_tiles is not None else None
    o_ref, ob_ref = rest.pop(0), rest.pop(0)
    route_ref = rest.pop(0) if n_exp is not None else None
    (wb_ref,) = rest

    @pl.when(pl.program_id(0) == 0)
    def _cast():
        wb_ref[...] = w_ref[...].astype(BF16)

    y = y_ref[...]
    if tail_ref is not None:
        y = jnp.where(pl.program_id(0) == n_head_tiles, tail_ref[...], y)
    mix = jnp.dot(y, wb_ref[...], preferred_element_type=F32)
    out = _layer_norm(alpha * x_ref[...] + mix, g_ref[...], b_ref[...])
    o_ref[...] = out
    ob_ref[...] = out.astype(BF16)
    if n_exp is not None:
        route_ref[...] = _route_top2(out, wr_ref[...], n_exp)


def _matmul_add_ln(yb, w_stack, w_layer, x, g_stack, b_stack, layer, alpha, w_router=None, router_layer=None,
                   y_tail=None):
    m, d = x.shape
    k = yb.shape[1]
    assert w_stack.shape[2] == d
    tm = TOKEN_TILE
    row = lambda width: pl.BlockSpec((tm, width), lambda i: (i, 0))
    par = pl.BlockSpec((None, 1, d), lambda i: (layer, 0, 0))
    n_head_tiles = None
    y_spec = row(k)
    if y_tail is not None:
        n_head_tiles = yb.shape[0] // tm
        assert yb.shape[0] == n_head_tiles * tm and m == yb.shape[0] + tm and y_tail.shape == (tm, k)
        y_spec = pl.BlockSpec((tm, k), lambda i: (jnp.minimum(i, n_head_tiles - 1), 0))
    in_specs = [y_spec,
                pl.BlockSpec((None, k, d), lambda i: (w_layer, 0, 0), pipeline_mode=pl.Buffered(1)),
                row(d), par, par]
    out_specs = [row(d), row(d)]
    out_shape = [jax.ShapeDtypeStruct((m, d), F32), jax.ShapeDtypeStruct((m, d), BF16)]
    args = [yb, w_stack, x, g_stack.reshape(-1, 1, d), b_stack.reshape(-1, 1, d)]
    n_exp = None
    if w_router is not None:
        n_exp = w_router.shape[2]
        in_specs.append(pl.BlockSpec((None, d, LANES), lambda i: (router_layer, 0, 0)))
        out_specs.append(row(LANES))
        out_shape.append(jax.ShapeDtypeStruct((m, LANES), F32))
        args.append(jnp.pad(w_router, ((0, 0), (0, 0), (0, LANES - n_exp))))
    if y_tail is not None:
        in_specs.append(pl.BlockSpec((tm, k), lambda i: (0, 0)))
        args.append(y_tail)
    return pl.pallas_call(
        functools.partial(_mm_ln_body, alpha=alpha, n_exp=n_exp, n_head_tiles=n_head_tiles),
        grid=(m // tm,),
        in_specs=in_specs,
        out_specs=out_specs,
        out_shape=out_shape,
        scratch_shapes=[pltpu.VMEM((k, d), BF16)],
        compiler_params=_params(("arbitrary",), 48),
        name="matmul_add_ln",
    )(*args)


def _add_ln_body(x_ref, y_ref, g_ref, b_ref, o_ref, ob_ref, *, alpha):
    out = _layer_norm(alpha * x_ref[...] + y_ref[...], g_ref[...], b_ref[...])
    o_ref[...] = out
    ob_ref[...] = out.astype(BF16)


def _add_ln(x, y, g_stack, b_stack, layer, alpha):
    m, d = x.shape
    tm = TOKEN_TILE
    row = pl.BlockSpec((tm, d), lambda i: (i, 0))
    par = pl.BlockSpec((None, 1, d), lambda i: (layer, 0, 0))
    return pl.pallas_call(
        functools.partial(_add_ln_body, alpha=alpha),
        grid=(m // tm,),
        in_specs=[row, row, par, par],
        out_specs=[row, row],
        out_shape=[jax.ShapeDtypeStruct((m, d), F32), jax.ShapeDtypeStruct((m, d), BF16)],
        compiler_params=_params(("arbitrary",), 32),
        name="add_ln",
    )(x, y, g_stack.reshape(-1, 1, d), b_stack.reshape(-1, 1, d))


def _shift_rows(u, k, carry_rows):
    out = pltpu.roll(u, k, 0)
    row = lax.broadcasted_iota(I32, u.shape, 0)
    for r, c in enumerate(carry_rows):
        out = jnp.where(row == r, c, out)
    return out


def _sconv_body(b_ref, c_ref, h_ref, w_ref, s0_ref, s1_ref, y_ref, newp_ref, news_ref, carry_ref,
                *, tiles_per_seq, n_prompt_tiles, bs):
    i = pl.program_id(0)
    w0, w1, w2 = w_ref[0:1, :], w_ref[1:2, :], w_ref[2:3, :]
    tm = y_ref.shape[0]

    @pl.when(i < n_prompt_tiles)
    def _prompt():
        u = c_ref[...] * h_ref[...]
        fresh = i % tiles_per_seq == 0
        c0 = jnp.where(fresh, 0.0, carry_ref[0:1, :])
        c1 = jnp.where(fresh, 0.0, carry_ref[1:2, :])
        p1 = _shift_rows(u, 1, (c1,))
        p2 = _shift_rows(u, 2, (c0, c1))
        y = w0 * p2 + w1 * p1 + w2 * u
        y_ref[...] = (b_ref[...] * y).astype(y_ref.dtype)
        carry_ref[0:2, :] = u[tm - 2:tm, :]
        newp_ref[...] = u[tm - 2:tm, :]

    @pl.when(i == n_prompt_tiles)
    def _sample():
        u = c_ref[0:bs, :] * h_ref[0:bs, :]
        y = w0 * s0_ref[...] + w1 * s1_ref[...] + w2 * u
        y_ref[...] = jnp.zeros(y_ref.shape, y_ref.dtype)
        y_ref[0:bs, :] = (b_ref[0:bs, :] * y).astype(y_ref.dtype)
        news_ref[...] = u


def _short_conv(proj, w_conv, layer, s0, s1, *, bp, seq, bs):
    m_tot = proj.shape[0]
    d = proj.shape[1] // 3
    tm = TOKEN_TILE
    tps = seq // tm
    npt = bp * tps
    col = lambda c: pl.BlockSpec((tm, d), lambda i: (i, c))
    full = lambda shape: pl.BlockSpec(shape, lambda i: (0,) * len(shape))
    return pl.pallas_call(
        functools.partial(_sconv_body, tiles_per_seq=tps, n_prompt_tiles=npt, bs=bs),
        grid=(m_tot // tm,),
        in_specs=[col(0), col(1), col(2),
                  pl.BlockSpec((None, w_conv.shape[1], d), lambda i: (layer, 0, 0)),
                  full((bs, d)), full((bs, d))],
        out_specs=[pl.BlockSpec((tm, d), lambda i: (i, 0)),
                   pl.BlockSpec((None, 2, d), lambda i: (jnp.minimum(i // tps, bp - 1), 0, 0)),
                   full((bs, d))],
        out_shape=[jax.ShapeDtypeStruct((m_tot, d), BF16),
                   jax.ShapeDtypeStruct((bp, 2, d), F32),
                   jax.ShapeDtypeStruct((bs, d), F32)],
        scratch_shapes=[pltpu.VMEM((SUBLANES, d), F32)],
        compiler_params=_params(("arbitrary",), 40),
        name="short_conv",
    )(proj, proj, proj, w_conv, s0, s1)


def _rope_head(x, cos_f, sin_f, half):
    lane = lax.broadcasted_iota(I32, x.shape, 1)
    partner = jnp.where(lane < half, pltpu.roll(x, LANES - half, 1), pltpu.roll(x, half, 1))
    return x * cos_f + partner * sin_f


def _rope_body(q_ref, k_ref, v_ref, cos_ref, sin_ref, cos_s_ref, sin_s_ref,
               qo_ref, kp_ref, vp_ref, ks_ref, vs_ref, *, n_heads, hd, half, n_prompt_tiles, bs):
    i = pl.program_id(0)

    def rotate(x_ref, rows, cos_f, sin_f):
        return [
            _rope_head(x_ref[rows, h * hd:(h + 1) * hd], cos_f, sin_f, half) for h in range(n_heads)
        ]

    @pl.when(i < n_prompt_tiles)
    def _prompt():
        rows = slice(None)
        cos_f, sin_f = cos_ref[...], sin_ref[...]
        for h, (qh, kh) in enumerate(zip(rotate(q_ref, rows, cos_f, sin_f), rotate(k_ref, rows, cos_f, sin_f))):
            qo_ref[:, h * hd:(h + 1) * hd] = qh
            kp_ref[:, h * hd:(h + 1) * hd] = kh
        vp_ref[...] = v_ref[...]

    @pl.when(i == n_prompt_tiles)
    def _sample():
        rows = slice(0, bs)
        cos_f, sin_f = cos_s_ref[...], sin_s_ref[...]
        qo_ref[...] = jnp.zeros(qo_ref.shape, qo_ref.dtype)
        for h, (qh, kh) in enumerate(zip(rotate(q_ref, rows, cos_f, sin_f), rotate(k_ref, rows, cos_f, sin_f))):
            qo_ref[0:bs, h * hd:(h + 1) * hd] = qh
            ks_ref[:, h * hd:(h + 1) * hd] = kh
        vs_ref[...] = v_ref[0:bs, :]


def _rope_tables(pos, hd):
    rot = hd // ROT_FRACTION
    half = rot // 2
    inv_freq = ROPE_THETA ** (-jnp.arange(half, dtype=F32) * 2.0 / rot)
    ang = pos.astype(F32)[:, None] * inv_freq[None, :]
    cos, sin = jnp.cos(ang), jnp.sin(ang)
    n = pos.shape[0]
    cos_f = jnp.concatenate([cos, cos, jnp.ones((n, hd - rot), F32)], axis=1)
    sin_f = jnp.concatenate([-sin, sin, jnp.zeros((n, hd - rot), F32)], axis=1)
    return cos_f, sin_f


def _rope(qkv, *, n_heads, hd, bp, seq, bs, past_len):
    m_tot = qkv.shape[0]
    hdim = n_heads * hd
    tm = TOKEN_TILE
    tps = seq // tm
    npt = bp * tps
    mp = bp * seq
    cos_p, sin_p = _rope_tables(jnp.arange(seq, dtype=I32), hd)
    cos_s, sin_s = _rope_tables(jnp.full((1,), past_len, I32), hd)
    col = lambda c: pl.BlockSpec((tm, hdim), lambda i: (i, c))
    tab = pl.BlockSpec((tm, hd), lambda i: (i % tps, 0))
    one = pl.BlockSpec((1, hd), lambda i: (0, 0))
    prow = pl.BlockSpec((tm, hdim), lambda i: (jnp.minimum(i, npt - 1), 0))
    srow = pl.BlockSpec((bs, hdim), lambda i: (0, 0))
    return pl.pallas_call(
        functools.partial(_rope_body, n_heads=n_heads, hd=hd, half=hd // ROT_FRACTION // 2,
                          n_prompt_tiles=npt, bs=bs),
        grid=(m_tot // tm,),
        in_specs=[col(0), col(1), col(2), tab, tab, one, one],
        out_specs=[pl.BlockSpec((tm, hdim), lambda i: (i, 0)), prow, prow, srow, srow],
        out_shape=[jax.ShapeDtypeStruct((m_tot, hdim), F32),
                   jax.ShapeDtypeStruct((mp, hdim), F32), jax.ShapeDtypeStruct((mp, hdim), F32),
                   jax.ShapeDtypeStruct((bs, hdim), F32), jax.ShapeDtypeStruct((bs, hdim), F32)],
        compiler_params=_params(("arbitrary",), 48),
        name="rope",
    )(qkv, qkv, qkv, cos_p, sin_p, cos_s, sin_s)


def _nt_dot(a, b, **kw):
    return lax.dot_general(a, b, (((1,), (1,)), ((), ())), preferred_element_type=F32, **kw)


def _attn_body(q_ref, k_ref, v_ref, o_ref, kmean_ref, kb_ref, vt_ref, *, nb, scale):
    blk = MOBA_BLOCK
    kmean_ref[...] = jnp.zeros(kmean_ref.shape, F32)
    for n in range(nb):
        rows = slice(n * blk, (n + 1) * blk)
        kmean_ref[n:n + 1, :] = jnp.sum(k_ref[rows, :], axis=0, keepdims=True) * (1.0 / blk)
        vt_ref[:, rows] = v_ref[rows, :].T.astype(BF16)
    kb_ref[...] = k_ref[...].astype(BF16)
    gate_all = _nt_dot(kmean_ref[...], q_ref[...], precision=lax.Precision.HIGHEST)
    bid = lax.broadcasted_iota(I32, (kmean_ref.shape[0], blk), 0)
    ki = lax.broadcasted_iota(I32, (blk, blk), 0)
    qi = lax.broadcasted_iota(I32, (blk, blk), 1)

    for j in range(nb):
        cols = slice(j * blk, (j + 1) * blk)
        gate = jnp.where(bid < j, gate_all[:, cols], NEG_INF)
        rank = jnp.zeros(gate.shape, I32)
        for m in range(j):
            gm = gate[m:m + 1, :]
            beats = (gm > gate) | ((gm == gate) & (m < bid))
            rank = rank + jnp.where(beats, 1, 0)
        sel = jnp.where((bid < j) & (rank < MOBA_TOPK), 1.0, 0.0)
        keys = (j + 1) * blk
        s = _nt_dot(kb_ref[0:keys, :], q_ref[cols, :].astype(BF16)) * (scale * LOG2_E)
        slabs = [jnp.where(sel[n:n + 1, :] > 0.0, s[n * blk:(n + 1) * blk, :], NEG_INF) for n in range(j)]
        slabs.append(jnp.where(ki <= qi, s[j * blk:keys, :], NEG_INF))
        top = slabs[0]
        for sl in slabs[1:]:
            top = jnp.maximum(top, sl)
        top = jnp.max(top, axis=0, keepdims=True)
        p = [jnp.exp2(sl - top) for sl in slabs]
        tot = p[0]
        for pn in p[1:]:
            tot = tot + pn
        denom = jnp.sum(tot, axis=0, keepdims=True)
        pv = jnp.dot(vt_ref[:, 0:keys], jnp.concatenate(p, axis=0).astype(BF16),
                     preferred_element_type=F32)
        o_ref[cols, :] = (pv / denom).T.astype(o_ref.dtype)


def _attn_prompt(q, k_p, v_p, *, bp, seq, n_heads, hd):
    nb = seq // MOBA_BLOCK
    nbp = -(-nb // SUBLANES) * SUBLANES
    blk = pl.BlockSpec((seq, hd), lambda b, h: (b, h))
    return pl.pallas_call(
        functools.partial(_attn_body, nb=nb, scale=hd ** -0.5),
        grid=(bp, n_heads),
        in_specs=[blk, blk, blk],
        out_specs=blk,
        out_shape=jax.ShapeDtypeStruct((bp * seq, n_heads * hd), BF16),
        scratch_shapes=[pltpu.VMEM((nbp, hd), F32), pltpu.VMEM((seq, hd), BF16), pltpu.VMEM((hd, seq), BF16)],
        compiler_params=_params(("arbitrary", "arbitrary"), 48),
        name="moba_prompt",
    )(q, k_p, v_p)


def _page_mean_body(pt_ref, *refs, pages_per_step, pages_per_block):
    del pt_ref
    pages, o_ref = refs[:pages_per_step], refs[pages_per_step]
    inv = 1.0 / MOBA_BLOCK
    for c in range(pages_per_step // pages_per_block):
        tot = jnp.sum(pages[c * pages_per_block][...], axis=0)
        for r in range(1, pages_per_block):
            tot = tot + jnp.sum(pages[c * pages_per_block + r][...], axis=0)
        o_ref[c] = tot * inv


def _page_means(cache_k, layer, page_table):
    _, _, page, n_heads, hd = cache_k.shape
    bs, n_pages = page_table.shape
    ppb = MOBA_BLOCK // page
    pps = _pick(n_pages, (16, 8, 4, 2)) if ppb == 2 else ppb
    n_blocks = n_pages // ppb

    def page_spec(c):
        return pl.BlockSpec((None, None, page, n_heads, hd),
                            lambda b, s, pt: (layer, pt[b * n_pages + s * pps + c], 0, 0, 0))

    return pl.pallas_call(
        functools.partial(_page_mean_body, pages_per_step=pps, pages_per_block=ppb),
        grid_spec=pltpu.PrefetchScalarGridSpec(
            num_scalar_prefetch=1,
            grid=(bs, n_pages // pps),
            in_specs=[page_spec(c) for c in range(pps)],
            out_specs=pl.BlockSpec((None, pps // ppb, n_heads, hd), lambda b, s, pt: (b, s, 0, 0)),
        ),
        out_shape=jax.ShapeDtypeStruct((bs, n_blocks, n_heads, hd), F32),
        compiler_params=_params(("arbitrary", "arbitrary"), 48),
        name="page_means",
    )(page_table.reshape(-1), *([cache_k] * pps))


def _select_body(km_ref, q_ref, o_ref, *, n_heads):
    n_blocks = km_ref.shape[1]
    res = jnp.zeros(o_ref.shape, I32)
    row = lax.broadcasted_iota(I32, o_ref.shape, 0)
    lane = lax.broadcasted_iota(I32, o_ref.shape, 1)
    blk_id = lax.broadcasted_iota(I32, (n_blocks, 1), 0)
    for h in range(n_heads):
        gate = jnp.sum(km_ref[h] * q_ref[h:h + 1, :], axis=1, keepdims=True)
        for k in range(MOBA_TOPK):
            best = jnp.max(gate, axis=0, keepdims=True)
            idx = jnp.min(jnp.where(gate == best, blk_id, n_blocks), axis=0, keepdims=True)
            res = jnp.where((row == k) & (lane == h), idx, res)
            gate = jnp.where(blk_id == idx, -jnp.inf, gate)
    o_ref[...] = res


def _select_blocks(kmean_t, q_s):
    bs, n_heads, n_blocks, hd = kmean_t.shape
    return pl.pallas_call(
        functools.partial(_select_body, n_heads=n_heads),
        grid=(bs,),
        in_specs=[pl.BlockSpec((None, n_heads, n_blocks, hd), lambda b: (b, 0, 0, 0)),
                  pl.BlockSpec((None, n_heads, hd), lambda b: (b, 0, 0))],
        out_specs=pl.BlockSpec((None, SUBLANES, LANES), lambda b: (b, 0, 0)),
        out_shape=jax.ShapeDtypeStruct((bs, SUBLANES, LANES), I32),
        compiler_params=_params(("arbitrary",), 32),
        name="moba_select",
    )(kmean_t, q_s.reshape(bs, n_heads, hd))


def _decode_attn_body(ph_ref, q_ref, kn_ref, vn_ref, ck_hbm, cv_hbm, o_ref, kbuf, vbuf, sem,
                      *, layer, n_heads, hd, n_sel_pages, scale):
    b = pl.program_id(0)

    def start_copies(tok, slot):
        for h in range(n_heads):
            for r in range(n_sel_pages):
                pg = ph_ref[(tok * n_heads + h) * n_sel_pages + r]
                pltpu.make_async_copy(ck_hbm.at[layer, pg, :, h, :], kbuf.at[slot, h, r], sem.at[slot]).start()
                pltpu.make_async_copy(cv_hbm.at[layer, pg, :, h, :], vbuf.at[slot, h, r], sem.at[slot]).start()

    @pl.when(b == 0)
    def _first():
        start_copies(b, 0)

    @pl.when(b + 1 < pl.num_programs(0))
    def _prefetch():
        start_copies(b + 1, (b + 1) % 2)

    slot = b % 2
    pltpu.make_async_copy(kbuf.at[1 - slot], kbuf.at[slot], sem.at[slot]).wait()
    pltpu.make_async_copy(vbuf.at[1 - slot], vbuf.at[slot], sem.at[slot]).wait()

    row = pl.ds(b, 1)
    q_row, kn_row, vn_row = q_ref[row, :], kn_ref[row, :], vn_ref[row, :]
    out = []
    for h in range(n_heads):
        cols = slice(h * hd, (h + 1) * hd)
        q = q_row[:, cols]
        s_new = jnp.sum(q * kn_row[:, cols], axis=1, keepdims=True) * scale
        scores = [jnp.sum(kbuf[slot, h, r] * q, axis=1, keepdims=True) * scale for r in range(n_sel_pages)]
        top = s_new
        for s in scores:
            top = jnp.maximum(top, jnp.max(s, axis=0, keepdims=True))
        p_new = jnp.exp(s_new - top)
        denom = p_new
        acc = p_new * vn_row[:, cols]
        for r, s in enumerate(scores):
            p = jnp.exp(s - top)
            denom = denom + jnp.sum(p, axis=0, keepdims=True)
            acc = acc + jnp.sum(p * vbuf[slot, h, r], axis=0, keepdims=True)
        out.append(acc / denom)
    o_ref[row, :] = jnp.concatenate(out, axis=1)


def _decode_attn(q_s, k_s, v_s, cache_k, cache_v, layer, phys_pages, *, n_heads, hd):
    bs = q_s.shape[0]
    page = cache_k.shape[2]
    nsp = phys_pages.shape[-1]
    rows = pl.BlockSpec((bs, n_heads * hd), lambda b, ph: (0, 0))
    anywhere = pl.BlockSpec(memory_space=pl.ANY)
    return pl.pallas_call(
        functools.partial(_decode_attn_body, layer=layer, n_heads=n_heads, hd=hd, n_sel_pages=nsp,
                          scale=hd ** -0.5),
        grid_spec=pltpu.PrefetchScalarGridSpec(
            num_scalar_prefetch=1,
            grid=(bs,),
            in_specs=[rows, rows, rows, anywhere, anywhere],
            out_specs=rows,
            scratch_shapes=[pltpu.VMEM((2, n_heads, nsp, page, hd), F32),
                            pltpu.VMEM((2, n_heads, nsp, page, hd), F32),
                            pltpu.SemaphoreType.DMA((2,))],
        ),
        out_shape=jax.ShapeDtypeStruct((bs, n_heads * hd), F32),
        compiler_params=_params(("arbitrary",), 40),
        name="moba_decode",
    )(phys_pages.reshape(-1), q_s, k_s, v_s, cache_k, cache_v)


def _gelu_tanh(x):
    return 0.5 * x * (1.0 + jnp.tanh(np.sqrt(2.0 / np.pi).astype(np.float32) * (x + 0.044715 * (x * x * x))))


def _softplus(x):
    return jnp.maximum(x, 0.0) + jnp.log1p(jnp.exp(-jnp.abs(x)))


def _lru_gates(uc, wa_ref, wx_ref, ba, bx, lam, n_heads, rb):
    ucb = uc.astype(BF16)
    a_parts, b_parts = [], []
    sp = _softplus(-lam)
    for h in range(n_heads):
        cols = slice(h * rb, (h + 1) * rb)
        r = jax.nn.sigmoid(jnp.dot(ucb[:, cols], wa_ref[h].astype(BF16), preferred_element_type=F32) + ba[:, cols])
        g = jax.nn.sigmoid(jnp.dot(ucb[:, cols], wx_ref[h].astype(BF16), preferred_element_type=F32) + bx[:, cols])
        a = jnp.exp(-LRU_C * r * sp[:, cols])
        a_parts.append(a)
        b_parts.append(jnp.sqrt(1.0 - a * a) * (g * uc[:, cols]))
    return jnp.concatenate(a_parts, axis=1), jnp.concatenate(b_parts, axis=1)


def _lru_body(g_ref, u_ref, wc_ref, bc_ref, wa_ref, wx_ref, ba_ref, bx_ref, lam_ref,
              c0_ref, c1_ref, c2_ref, h0_ref,
              y_ref, newc_ref, newh_ref, us_ref, hs_ref,
              carry_ref, hcarry_ref, a_buf, b_buf, *, tiles_per_seq, n_prompt_tiles, bs, n_heads, rb):
    i = pl.program_id(0)
    tm = y_ref.shape[0]
    w0, w1, w2, w3 = wc_ref[0:1, :], wc_ref[1:2, :], wc_ref[2:3, :], wc_ref[3:4, :]
    bias = bc_ref[...]
    gates = functools.partial(_lru_gates, wa_ref=wa_ref, wx_ref=wx_ref, ba=ba_ref[...], bx=bx_ref[...],
                              lam=lam_ref[...], n_heads=n_heads, rb=rb)

    @pl.when(i < n_prompt_tiles)
    def _prompt():
        u = u_ref[...]
        fresh = i % tiles_per_seq == 0
        c0, c1, c2 = (jnp.where(fresh, 0.0, carry_ref[r:r + 1, :]) for r in range(3))
        p1 = _shift_rows(u, 1, (c2,))
        p2 = _shift_rows(u, 2, (c1, c2))
        p3 = _shift_rows(u, 3, (c0, c1, c2))
        uc = w0 * p3 + w1 * p2 + w2 * p1 + w3 * u + bias
        a, b = gates(uc)
        a_buf[...] = a
        b_buf[...] = b
        sub = lax.broadcasted_iota(I32, (SUBLANES, a.shape[1]), 0)

        def group(gi, h):
            rows = pl.ds(pl.multiple_of(gi * SUBLANES, SUBLANES), SUBLANES)
            ag, bg = a_buf[rows, :], b_buf[rows, :]
            for d in (1, 2, 4):
                bg = bg + ag * jnp.where(sub >= d, pltpu.roll(bg, d, 0), 0.0)
                ag = ag * jnp.where(sub >= d, pltpu.roll(ag, d, 0), 1.0)
            hg = ag * h + bg
            b_buf[rows, :] = hg
            return hg[SUBLANES - 1:SUBLANES, :]

        h_last = lax.fori_loop(0, tm // SUBLANES, group, jnp.where(fresh, 0.0, hcarry_ref[0:1, :]))
        hcarry_ref[0:1, :] = h_last
        y_ref[...] = (b_buf[...] * _gelu_tanh(g_ref[...])).astype(y_ref.dtype)
        carry_ref[0:3, :] = u[tm - 3:tm, :]
        newc_ref[...] = u[tm - 3:tm, :]
        newh_ref[...] = h_last

    @pl.when(i == n_prompt_tiles)
    def _sample():
        u = u_ref[0:bs, :]
        uc = w0 * c0_ref[...] + w1 * c1_ref[...] + w2 * c2_ref[...] + w3 * u + bias
        a, b = gates(uc)
        h = a * h0_ref[...] + b
        y_ref[...] = jnp.zeros(y_ref.shape, y_ref.dtype)
        y_ref[0:bs, :] = (h * _gelu_tanh(g_ref[0:bs, :])).astype(y_ref.dtype)
        us_ref[...] = u
        hs_ref[...] = h


def _rglru(proj, w_conv, b_conv, w_gate_a, b_gate_a, w_gate_x, b_gate_x, lam, layer, conv_state, h0,
           *, bp, seq, bs):
    m_tot = proj.shape[0]
    dr = proj.shape[1] // 2
    n_heads, rb = w_gate_a.shape[1], w_gate_a.shape[2]
    tm = TOKEN_TILE
    tps = seq // tm
    npt = bp * tps
    width = w_conv.shape[1]
    col = lambda c: pl.BlockSpec((tm, dr), lambda i: (i, c))
    vec = pl.BlockSpec((None, 1, dr), lambda i: (layer, 0, 0))
    gw = pl.BlockSpec((None, n_heads, rb, rb), lambda i: (layer, 0, 0, 0))
    st = pl.BlockSpec((bs, dr), lambda i: (0, 0))
    pb = lambda rows: pl.BlockSpec((None, rows, dr), lambda i: (jnp.minimum(i // tps, bp - 1), 0, 0))
    return pl.pallas_call(
        functools.partial(_lru_body, tiles_per_seq=tps, n_prompt_tiles=npt, bs=bs, n_heads=n_heads, rb=rb),
        grid=(m_tot // tm,),
        in_specs=[col(0), col(1), pl.BlockSpec((None, width, dr), lambda i: (layer, 0, 0)), vec,
                  gw, gw, vec, vec, vec, st, st, st, st],
        out_specs=[pl.BlockSpec((tm, dr), lambda i: (i, 0)), pb(width - 1), pb(1), st, st],
        out_shape=[jax.ShapeDtypeStruct((m_tot, dr), BF16),
                   jax.ShapeDtypeStruct((bp, width - 1, dr), F32),
                   jax.ShapeDtypeStruct((bp, 1, dr), F32),
                   jax.ShapeDtypeStruct((bs, dr), F32),
                   jax.ShapeDtypeStruct((bs, dr), F32)],
        scratch_shapes=[pltpu.VMEM((SUBLANES, dr), F32), pltpu.VMEM((SUBLANES, dr), F32),
                        pltpu.VMEM((tm, dr), F32), pltpu.VMEM((tm, dr), F32)],
        compiler_params=_params(("arbitrary",), 48),
        name="rglru",
    )(proj, proj, w_conv, b_conv.reshape(-1, 1, dr), w_gate_a, w_gate_x,
      b_gate_a.reshape(-1, 1, dr), b_gate_x.reshape(-1, 1, dr), lam.reshape(-1, 1, dr),
      conv_state[:, 0], conv_state[:, 1], conv_state[:, 2], h0)


def _ffn_body(e_ref, start_ref, nt_ref, nused_ref, xs_hbm, wg_ref, wu_ref, wd_ref, o_hbm,
              xbuf, acc, wgb, wub, wdb, sem_in, sem_out, *, tm, nf):
    del e_ref
    s, f = pl.program_id(0), pl.program_id(1)
    nt = nt_ref[s]
    start = start_ref[s]

    def copy_in(r):
        return pltpu.make_async_copy(xs_hbm.at[pl.ds(pl.multiple_of(start + r * tm, tm), tm), :],
                                     xbuf.at[pl.ds(pl.multiple_of(r * tm, tm), tm), :], sem_in)

    def copy_out(r):
        return pltpu.make_async_copy(acc.at[pl.ds(pl.multiple_of(r * tm, tm), tm), :],
                                     o_hbm.at[pl.ds(pl.multiple_of(start + r * tm, tm), tm), :], sem_out)

    def for_tiles(fn):
        def body(r, c):
            fn(r)
            return c
        lax.fori_loop(0, nt, body, 0)

    @pl.when(nt > 0)
    def _run():
        @pl.when(f == 0)
        def _load():
            for_tiles(lambda r: copy_in(r).start())

            def clear(r):
                acc[pl.ds(pl.multiple_of(r * tm, tm), tm), :] = jnp.zeros((tm, acc.shape[1]), F32)

            for_tiles(clear)
            for_tiles(lambda r: copy_in(r).wait())

        def rows_step(first_tile, n_tiles, cast_weights=False):
            rows = pl.ds(pl.multiple_of(first_tile * tm, tm), n_tiles * tm)
            x = xbuf[rows, :]
            if cast_weights:
                wgb[...] = wg_ref[...].astype(BF16)
            g = jnp.dot(x, wgb[...], preferred_element_type=F32)
            if cast_weights:
                wub[...] = wu_ref[...].astype(BF16)
            u = jnp.dot(x, wub[...], preferred_element_type=F32)
            h = (g * jax.nn.sigmoid(g) * u).astype(BF16)
            if cast_weights:
                wdb[...] = wd_ref[...].astype(BF16)
            acc[rows, :] += jnp.dot(h, wdb[...], preferred_element_type=F32)

            @pl.when(f == nf - 1)
            def _store():
                for r in range(n_tiles):
                    copy_out(first_tile + r).start()

        big = FFN_STEP_TILES
        n_big = nt // big
        rem = nt % big

        @pl.when(n_big > 0)
        def _first_big():
            rows_step(0, big, cast_weights=True)

        def big_step(i, c):
            rows_step(big * i, big)
            return c

        lax.fori_loop(1, n_big, big_step, 0)

        size = big // 2
        while size >= 1:
            present = (rem // size) % 2 == 1
            first = (n_big == 0) & (rem < 2 * size)
            offset = n_big * big + rem - rem % (2 * size)

            @pl.when(present & first)
            def _piece_first(size=size):
                rows_step(0, size, cast_weights=True)

            @pl.when(present & jnp.logical_not(first))
            def _piece(size=size, offset=offset):
                rows_step(offset, size)

            size //= 2

        @pl.when(f == nf - 1)
        def _drain_stores():
            for_tiles(lambda r: copy_out(r).wait())

    @pl.when((s == pl.num_programs(0) - 1) & (f == nf - 1))
    def _zero_unused():
        acc[0:tm, :] = jnp.zeros((tm, acc.shape[1]), F32)

        def zero_tile(t, c):
            cp = pltpu.make_async_copy(acc.at[0:tm, :], o_hbm.at[pl.ds(pl.multiple_of(t * tm, tm), tm), :], sem_out)
            cp.start()
            cp.wait()
            return c

        lax.fori_loop(nused_ref[0], o_hbm.shape[0] // tm, zero_tile, 0)


def _ffn(xs, w_gate, w_up, w_down, layer, sched_e, sched_start, sched_nt, n_used_tiles):
    s_rows, d = xs.shape
    dff = w_gate.shape[3]
    tm, tf, rt = TOKEN_TILE, FFN_COL_TILE, FFN_SUPER_TILES
    nf = dff // tf
    n_super = sched_e.shape[0]

    def fcol(s, f, nt_ref):
        return jnp.where(nt_ref[s] > 0, f, nf - 1)

    up_spec = pl.BlockSpec((None, None, d, tf), lambda s, f, e, st, nt, nu: (layer, e[s], 0, fcol(s, f, nt)))
    down_spec = pl.BlockSpec((None, None, tf, d), lambda s, f, e, st, nt, nu: (layer, e[s], fcol(s, f, nt), 0))
    return pl.pallas_call(
        functools.partial(_ffn_body, tm=tm, nf=nf),
        grid_spec=pltpu.PrefetchScalarGridSpec(
            num_scalar_prefetch=4,
            grid=(n_super, nf),
            in_specs=[pl.BlockSpec(memory_space=pl.ANY), up_spec, up_spec, down_spec],
            out_specs=pl.BlockSpec(memory_space=pl.ANY),
            scratch_shapes=[pltpu.VMEM((rt * tm, d), BF16), pltpu.VMEM((rt * tm, d), F32),
                            pltpu.VMEM((d, tf), BF16), pltpu.VMEM((d, tf), BF16), pltpu.VMEM((tf, d), BF16),
                            pltpu.SemaphoreType.DMA(()), pltpu.SemaphoreType.DMA(())],
        ),
        out_shape=jax.ShapeDtypeStruct((s_rows, d), F32),
        compiler_params=_params(("arbitrary", "arbitrary"), 56),
        name="swiglu_ffn",
    )(sched_e, sched_start, sched_nt, n_used_tiles, xs, w_gate, w_up, w_down)


def _dense_schedule(n_tiles):
    rt = FFN_SUPER_TILES
    n_super = -(-n_tiles // rt)
    start = np.arange(n_super, dtype=np.int32) * rt
    nt = np.minimum(rt, n_tiles - start).astype(np.int32)
    return (jnp.zeros((n_super,), I32), jnp.asarray(start * TOKEN_TILE, I32), jnp.asarray(nt, I32),
            jnp.full((1,), n_tiles, I32))


def _slot_token_body(p1_ref, p2_ref, o_ref, *, n_valid, n_slots):
    clear_unroll = 4 * DMA_ISSUE_UNROLL

    def clear(g, c):
        for r in range(clear_unroll):
            o_ref[g * clear_unroll + r] = 0
        return c

    def place(t):
        o_ref[p1_ref[t]] = t
        o_ref[p2_ref[t]] = t

    def place_group(g, c):
        for r in range(DMA_ISSUE_UNROLL):
            place(g * DMA_ISSUE_UNROLL + r)
        return c

    def place_one(t, c):
        place(t)
        return c

    assert n_slots % clear_unroll == 0
    lax.fori_loop(0, n_slots // clear_unroll, clear, 0)
    n_groups = n_valid // DMA_ISSUE_UNROLL
    lax.fori_loop(0, n_groups, place_group, 0)
    lax.fori_loop(n_groups * DMA_ISSUE_UNROLL, n_valid, place_one, 0)


def _slot_tokens(pos1, pos2, n_valid, n_slots):
    smem = pl.BlockSpec(memory_space=pltpu.SMEM)
    return pl.pallas_call(
        functools.partial(_slot_token_body, n_valid=n_valid, n_slots=n_slots),
        in_specs=[smem, smem],
        out_specs=smem,
        out_shape=jax.ShapeDtypeStruct((n_slots,), I32),
        name="moe_slot_tokens",
    )(pos1, pos2)


def _moe_plan(route, n_valid, n_exp):
    m_tot = route.shape[0]
    tm, rt = TOKEN_TILE, FFN_SUPER_TILES
    n_tiles_max = (MOE_TOP_K * n_valid + n_exp * (tm - 1)) // tm
    s_rows = n_tiles_max * tm
    n_super = n_tiles_max // rt + n_exp
    e1, e2 = route[:, 0].astype(I32), route[:, 1].astype(I32)
    tok = jnp.arange(m_tot, dtype=I32)
    valid = tok < n_valid
    ids = jnp.arange(n_exp, dtype=I32)[None, :]
    oh1 = (e1[:, None] == ids) & valid[:, None]
    oh2 = (e2[:, None] == ids) & valid[:, None]
    oh = oh1.astype(I32) + oh2.astype(I32)
    csum = jnp.cumsum(oh, axis=0)
    before = csum - oh
    counts = csum[-1]
    tiles_e = (counts + tm - 1) // tm
    tile_end = jnp.cumsum(tiles_e)
    tile_off = tile_end - tiles_e
    row_off = tile_off * tm
    pos1 = jnp.sum(jnp.where(oh1, before + row_off[None, :], 0), axis=1)
    pos2 = jnp.sum(jnp.where(oh2, before + row_off[None, :], 0), axis=1)
    slot_token = _slot_tokens(pos1, pos2, n_valid, s_rows)
    n_used_tiles = tile_end[-1:]
    super_e = -(-tiles_e // rt)
    super_end = jnp.cumsum(super_e)
    super_off = super_end - super_e
    n_used_super = super_end[-1]
    sidx = jnp.arange(n_super, dtype=I32)
    live = sidx < n_used_super
    sclamp = jnp.minimum(sidx, n_used_super - 1)
    exp_of = jnp.sum((sclamp[:, None] >= super_end[None, :]).astype(I32), axis=1)
    k = sclamp - super_off[exp_of]
    sched_start = (tile_off[exp_of] + k * rt) * tm
    sched_nt = jnp.where(live, jnp.clip(tiles_e[exp_of] - k * rt, 0, rt), 0)
    return pos1, pos2, slot_token, n_used_tiles, (exp_of, sched_start, sched_nt), s_rows


def _gather_body(tok_ref, nused_ref, x_hbm, o_ref, buf, sem, *, tm):
    t = pl.program_id(0)
    n_used = nused_ref[0]

    def issue_tile(tile, slot):
        def issue(g, c):
            for r in range(DMA_ISSUE_UNROLL):
                row = g * DMA_ISSUE_UNROLL + r
                pltpu.make_async_copy(x_hbm.at[pl.ds(tok_ref[tile * tm + row], 1), :],
                                      buf.at[slot, pl.ds(row, 1), :], sem.at[slot]).start()
            return c

        lax.fori_loop(0, tm // DMA_ISSUE_UNROLL, issue, 0)

    @pl.when((t == 0) & (n_used > 0))
    def _first():
        issue_tile(0, 0)

    @pl.when(t + 1 < n_used)
    def _prefetch():
        issue_tile(t + 1, (t + 1) % 2)

    @pl.when(t < n_used)
    def _used():
        slot = t % 2
        pltpu.make_async_copy(x_hbm.at[pl.ds(0, tm), :], buf.at[slot], sem.at[slot]).wait()
        o_ref[...] = buf[slot].astype(o_ref.dtype)

    @pl.when(t >= n_used)
    def _unused():
        o_ref[...] = jnp.zeros(o_ref.shape, o_ref.dtype)


def _gather_rows(x, slot_token, n_used_tiles):
    s_rows = slot_token.shape[0]
    d = x.shape[1]
    tm = TOKEN_TILE
    return pl.pallas_call(
        functools.partial(_gather_body, tm=tm),
        grid_spec=pltpu.PrefetchScalarGridSpec(
            num_scalar_prefetch=2,
            grid=(s_rows // tm,),
            in_specs=[pl.BlockSpec(memory_space=pl.ANY)],
            out_specs=pl.BlockSpec((tm, d), lambda t, tok, nu: (t, 0)),
            scratch_shapes=[pltpu.VMEM((2, tm, d), F32), pltpu.SemaphoreType.DMA((2,))],
        ),
        out_shape=jax.ShapeDtypeStruct((s_rows, d), BF16),
        compiler_params=_params(("arbitrary",), 32),
        name="moe_gather",
    )(slot_token, n_used_tiles, x)


def _combine_body(p1_ref, p2_ref, x_ref, route_ref, g_ref, b_ref, y_hbm, o_ref, ob_ref, buf, sem,
                  *, tm, alpha, n_prompt_tiles):
    t = pl.program_id(0)

    def issue_tile(tile, slot):
        def issue(g, c):
            for r in range(DMA_ISSUE_UNROLL):
                row = g * DMA_ISSUE_UNROLL + r
                for which, p_ref in enumerate((p1_ref, p2_ref)):
                    pltpu.make_async_copy(y_hbm.at[pl.ds(p_ref[tile * tm + row], 1), :],
                                          buf.at[slot, which, pl.ds(row, 1), :], sem.at[slot]).start()
            return c

        lax.fori_loop(0, tm // DMA_ISSUE_UNROLL, issue, 0)

    @pl.when(t == 0)
    def _first():
        issue_tile(0, 0)

    @pl.when(t + 1 < pl.num_programs(0))
    def _prefetch():
        issue_tile(t + 1, (t + 1) % 2)

    slot = t % 2
    for which in range(2):
        pltpu.make_async_copy(y_hbm.at[pl.ds(0, tm), :], buf.at[slot, which], sem.at[slot]).wait()
    route = route_ref[...]
    y = route[:, 2:3] * buf[slot, 0] + route[:, 3:4] * buf[slot, 1]
    out = _layer_norm(alpha * x_ref[...] + y, g_ref[...], b_ref[...])
    if n_prompt_tiles is None:
        o_ref[...] = out
        ob_ref[...] = out.astype(BF16)
    else:
        @pl.when(t < n_prompt_tiles)
        def _prompt():
            o_ref[...] = out

        @pl.when(t == n_prompt_tiles)
        def _sample():
            ob_ref[...] = out


def _combine_ln(x, route, y_slots, pos1, pos2, g_stack, b_stack, layer, alpha, n_prompt_rows=None):
    m, d = x.shape
    tm = TOKEN_TILE
    row = pl.BlockSpec((tm, d), lambda t, p1, p2: (t, 0))
    par = pl.BlockSpec((None, 1, d), lambda t, p1, p2: (layer, 0, 0))
    if n_prompt_rows is None:
        npt = None
        out_specs = [row, row]
        out_shape = [jax.ShapeDtypeStruct((m, d), F32), jax.ShapeDtypeStruct((m, d), BF16)]
    else:
        npt = n_prompt_rows // tm
        assert n_prompt_rows % tm == 0 and m == n_prompt_rows + tm
        out_specs = [pl.BlockSpec((tm, d), lambda t, p1, p2: (jnp.minimum(t, npt - 1), 0)),
                     pl.BlockSpec((tm, d), lambda t, p1, p2: (0, 0))]
        out_shape = [jax.ShapeDtypeStruct((n_prompt_rows, d), F32), jax.ShapeDtypeStruct((tm, d), F32)]
    return pl.pallas_call(
        functools.partial(_combine_body, tm=tm, alpha=alpha, n_prompt_tiles=npt),
        grid_spec=pltpu.PrefetchScalarGridSpec(
            num_scalar_prefetch=2,
            grid=(m // tm,),
            in_specs=[row, pl.BlockSpec((tm, LANES), lambda t, p1, p2: (t, 0)), par, par,
                      pl.BlockSpec(memory_space=pl.ANY)],
            out_specs=out_specs,
            scratch_shapes=[pltpu.VMEM((2, 2, tm, d), F32), pltpu.SemaphoreType.DMA((2,))],
        ),
        out_shape=out_shape,
        compiler_params=_params(("arbitrary",), 40),
        name="moe_combine_ln",
    )(pos1, pos2, x, route, g_stack.reshape(-1, 1, d), b_stack.reshape(-1, 1, d), y_slots)


def kernel(x_prompt, x_sample, cache_k, cache_v, page_table, state_conv, state_lru_conv, state_lru_h, ln1_g, ln1_b, ln2_g, ln2_b, sc_w_in, sc_w_conv, sc_w_out, attn_w_qkv, attn_w_o, lru_w_in, lru_w_conv, lru_b_conv, lru_w_gate_a, lru_b_gate_a, lru_w_gate_x, lru_b_gate_x, lru_lambda, lru_w_out, ffn_w_gate, ffn_w_up, ffn_w_down, moe_w_router, moe_w_gate, moe_w_up, moe_w_down):
    bp, seq, d = x_prompt.shape
    bs, dec_seq, _ = x_sample.shape
    depth = ln1_g.shape[0]
    n_heads, hd = cache_k.shape[3], cache_k.shape[4]
    page = cache_k.shape[2]
    past_len = page_table.shape[1] * page
    n_exp = moe_w_router.shape[2]
    assert dec_seq == 1 and bs <= TOKEN_TILE
    assert seq % MOBA_BLOCK == 0 and seq % TOKEN_TILE == 0 and past_len % MOBA_BLOCK == 0
    assert MOBA_BLOCK % page == 0 and past_len // MOBA_BLOCK >= MOBA_TOPK
    alpha = float((2.0 * depth) ** 0.25)
    mp = bp * seq
    n_valid = mp + bs
    m_tot = -(-n_valid // TOKEN_TILE) * TOKEN_TILE
    ppb = MOBA_BLOCK // page

    x = jnp.concatenate([x_prompt.reshape(mp, d), x_sample.reshape(bs, d),
                         jnp.zeros((m_tot - n_valid, d), x_prompt.dtype)], axis=0)
    xb = x.astype(BF16)
    dense_sched = _dense_schedule(m_tot // TOKEN_TILE)

    conv_p, conv_s, kp_l, vp_l, ks_l, vs_l, lc_p, lc_s, lh_p, lh_s = ([] for _ in range(10))
    final = None
    for i in range(depth):
        kind, m = i % 3, i // 3
        if kind == 0:
            proj = _matmul(xb, sc_w_in, m)
            y, new_p, u_s = _short_conv(proj, sc_w_conv, m, state_conv[m, :, 0], state_conv[m, :, 1],
                                        bp=bp, seq=seq, bs=bs)
            conv_p.append(new_p)
            conv_s.append(jnp.stack([state_conv[m, :, 1], u_s], axis=1))
            mixed, w_out = y, sc_w_out
        elif kind == 1:
            qkv = _matmul(xb, attn_w_qkv, m)
            q, k_p, v_p, k_s, v_s = _rope(qkv, n_heads=n_heads, hd=hd, bp=bp, seq=seq, bs=bs, past_len=past_len)
            kp_l.append(k_p.reshape(bp, seq, n_heads, hd))
            vp_l.append(v_p.reshape(bp, seq, n_heads, hd))
            ks_l.append(k_s.reshape(bs, 1, n_heads, hd))
            vs_l.append(v_s.reshape(bs, 1, n_heads, hd))
            o_p = _attn_prompt(q, k_p, v_p, bp=bp, seq=seq, n_heads=n_heads, hd=hd)
            q_s = q[mp:mp + bs]
            kmean = _page_means(cache_k, m, page_table)
            sel = _select_blocks(kmean.transpose(0, 2, 1, 3), q_s)
            sel = sel[:, :MOBA_TOPK, :n_heads].transpose(0, 2, 1)
            logical = sel[..., None] * ppb + jnp.arange(ppb, dtype=I32)
            phys = jnp.take_along_axis(page_table, logical.reshape(bs, -1), axis=1)
            o_s = _decode_attn(q_s, k_s, v_s, cache_k, cache_v, m, phys.reshape(bs, n_heads, -1),
                               n_heads=n_heads, hd=hd)
            tail = jnp.concatenate([o_s.astype(BF16), jnp.zeros((m_tot - n_valid, n_heads * hd), BF16)], axis=0)
            mixed, w_out = o_p, attn_w_o
        else:
            proj = _matmul(xb, lru_w_in, m)
            y, new_c, new_h, u_s, h_s = _rglru(proj, lru_w_conv, lru_b_conv, lru_w_gate_a, lru_b_gate_a,
                                               lru_w_gate_x, lru_b_gate_x, lru_lambda, m,
                                               state_lru_conv[m], state_lru_h[m], bp=bp, seq=seq, bs=bs)
            lc_p.append(new_c)
            lc_s.append(jnp.concatenate([state_lru_conv[m, :, 1:], u_s[:, None, :]], axis=1))
            lh_p.append(new_h.reshape(bp, -1))
            lh_s.append(h_s)
            mixed, w_out = y, lru_w_out
        y_tail = tail if kind == 1 else None
        if i % 2 == 0:
            x, xb = _matmul_add_ln(mixed, w_out, m, x, ln1_g, ln1_b, i, alpha, y_tail=y_tail)
        else:
            x, xb, route = _matmul_add_ln(mixed, w_out, m, x, ln1_g, ln1_b, i, alpha,
                                          w_router=moe_w_router, router_layer=i // 2, y_tail=y_tail)
        j = i // 2
        if i % 2 == 0:
            f = _ffn(xb, ffn_w_gate[:, None], ffn_w_up[:, None], ffn_w_down[:, None], j, *dense_sched)
            x, xb = _add_ln(x, f, ln2_g, ln2_b, i, alpha)
        else:
            pos1, pos2, slot_token, n_used, sched, _ = _moe_plan(route, n_valid, n_exp)
            xs = _gather_rows(x, slot_token, n_used)
            ys = _ffn(xs, moe_w_gate, moe_w_up, moe_w_down, j, *sched, n_used)
            if i == depth - 1 and m_tot == mp + TOKEN_TILE:
                final = _combine_ln(x, route, ys, pos1, pos2, ln2_g, ln2_b, i, alpha, n_prompt_rows=mp)
            else:
                x, xb = _combine_ln(x, route, ys, pos1, pos2, ln2_g, ln2_b, i, alpha)
    x_p, x_s = final if final is not None else (x[:mp], x[mp:])
    return (x_p.reshape(bp, seq, d), x_s[:bs].reshape(bs, 1, d),
            jnp.stack(conv_p), jnp.stack(conv_s),
            jnp.stack(kp_l), jnp.stack(vp_l), jnp.stack(ks_l), jnp.stack(vs_l),
            jnp.stack(lc_p), jnp.stack(lc_s), jnp.stack(lh_p), jnp.stack(lh_s))
```

```python
import functools

import numpy as np
import jax
import jax.numpy as jnp
from jax import lax
from jax.experimental import pallas as pl
from jax.experimental.pallas import tpu as pltpu

F32, BF16, I32 = jnp.float32, jnp.bfloat16, jnp.int32

MOBA_BLOCK = 256
MOBA_TOPK = 3
ROPE_THETA = 500000.0
ROT_FRACTION = 4
LRU_C = 8.0
MOE_TOP_K = 2
LN_EPS = 1e-5
NEG_INF = -1e30
LOG2_E = 1.4426950408889634

LANES = 128
SUBLANES = 8
V7X_VMEM_BYTES = 64 * 1024 * 1024

TOKEN_TILE = 256
FFN_COL_TILE = 256
FFN_SUPER_TILES = 10
FFN_STEP_TILES = 4
DMA_ISSUE_UNROLL = 8


def _params(sem, vmem_mib, **kw):
    return pltpu.CompilerParams(dimension_semantics=sem, vmem_limit_bytes=vmem_mib * 1024 * 1024, **kw)


def _pick(total, candidates):
    for c in candidates:
        if total % c == 0:
            return c
    raise ValueError(f"no tile in {candidates} divides {total}")


def _mm_body(x_ref, w_ref, o_ref, wb_ref):
    @pl.when(pl.program_id(1) == 0)
    def _cast():
        wb_ref[...] = w_ref[...].astype(BF16)

    o_ref[...] = jnp.dot(x_ref[...], wb_ref[...], preferred_element_type=F32).astype(o_ref.dtype)


def _matmul(xb, w_stack, layer, out_dtype=F32):
    m, k = xb.shape
    n = w_stack.shape[2]
    tm = _pick(m, (768, 512, 256))
    tn = _pick(n, (1024, 512, 256, 128))
    return pl.pallas_call(
        _mm_body,
        grid=(n // tn, m // tm),
        in_specs=[pl.BlockSpec((tm, k), lambda j, i: (i, 0)),
                  pl.BlockSpec((None, k, tn), lambda j, i: (layer, 0, j))],
        out_specs=pl.BlockSpec((tm, tn), lambda j, i: (i, j)),
        out_shape=jax.ShapeDtypeStruct((m, n), out_dtype),
        scratch_shapes=[pltpu.VMEM((k, tn), BF16)],
        compiler_params=_params(("arbitrary", "arbitrary"), 48),
        name="matmul",
    )(xb, w_stack)


def _layer_norm(z, g, b):
    mu = jnp.mean(z, axis=-1, keepdims=True)
    zc = z - mu
    var = jnp.mean(zc * zc, axis=-1, keepdims=True)
    return zc * lax.rsqrt(var + LN_EPS) * g + b


def _route_top2(x, w, n_exp):
    xh, wh = x.astype(BF16), w.astype(BF16)
    xl, wl = (x - xh.astype(F32)).astype(BF16), (w - wh.astype(F32)).astype(BF16)
    dot = functools.partial(jnp.dot, preferred_element_type=F32)
    logits = dot(xh, wh) + (dot(xh, wl) + dot(xl, wh))
    lane = lax.broadcasted_iota(I32, logits.shape, 1)
    lg = jnp.where(lane < n_exp, logits, -jnp.inf)
    m1 = jnp.max(lg, axis=1, keepdims=True)
    i1 = jnp.min(jnp.where(lg == m1, lane, LANES), axis=1, keepdims=True)
    lg = jnp.where(lane == i1, -jnp.inf, lg)
    m2 = jnp.max(lg, axis=1, keepdims=True)
    i2 = jnp.min(jnp.where(lg == m2, lane, LANES), axis=1, keepdims=True)
    e = jnp.exp(m2 - m1)
    g1 = 1.0 / (1.0 + e)
    g2 = e / (1.0 + e)
    return jnp.where(lane == 0, i1.astype(F32),
                     jnp.where(lane == 1, i2.astype(F32),
                               jnp.where(lane == 2, g1, jnp.where(lane == 3, g2, 0.0))))


def _mm_ln_body(y_ref, w_ref, x_ref, g_ref, b_ref, *rest, alpha, n_exp, n_head_tiles):
    rest = list(rest)
    wr_ref = rest.pop(0) if n_exp is not None else None
    tail_ref = rest.pop(0) if n_head_tiles is not None else None
    o_ref, ob_ref = rest.pop(0), rest.pop(0)
    route_ref = rest.pop(0) if n_exp is not None else None
    (wb_ref,) = rest

    @pl.when(pl.program_id(0) == 0)
    def _cast():
        wb_ref[...] = w_ref[...].astype(BF16)

    y = y_ref[...]
    if tail_ref is not None:
        y = jnp.where(pl.program_id(0) == n_head_tiles, tail_ref[...], y)
    mix = jnp.dot(y, wb_ref[...], preferred_element_type=F32)
    out = _layer_norm(alpha * x_ref[...] + mix, g_ref[...], b_ref[...])
    o_ref[...] = out
    ob_ref[...] = out.astype(BF16)
    if n_exp is not None:
        route_ref[...] = _route_top2(out, wr_ref[...], n_exp)


def _matmul_add_ln(yb, w_stack, w_layer, x, g_stack, b_stack, layer, alpha, w_router=None, router_layer=None,
                   y_tail=None):
    m, d = x.shape
    k = yb.shape[1]
    assert w_stack.shape[2] == d
    tm = TOKEN_TILE
    row = lambda width: pl.BlockSpec((tm, width), lambda i: (i, 0))
    par = pl.BlockSpec((None, 1, d), lambda i: (layer, 0, 0))
    n_head_tiles = None
    y_spec = row(k)
    if y_tail is not None:
        n_head_tiles = yb.shape[0] // tm
        assert yb.shape[0] == n_head_tiles * tm and m == yb.shape[0] + tm and y_tail.shape == (tm, k)
        y_spec = pl.BlockSpec((tm, k), lambda i: (jnp.minimum(i, n_head_tiles - 1), 0))
    in_specs = [y_spec,
                pl.BlockSpec((None, k, d), lambda i: (w_layer, 0, 0), pipeline_mode=pl.Buffered(1)),
                row(d), par, par]
    out_specs = [row(d), row(d)]
    out_shape = [jax.ShapeDtypeStruct((m, d), F32), jax.ShapeDtypeStruct((m, d), BF16)]
    args = [yb, w_stack, x, g_stack.reshape(-1, 1, d), b_stack.reshape(-1, 1, d)]
    n_exp = None
    if w_router is not None:
        n_exp = w_router.shape[2]
        in_specs.append(pl.BlockSpec((None, d, LANES), lambda i: (router_layer, 0, 0)))
        out_specs.append(row(LANES))
        out_shape.append(jax.ShapeDtypeStruct((m, LANES), F32))
        args.append(jnp.pad(w_router, ((0, 0), (0, 0), (0, LANES - n_exp))))
    if y_tail is not None:
        in_specs.append(pl.BlockSpec((tm, k), lambda i: (0, 0)))
        args.append(y_tail)
    return pl.pallas_call(
        functools.partial(_mm_ln_body, alpha=alpha, n_exp=n_exp, n_head_tiles=n_head_tiles),
        grid=(m // tm,),
        in_specs=in_specs,
        out_specs=out_specs,
        out_shape=out_shape,
        scratch_shapes=[pltpu.VMEM((k, d), BF16)],
        compiler_params=_params(("arbitrary",), 48),
        name="matmul_add_ln",
    )(*args)


def _add_ln_body(x_ref, y_ref, g_ref, b_ref, o_ref, ob_ref, *, alpha):
    out = _layer_norm(alpha * x_ref[...] + y_ref[...], g_ref[...], b_ref[...])
    o_ref[...] = out
    ob_ref[...] = out.astype(BF16)


def _add_ln(x, y, g_stack, b_stack, layer, alpha):
    m, d = x.shape
    tm = TOKEN_TILE
    row = pl.BlockSpec((tm, d), lambda i: (i, 0))
    par = pl.BlockSpec((None, 1, d), lambda i: (layer, 0, 0))
    return pl.pallas_call(
        functools.partial(_add_ln_body, alpha=alpha),
        grid=(m // tm,),
        in_specs=[row, row, par, par],
        out_specs=[row, row],
        out_shape=[jax.ShapeDtypeStruct((m, d), F32), jax.ShapeDtypeStruct((m, d), BF16)],
        compiler_params=_params(("arbitrary",), 32),
        name="add_ln",
    )(x, y, g_stack.reshape(-1, 1, d), b_stack.reshape(-1, 1, d))


def _shift_rows(u, k, carry_rows):
    out = pltpu.roll(u, k, 0)
    row = lax.broadcasted_iota(I32, u.shape, 0)
    for r, c in enumerate(carry_rows):
        out = jnp.where(row == r, c, out)
    return out


def _sconv_body(b_ref, c_ref, h_ref, w_ref, s0_ref, s1_ref, y_ref, newp_ref, news_ref, carry_ref,
                *, tiles_per_seq, n_prompt_tiles, bs):
    i = pl.program_id(0)
    w0, w1, w2 = w_ref[0:1, :], w_ref[1:2, :], w_ref[2:3, :]
    tm = y_ref.shape[0]

    @pl.when(i < n_prompt_tiles)
    def _prompt():
        u = c_ref[...] * h_ref[...]
        fresh = i % tiles_per_seq == 0
        c0 = jnp.where(fresh, 0.0, carry_ref[0:1, :])
        c1 = jnp.where(fresh, 0.0, carry_ref[1:2, :])
        p1 = _shift_rows(u, 1, (c1,))
        p2 = _shift_rows(u, 2, (c0, c1))
        y = w0 * p2 + w1 * p1 + w2 * u
        y_ref[...] = (b_ref[...] * y).astype(y_ref.dtype)
        carry_ref[0:2, :] = u[tm - 2:tm, :]
        newp_ref[...] = u[tm - 2:tm, :]

    @pl.when(i == n_prompt_tiles)
    def _sample():
        u = c_ref[0:bs, :] * h_ref[0:bs, :]
        y = w0 * s0_ref[...] + w1 * s1_ref[...] + w2 * u
        y_ref[...] = jnp.zeros(y_ref.shape, y_ref.dtype)
        y_ref[0:bs, :] = (b_ref[0:bs, :] * y).astype(y_ref.dtype)
        news_ref[...] = u


def _short_conv(proj, w_conv, layer, s0, s1, *, bp, seq, bs):
    m_tot = proj.shape[0]
    d = proj.shape[1] // 3
    tm = TOKEN_TILE
    tps = seq // tm
    npt = bp * tps
    col = lambda c: pl.BlockSpec((tm, d), lambda i: (i, c))
    full = lambda shape: pl.BlockSpec(shape, lambda i: (0,) * len(shape))
    return pl.pallas_call(
        functools.partial(_sconv_body, tiles_per_seq=tps, n_prompt_tiles=npt, bs=bs),
        grid=(m_tot // tm,),
        in_specs=[col(0), col(1), col(2),
                  pl.BlockSpec((None, w_conv.shape[1], d), lambda i: (layer, 0, 0)),
                  full((bs, d)), full((bs, d))],
        out_specs=[pl.BlockSpec((tm, d), lambda i: (i, 0)),
                   pl.BlockSpec((None, 2, d), lambda i: (jnp.minimum(i // tps, bp - 1), 0, 0)),
                   full((bs, d))],
        out_shape=[jax.ShapeDtypeStruct((m_tot, d), BF16),
                   jax.ShapeDtypeStruct((bp, 2, d), F32),
                   jax.ShapeDtypeStruct((bs, d), F32)],
        scratch_shapes=[pltpu.VMEM((SUBLANES, d), F32)],
        compiler_params=_params(("arbitrary",), 40),
        name="short_conv",
    )(proj, proj, proj, w_conv, s0, s1)


def _rope_head(x, cos_f, sin_f, half):
    lane = lax.broadcasted_iota(I32, x.shape, 1)
    partner = jnp.where(lane < half, pltpu.roll(x, LANES - half, 1), pltpu.roll(x, half, 1))
    return x * cos_f + partner * sin_f


def _rope_body(q_ref, k_ref, v_ref, cos_ref, sin_ref, cos_s_ref, sin_s_ref,
               qo_ref, kp_ref, vp_ref, ks_ref, vs_ref, *, n_heads, hd, half, n_prompt_tiles, bs):
    i = pl.program_id(0)

    def rotate(x_ref, rows, cos_f, sin_f):
        return [
            _rope_head(x_ref[rows, h * hd:(h + 1) * hd], cos_f, sin_f, half) for h in range(n_heads)
        ]

    @pl.when(i < n_prompt_tiles)
    def _prompt():
        rows = slice(None)
        cos_f, sin_f = cos_ref[...], sin_ref[...]
        for h, (qh, kh) in enumerate(zip(rotate(q_ref, rows, cos_f, sin_f), rotate(k_ref, rows, cos_f, sin_f))):
            qo_ref[:, h * hd:(h + 1) * hd] = qh
            kp_ref[:, h * hd:(h + 1) * hd] = kh
        vp_ref[...] = v_ref[...]

    @pl.when(i == n_prompt_tiles)
    def _sample():
        rows = slice(0, bs)
        cos_f, sin_f = cos_s_ref[...], sin_s_ref[...]
        qo_ref[...] = jnp.zeros(qo_ref.shape, qo_ref.dtype)
        for h, (qh, kh) in enumerate(zip(rotate(q_ref, rows, cos_f, sin_f), rotate(k_ref, rows, cos_f, sin_f))):
            qo_ref[0:bs, h * hd:(h + 1) * hd] = qh
            ks_ref[:, h * hd:(h + 1) * hd] = kh
        vs_ref[...] = v_ref[0:bs, :]


def _rope_tables(pos, hd):
    rot = hd // ROT_FRACTION
    half = rot // 2
    inv_freq = ROPE_THETA ** (-jnp.arange(half, dtype=F32) * 2.0 / rot)
    ang = pos.astype(F32)[:, None] * inv_freq[None, :]
    cos, sin = jnp.cos(ang), jnp.sin(ang)
    n = pos.shape[0]
    cos_f = jnp.concatenate([cos, cos, jnp.ones((n, hd - rot), F32)], axis=1)
    sin_f = jnp.concatenate([-sin, sin, jnp.zeros((n, hd - rot), F32)], axis=1)
    return cos_f, sin_f


def _rope(qkv, *, n_heads, hd, bp, seq, bs, past_len):
    m_tot = qkv.shape[0]
    hdim = n_heads * hd
    tm = TOKEN_TILE
    tps = seq // tm
    npt = bp * tps
    mp = bp * seq
    cos_p, sin_p = _rope_tables(jnp.arange(seq, dtype=I32), hd)
    cos_s, sin_s = _rope_tables(jnp.full((1,), past_len, I32), hd)
    col = lambda c: pl.BlockSpec((tm, hdim), lambda i: (i, c))
    tab = pl.BlockSpec((tm, hd), lambda i: (i % tps, 0))
    one = pl.BlockSpec((1, hd), lambda i: (0, 0))
    prow = pl.BlockSpec((tm, hdim), lambda i: (jnp.minimum(i, npt - 1), 0))
    srow = pl.BlockSpec((bs, hdim), lambda i: (0, 0))
    return pl.pallas_call(
        functools.partial(_rope_body, n_heads=n_heads, hd=hd, half=hd // ROT_FRACTION // 2,
                          n_prompt_tiles=npt, bs=bs),
        grid=(m_tot // tm,),
        in_specs=[col(0), col(1), col(2), tab, tab, one, one],
        out_specs=[pl.BlockSpec((tm, hdim), lambda i: (i, 0)), prow, prow, srow, srow],
        out_shape=[jax.ShapeDtypeStruct((m_tot, hdim), F32),
                   jax.ShapeDtypeStruct((mp, hdim), F32), jax.ShapeDtypeStruct((mp, hdim), F32),
                   jax.ShapeDtypeStruct((bs, hdim), F32), jax.ShapeDtypeStruct((bs, hdim), F32)],
        compiler_params=_params(("arbitrary",), 48),
        name="rope",
    )(qkv, qkv, qkv, cos_p, sin_p, cos_s, sin_s)


def _nt_dot(a, b, **kw):
    return lax.dot_general(a, b, (((1,), (1,)), ((), ())), preferred_element_type=F32, **kw)


def _attn_body(q_ref, k_ref, v_ref, o_ref, kmean_ref, kb_ref, vt_ref, *, nb, scale):
    blk = MOBA_BLOCK
    kmean_ref[...] = jnp.zeros(kmean_ref.shape, F32)
    for n in range(nb):
        rows = slice(n * blk, (n + 1) * blk)
        kmean_ref[n:n + 1, :] = jnp.sum(k_ref[rows, :], axis=0, keepdims=True) * (1.0 / blk)
        vt_ref[:, rows] = v_ref[rows, :].T.astype(BF16)
    kb_ref[...] = k_ref[...].astype(BF16)
    gate_all = _nt_dot(kmean_ref[...], q_ref[...], precision=lax.Precision.HIGHEST)
    bid = lax.broadcasted_iota(I32, (kmean_ref.shape[0], blk), 0)
    ki = lax.broadcasted_iota(I32, (blk, blk), 0)
    qi = lax.broadcasted_iota(I32, (blk, blk), 1)

    for j in range(nb):
        cols = slice(j * blk, (j + 1) * blk)
        gate = jnp.where(bid < j, gate_all[:, cols], NEG_INF)
        rank = jnp.zeros(gate.shape, I32)
        for m in range(j):
            gm = gate[m:m + 1, :]
            beats = (gm > gate) | ((gm == gate) & (m < bid))
            rank = rank + jnp.where(beats, 1, 0)
        sel = jnp.where((bid < j) & (rank < MOBA_TOPK), 1.0, 0.0)
        keys = (j + 1) * blk
        s = _nt_dot(kb_ref[0:keys, :], q_ref[cols, :].astype(BF16)) * (scale * LOG2_E)
        slabs = [jnp.where(sel[n:n + 1, :] > 0.0, s[n * blk:(n + 1) * blk, :], NEG_INF) for n in range(j)]
        slabs.append(jnp.where(ki <= qi, s[j * blk:keys, :], NEG_INF))
        top = slabs[0]
        for sl in slabs[1:]:
            top = jnp.maximum(top, sl)
        top = jnp.max(top, axis=0, keepdims=True)
        p = [jnp.exp2(sl - top) for sl in slabs]
        tot = p[0]
        for pn in p[1:]:
            tot = tot + pn
        denom = jnp.sum(tot, axis=0, keepdims=True)
        pv = jnp.dot(vt_ref[:, 0:keys], jnp.concatenate(p, axis=0).astype(BF16),
                     preferred_element_type=F32)
        o_ref[cols, :] = (pv / denom).T.astype(o_ref.dtype)


def _attn_prompt(q, k_p, v_p, *, bp, seq, n_heads, hd):
    nb = seq // MOBA_BLOCK
    nbp = -(-nb // SUBLANES) * SUBLANES
    blk = pl.BlockSpec((seq, hd), lambda b, h: (b, h))
    return pl.pallas_call(
        functools.partial(_attn_body, nb=nb, scale=hd ** -0.5),
        grid=(bp, n_heads),
        in_specs=[blk, blk, blk],
        out_specs=blk,
        out_shape=jax.ShapeDtypeStruct((bp * seq, n_heads * hd), BF16),
        scratch_shapes=[pltpu.VMEM((nbp, hd), F32), pltpu.VMEM((seq, hd), BF16), pltpu.VMEM((hd, seq), BF16)],
        compiler_params=_params(("arbitrary", "arbitrary"), 48),
        name="moba_prompt",
    )(q, k_p, v_p)


def _page_mean_body(pt_ref, *refs, pages_per_step, pages_per_block):
    del pt_ref
    pages, o_ref = refs[:pages_per_step], refs[pages_per_step]
    inv = 1.0 / MOBA_BLOCK
    for c in range(pages_per_step // pages_per_block):
        tot = jnp.sum(pages[c * pages_per_block][...], axis=0)
        for r in range(1, pages_per_block):
            tot = tot + jnp.sum(pages[c * pages_per_block + r][...], axis=0)
        o_ref[c] = tot * inv


def _page_means(cache_k, layer, page_table):
    _, _, page, n_heads, hd = cache_k.shape
    bs, n_pages = page_table.shape
    ppb = MOBA_BLOCK // page
    pps = _pick(n_pages, (16, 8, 4, 2)) if ppb == 2 else ppb
    n_blocks = n_pages // ppb

    def page_spec(c):
        return pl.BlockSpec((None, None, page, n_heads, hd),
                            lambda b, s, pt: (layer, pt[b * n_pages + s * pps + c], 0, 0, 0))

    return pl.pallas_call(
        functools.partial(_page_mean_body, pages_per_step=pps, pages_per_block=ppb),
        grid_spec=pltpu.PrefetchScalarGridSpec(
            num_scalar_prefetch=1,
            grid=(bs, n_pages // pps),
            in_specs=[page_spec(c) for c in range(pps)],
            out_specs=pl.BlockSpec((None, pps // ppb, n_heads, hd), lambda b, s, pt: (b, s, 0, 0)),
        ),
        out_shape=jax.ShapeDtypeStruct((bs, n_blocks, n_heads, hd), F32),
        compiler_params=_params(("arbitrary", "arbitrary"), 48),
        name="page_means",
    )(page_table.reshape(-1), *([cache_k] * pps))


def _select_body(km_ref, q_ref, o_ref, *, n_heads):
    n_blocks = km_ref.shape[1]
    res = jnp.zeros(o_ref.shape, I32)
    row = lax.broadcasted_iota(I32, o_ref.shape, 0)
    lane = lax.broadcasted_iota(I32, o_ref.shape, 1)
    blk_id = lax.broadcasted_iota(I32, (n_blocks, 1), 0)
    for h in range(n_heads):
        gate = jnp.sum(km_ref[h] * q_ref[h:h + 1, :], axis=1, keepdims=True)
        for k in range(MOBA_TOPK):
            best = jnp.max(gate, axis=0, keepdims=True)
            idx = jnp.min(jnp.where(gate == best, blk_id, n_blocks), axis=0, keepdims=True)
            res = jnp.where((row == k) & (lane == h), idx, res)
            gate = jnp.where(blk_id == idx, -jnp.inf, gate)
    o_ref[...] = res


def _select_blocks(kmean_t, q_s):
    bs, n_heads, n_blocks, hd = kmean_t.shape
    return pl.pallas_call(
        functools.partial(_select_body, n_heads=n_heads),
        grid=(bs,),
        in_specs=[pl.BlockSpec((None, n_heads, n_blocks, hd), lambda b: (b, 0, 0, 0)),
                  pl.BlockSpec((None, n_heads, hd), lambda b: (b, 0, 0))],
        out_specs=pl.BlockSpec((None, SUBLANES, LANES), lambda b: (b, 0, 0)),
        out_shape=jax.ShapeDtypeStruct((bs, SUBLANES, LANES), I32),
        compiler_params=_params(("arbitrary",), 32),
        name="moba_select",
    )(kmean_t, q_s.reshape(bs, n_heads, hd))


def _decode_attn_body(ph_ref, q_ref, kn_ref, vn_ref, ck_hbm, cv_hbm, o_ref, kbuf, vbuf, sem,
                      *, layer, n_heads, hd, n_sel_pages, scale):
    b = pl.program_id(0)

    def start_copies(tok, slot):
        for h in range(n_heads):
            for r in range(n_sel_pages):
                pg = ph_ref[(tok * n_heads + h) * n_sel_pages + r]
                pltpu.make_async_copy(ck_hbm.at[layer, pg, :, h, :], kbuf.at[slot, h, r], sem.at[slot]).start()
                pltpu.make_async_copy(cv_hbm.at[layer, pg, :, h, :], vbuf.at[slot, h, r], sem.at[slot]).start()

    @pl.when(b == 0)
    def _first():
        start_copies(b, 0)

    @pl.when(b + 1 < pl.num_programs(0))
    def _prefetch():
        start_copies(b + 1, (b + 1) % 2)

    slot = b % 2
    pltpu.make_async_copy(kbuf.at[1 - slot], kbuf.at[slot], sem.at[slot]).wait()
    pltpu.make_async_copy(vbuf.at[1 - slot], vbuf.at[slot], sem.at[slot]).wait()

    row = pl.ds(b, 1)
    q_row, kn_row, vn_row = q_ref[row, :], kn_ref[row, :], vn_ref[row, :]
    out = []
    for h in range(n_heads):
        cols = slice(h * hd, (h + 1) * hd)
        q = q_row[:, cols]
        s_new = jnp.sum(q * kn_row[:, cols], axis=1, keepdims=True) * scale
        scores = [jnp.sum(kbuf[slot, h, r] * q, axis=1, keepdims=True) * scale for r in range(n_sel_pages)]
        top = s_new
        for s in scores:
            top = jnp.maximum(top, jnp.max(s, axis=0, keepdims=True))
        p_new = jnp.exp(s_new - top)
        denom = p_new
        acc = p_new * vn_row[:, cols]
        for r, s in enumerate(scores):
            p = jnp.exp(s - top)
            denom = denom + jnp.sum(p, axis=0, keepdims=True)
            acc = acc + jnp.sum(p * vbuf[slot, h, r], axis=0, keepdims=True)
        out.append(acc / denom)
    o_ref[row, :] = jnp.concatenate(out, axis=1)


def _decode_attn(q_s, k_s, v_s, cache_k, cache_v, layer, phys_pages, *, n_heads, hd):
    bs = q_s.shape[0]
    page = cache_k.shape[2]
    nsp = phys_pages.shape[-1]
    rows = pl.BlockSpec((bs, n_heads * hd), lambda b, ph: (0, 0))
    anywhere = pl.BlockSpec(memory_space=pl.ANY)
    return pl.pallas_call(
        functools.partial(_decode_attn_body, layer=layer, n_heads=n_heads, hd=hd, n_sel_pages=nsp,
                          scale=hd ** -0.5),
        grid_spec=pltpu.PrefetchScalarGridSpec(
            num_scalar_prefetch=1,
            grid=(bs,),
            in_specs=[rows, rows, rows, anywhere, anywhere],
            out_specs=rows,
            scratch_shapes=[pltpu.VMEM((2, n_heads, nsp, page, hd), F32),
                            pltpu.VMEM((2, n_heads, nsp, page, hd), F32),
                            pltpu.SemaphoreType.DMA((2,))],
        ),
        out_shape=jax.ShapeDtypeStruct((bs, n_heads * hd), F32),
        compiler_params=_params(("arbitrary",), 40),
        name="moba_decode",
    )(phys_pages.reshape(-1), q_s, k_s, v_s, cache_k, cache_v)


def _gelu_tanh(x):
    return 0.5 * x * (1.0 + jnp.tanh(np.sqrt(2.0 / np.pi).astype(np.float32) * (x + 0.044715 * (x * x * x))))


def _softplus(x):
    return jnp.maximum(x, 0.0) + jnp.log1p(jnp.exp(-jnp.abs(x)))


def _lru_gates(uc, wa_ref, wx_ref, ba, bx, lam, n_heads, rb):
    ucb = uc.astype(BF16)
    a_parts, b_parts = [], []
    sp = _softplus(-lam)
    for h in range(n_heads):
        cols = slice(h * rb, (h + 1) * rb)
        r = jax.nn.sigmoid(jnp.dot(ucb[:, cols], wa_ref[h], preferred_element_type=F32) + ba[:, cols])
        g = jax.nn.sigmoid(jnp.dot(ucb[:, cols], wx_ref[h], preferred_element_type=F32) + bx[:, cols])
        a = jnp.exp(-LRU_C * r * sp[:, cols])
        a_parts.append(a)
        b_parts.append(jnp.sqrt(1.0 - a * a) * (g * uc[:, cols]))
    return jnp.concatenate(a_parts, axis=1), jnp.concatenate(b_parts, axis=1)


def _lru_body(g_ref, u_ref, wc_ref, bc_ref, wa_ref, wx_ref, ba_ref, bx_ref, lam_ref,
              c0_ref, c1_ref, c2_ref, h0_ref,
              y_ref, newc_ref, newh_ref, us_ref, hs_ref,
              carry_ref, hcarry_ref, a_buf, b_buf, wab_ref, wxb_ref,
              *, tiles_per_seq, n_prompt_tiles, bs, n_heads, rb):
    i = pl.program_id(0)
    tm = y_ref.shape[0]
    w0, w1, w2, w3 = wc_ref[0:1, :], wc_ref[1:2, :], wc_ref[2:3, :], wc_ref[3:4, :]
    bias = bc_ref[...]
    @pl.when(pl.program_id(0) == 0)
    def _cast_gate_weights():
        wab_ref[...] = wa_ref[...].astype(BF16)
        wxb_ref[...] = wx_ref[...].astype(BF16)

    gates = functools.partial(_lru_gates, wa_ref=wab_ref, wx_ref=wxb_ref, ba=ba_ref[...], bx=bx_ref[...],
                              lam=lam_ref[...], n_heads=n_heads, rb=rb)

    @pl.when(i < n_prompt_tiles)
    def _prompt():
        u = u_ref[...]
        fresh = i % tiles_per_seq == 0
        c0, c1, c2 = (jnp.where(fresh, 0.0, carry_ref[r:r + 1, :]) for r in range(3))
        p1 = _shift_rows(u, 1, (c2,))
        p2 = _shift_rows(u, 2, (c1, c2))
        p3 = _shift_rows(u, 3, (c0, c1, c2))
        uc = w0 * p3 + w1 * p2 + w2 * p1 + w3 * u + bias
        a, b = gates(uc)
        a_buf[...] = a
        b_buf[...] = b
        sub = lax.broadcasted_iota(I32, (SUBLANES, a.shape[1]), 0)

        def group(gi, h):
            rows = pl.ds(pl.multiple_of(gi * SUBLANES, SUBLANES), SUBLANES)
            ag, bg = a_buf[rows, :], b_buf[rows, :]
            for d in (1, 2, 4):
                bg = bg + ag * jnp.where(sub >= d, pltpu.roll(bg, d, 0), 0.0)
                ag = ag * jnp.where(sub >= d, pltpu.roll(ag, d, 0), 1.0)
            hg = ag * h + bg
            b_buf[rows, :] = hg
            return hg[SUBLANES - 1:SUBLANES, :]

        h_last = lax.fori_loop(0, tm // SUBLANES, group, jnp.where(fresh, 0.0, hcarry_ref[0:1, :]))
        hcarry_ref[0:1, :] = h_last
        y_ref[...] = (b_buf[...] * _gelu_tanh(g_ref[...])).astype(y_ref.dtype)
        carry_ref[0:3, :] = u[tm - 3:tm, :]
        newc_ref[...] = u[tm - 3:tm, :]
        newh_ref[...] = h_last

    @pl.when(i == n_prompt_tiles)
    def _sample():
        u = u_ref[0:bs, :]
        uc = w0 * c0_ref[...] + w1 * c1_ref[...] + w2 * c2_ref[...] + w3 * u + bias
        a, b = gates(uc)
        h = a * h0_ref[...] + b
        y_ref[...] = jnp.zeros(y_ref.shape, y_ref.dtype)
        y_ref[0:bs, :] = (h * _gelu_tanh(g_ref[0:bs, :])).astype(y_ref.dtype)
        us_ref[...] = u
        hs_ref[...] = h


def _rglru(proj, w_conv, b_conv, w_gate_a, b_gate_a, w_gate_x, b_gate_x, lam, layer, conv_state, h0,
           *, bp, seq, bs):
    m_tot = proj.shape[0]
    dr = proj.shape[1] // 2
    n_heads, rb = w_gate_a.shape[1], w_gate_a.shape[2]
    tm = TOKEN_TILE
    tps = seq // tm
    npt = bp * tps
    width = w_conv.shape[1]
    col = lambda c: pl.BlockSpec((tm, dr), lambda i: (i, c))
    vec = pl.BlockSpec((None, 1, dr), lambda i: (layer, 0, 0))
    gw = pl.BlockSpec((None, n_heads, rb, rb), lambda i: (layer, 0, 0, 0))
    st = pl.BlockSpec((bs, dr), lambda i: (0, 0))
    pb = lambda rows: pl.BlockSpec((None, rows, dr), lambda i: (jnp.minimum(i // tps, bp - 1), 0, 0))
    return pl.pallas_call(
        functools.partial(_lru_body, tiles_per_seq=tps, n_prompt_tiles=npt, bs=bs, n_heads=n_heads, rb=rb),
        grid=(m_tot // tm,),
        in_specs=[col(0), col(1), pl.BlockSpec((None, width, dr), lambda i: (layer, 0, 0)), vec,
                  gw, gw, vec, vec, vec, st, st, st, st],
        out_specs=[pl.BlockSpec((tm, dr), lambda i: (i, 0)), pb(width - 1), pb(1), st, st],
        out_shape=[jax.ShapeDtypeStruct((m_tot, dr), BF16),
                   jax.ShapeDtypeStruct((bp, width - 1, dr), F32),
                   jax.ShapeDtypeStruct((bp, 1, dr), F32),
                   jax.ShapeDtypeStruct((bs, dr), F32),
                   jax.ShapeDtypeStruct((bs, dr), F32)],
        scratch_shapes=[pltpu.VMEM((SUBLANES, dr), F32), pltpu.VMEM((SUBLANES, dr), F32),
                        pltpu.VMEM((tm, dr), F32), pltpu.VMEM((tm, dr), F32),
                        pltpu.VMEM((n_heads, rb, rb), BF16), pltpu.VMEM((n_heads, rb, rb), BF16)],
        compiler_params=_params(("arbitrary",), 48),
        name="rglru",
    )(proj, proj, w_conv, b_conv.reshape(-1, 1, dr), w_gate_a, w_gate_x,
      b_gate_a.reshape(-1, 1, dr), b_gate_x.reshape(-1, 1, dr), lam.reshape(-1, 1, dr),
      conv_state[:, 0], conv_state[:, 1], conv_state[:, 2], h0)


def _ffn_body(e_ref, start_ref, nt_ref, nused_ref, xs_hbm, wg_ref, wu_ref, wd_ref, o_hbm,
              xbuf, acc, wgb, wub, wdb, sem_in, sem_out, *, tm, nf):
    del e_ref
    s, f = pl.program_id(0), pl.program_id(1)
    nt = nt_ref[s]
    start = start_ref[s]

    def copy_in(r):
        return pltpu.make_async_copy(xs_hbm.at[pl.ds(pl.multiple_of(start + r * tm, tm), tm), :],
                                     xbuf.at[pl.ds(pl.multiple_of(r * tm, tm), tm), :], sem_in)

    def copy_out(r):
        return pltpu.make_async_copy(acc.at[pl.ds(pl.multiple_of(r * tm, tm), tm), :],
                                     o_hbm.at[pl.ds(pl.multiple_of(start + r * tm, tm), tm), :], sem_out)

    def for_tiles(fn):
        def body(r, c):
            fn(r)
            return c
        lax.fori_loop(0, nt, body, 0)

    @pl.when(nt > 0)
    def _run():
        @pl.when(f == 0)
        def _load():
            for_tiles(lambda r: copy_in(r).start())

            def clear(r):
                acc[pl.ds(pl.multiple_of(r * tm, tm), tm), :] = jnp.zeros((tm, acc.shape[1]), F32)

            for_tiles(clear)
            for_tiles(lambda r: copy_in(r).wait())

        def rows_step(first_tile, n_tiles, cast_weights=False):
            rows = pl.ds(pl.multiple_of(first_tile * tm, tm), n_tiles * tm)
            x = xbuf[rows, :]
            if cast_weights:
                wgb[...] = wg_ref[...].astype(BF16)
            g = jnp.dot(x, wgb[...], preferred_element_type=F32)
            if cast_weights:
                wub[...] = wu_ref[...].astype(BF16)
            u = jnp.dot(x, wub[...], preferred_element_type=F32)
            h = (g * jax.nn.sigmoid(g) * u).astype(BF16)
            if cast_weights:
                wdb[...] = wd_ref[...].astype(BF16)
            acc[rows, :] += jnp.dot(h, wdb[...], preferred_element_type=F32)

            @pl.when(f == nf - 1)
            def _store():
                for r in range(n_tiles):
                    copy_out(first_tile + r).start()

        big = FFN_STEP_TILES
        n_big = nt // big
        rem = nt % big

        @pl.when(n_big > 0)
        def _first_big():
            rows_step(0, big, cast_weights=True)

        def big_step(i, c):
            rows_step(big * i, big)
            return c

        lax.fori_loop(1, n_big, big_step, 0)

        size = big // 2
        while size >= 1:
            present = (rem // size) % 2 == 1
            first = (n_big == 0) & (rem < 2 * size)
            offset = n_big * big + rem - rem % (2 * size)

            @pl.when(present & first)
            def _piece_first(size=size):
                rows_step(0, size, cast_weights=True)

            @pl.when(present & jnp.logical_not(first))
            def _piece(size=size, offset=offset):
                rows_step(offset, size)

            size //= 2

        @pl.when(f == nf - 1)
        def _drain_stores():
            for_tiles(lambda r: copy_out(r).wait())

    @pl.when((s == pl.num_programs(0) - 1) & (f == nf - 1))
    def _zero_unused():
        acc[0:tm, :] = jnp.zeros((tm, acc.shape[1]), F32)

        def zero_tile(t, c):
            cp = pltpu.make_async_copy(acc.at[0:tm, :], o_hbm.at[pl.ds(pl.multiple_of(t * tm, tm), tm), :], sem_out)
            cp.start()
            cp.wait()
            return c

        lax.fori_loop(nused_ref[0], o_hbm.shape[0] // tm, zero_tile, 0)


def _ffn(xs, w_gate, w_up, w_down, layer, sched_e, sched_start, sched_nt, n_used_tiles):
    s_rows, d = xs.shape
    dff = w_gate.shape[3]
    tm, tf, rt = TOKEN_TILE, FFN_COL_TILE, FFN_SUPER_TILES
    nf = dff // tf
    n_super = sched_e.shape[0]

    def fcol(s, f, nt_ref):
        return jnp.where(nt_ref[s] > 0, f, nf - 1)

    up_spec = pl.BlockSpec((None, None, d, tf), lambda s, f, e, st, nt, nu: (layer, e[s], 0, fcol(s, f, nt)))
    down_spec = pl.BlockSpec((None, None, tf, d), lambda s, f, e, st, nt, nu: (layer, e[s], fcol(s, f, nt), 0))
    return pl.pallas_call(
        functools.partial(_ffn_body, tm=tm, nf=nf),
        grid_spec=pltpu.PrefetchScalarGridSpec(
            num_scalar_prefetch=4,
            grid=(n_super, nf),
            in_specs=[pl.BlockSpec(memory_space=pl.ANY), up_spec, up_spec, down_spec],
            out_specs=pl.BlockSpec(memory_space=pl.ANY),
            scratch_shapes=[pltpu.VMEM((rt * tm, d), BF16), pltpu.VMEM((rt * tm, d), F32),
                            pltpu.VMEM((d, tf), BF16), pltpu.VMEM((d, tf), BF16), pltpu.VMEM((tf, d), BF16),
                            pltpu.SemaphoreType.DMA(()), pltpu.SemaphoreType.DMA(())],
        ),
        out_shape=jax.ShapeDtypeStruct((s_rows, d), F32),
        compiler_params=_params(("arbitrary", "arbitrary"), 56),
        name="swiglu_ffn",
    )(sched_e, sched_start, sched_nt, n_used_tiles, xs, w_gate, w_up, w_down)


def _dense_schedule(n_tiles):
    rt = FFN_SUPER_TILES
    n_super = -(-n_tiles // rt)
    start = np.arange(n_super, dtype=np.int32) * rt
    nt = np.minimum(rt, n_tiles - start).astype(np.int32)
    return (jnp.zeros((n_super,), I32), jnp.asarray(start * TOKEN_TILE, I32), jnp.asarray(nt, I32),
            jnp.full((1,), n_tiles, I32))


def _slot_token_body(p1_ref, p2_ref, o_ref, *, n_valid, n_slots):
    clear_unroll = 4 * DMA_ISSUE_UNROLL

    def clear(g, c):
        for r in range(clear_unroll):
            o_ref[g * clear_unroll + r] = 0
        return c

    def place(t):
        o_ref[p1_ref[t]] = t
        o_ref[p2_ref[t]] = t

    def place_group(g, c):
        for r in range(DMA_ISSUE_UNROLL):
            place(g * DMA_ISSUE_UNROLL + r)
        return c

    def place_one(t, c):
        place(t)
        return c

    assert n_slots % clear_unroll == 0
    lax.fori_loop(0, n_slots // clear_unroll, clear, 0)
    n_groups = n_valid // DMA_ISSUE_UNROLL
    lax.fori_loop(0, n_groups, place_group, 0)
    lax.fori_loop(n_groups * DMA_ISSUE_UNROLL, n_valid, place_one, 0)


def _slot_tokens(pos1, pos2, n_valid, n_slots):
    smem = pl.BlockSpec(memory_space=pltpu.SMEM)
    return pl.pallas_call(
        functools.partial(_slot_token_body, n_valid=n_valid, n_slots=n_slots),
        in_specs=[smem, smem],
        out_specs=smem,
        out_shape=jax.ShapeDtypeStruct((n_slots,), I32),
        name="moe_slot_tokens",
    )(pos1, pos2)


def _moe_plan(route, n_valid, n_exp):
    m_tot = route.shape[0]
    tm, rt = TOKEN_TILE, FFN_SUPER_TILES
    n_tiles_max = (MOE_TOP_K * n_valid + n_exp * (tm - 1)) // tm
    s_rows = n_tiles_max * tm
    n_super = n_tiles_max // rt + n_exp
    e1, e2 = route[:, 0].astype(I32), route[:, 1].astype(I32)
    tok = jnp.arange(m_tot, dtype=I32)
    valid = tok < n_valid
    ids = jnp.arange(n_exp, dtype=I32)[None, :]
    oh1 = (e1[:, None] == ids) & valid[:, None]
    oh2 = (e2[:, None] == ids) & valid[:, None]
    oh = oh1.astype(I32) + oh2.astype(I32)
    csum = jnp.cumsum(oh, axis=0)
    before = csum - oh
    counts = csum[-1]
    tiles_e = (counts + tm - 1) // tm
    tile_end = jnp.cumsum(tiles_e)
    tile_off = tile_end - tiles_e
    row_off = tile_off * tm
    pos1 = jnp.sum(jnp.where(oh1, before + row_off[None, :], 0), axis=1)
    pos2 = jnp.sum(jnp.where(oh2, before + row_off[None, :], 0), axis=1)
    slot_token = _slot_tokens(pos1, pos2, n_valid, s_rows)
    n_used_tiles = tile_end[-1:]
    super_e = -(-tiles_e // rt)
    super_end = jnp.cumsum(super_e)
    super_off = super_end - super_e
    n_used_super = super_end[-1]
    sidx = jnp.arange(n_super, dtype=I32)
    live = sidx < n_used_super
    sclamp = jnp.minimum(sidx, n_used_super - 1)
    exp_of = jnp.sum((sclamp[:, None] >= super_end[None, :]).astype(I32), axis=1)
    k = sclamp - super_off[exp_of]
    sched_start = (tile_off[exp_of] + k * rt) * tm
    sched_nt = jnp.where(live, jnp.clip(tiles_e[exp_of] - k * rt, 0, rt), 0)
    return pos1, pos2, slot_token, n_used_tiles, (exp_of, sched_start, sched_nt), s_rows


def _gather_body(tok_ref, nused_ref, x_hbm, o_ref, buf, sem, *, tm):
    t = pl.program_id(0)
    n_used = nused_ref[0]

    def issue_tile(tile, slot):
        def issue(g, c):
            for r in range(DMA_ISSUE_UNROLL):
                row = g * DMA_ISSUE_UNROLL + r
                pltpu.make_async_copy(x_hbm.at[pl.ds(tok_ref[tile * tm + row], 1), :],
                                      buf.at[slot, pl.ds(row, 1), :], sem.at[slot]).start()
            return c

        lax.fori_loop(0, tm // DMA_ISSUE_UNROLL, issue, 0)

    @pl.when((t == 0) & (n_used > 0))
    def _first():
        issue_tile(0, 0)

    @pl.when(t + 1 < n_used)
    def _prefetch():
        issue_tile(t + 1, (t + 1) % 2)

    @pl.when(t < n_used)
    def _used():
        slot = t % 2
        pltpu.make_async_copy(x_hbm.at[pl.ds(0, tm), :], buf.at[slot], sem.at[slot]).wait()
        o_ref[...] = buf[slot].astype(o_ref.dtype)

    @pl.when(t >= n_used)
    def _unused():
        o_ref[...] = jnp.zeros(o_ref.shape, o_ref.dtype)


def _gather_rows(x, slot_token, n_used_tiles):
    s_rows = slot_token.shape[0]
    d = x.shape[1]
    tm = TOKEN_TILE
    return pl.pallas_call(
        functools.partial(_gather_body, tm=tm),
        grid_spec=pltpu.PrefetchScalarGridSpec(
            num_scalar_prefetch=2,
            grid=(s_rows // tm,),
            in_specs=[pl.BlockSpec(memory_space=pl.ANY)],
            out_specs=pl.BlockSpec((tm, d), lambda t, tok, nu: (t, 0)),
            scratch_shapes=[pltpu.VMEM((2, tm, d), F32), pltpu.SemaphoreType.DMA((2,))],
        ),
        out_shape=jax.ShapeDtypeStruct((s_rows, d), BF16),
        compiler_params=_params(("arbitrary",), 32),
        name="moe_gather",
    )(slot_token, n_used_tiles, x)


def _combine_body(p1_ref, p2_ref, x_ref, route_ref, g_ref, b_ref, y_hbm, o_ref, ob_ref, buf, sem,
                  *, tm, alpha, n_prompt_tiles):
    t = pl.program_id(0)

    def issue_tile(tile, slot):
        def issue(g, c):
            for r in range(DMA_ISSUE_UNROLL):
                row = g * DMA_ISSUE_UNROLL + r
                for which, p_ref in enumerate((p1_ref, p2_ref)):
                    pltpu.make_async_copy(y_hbm.at[pl.ds(p_ref[tile * tm + row], 1), :],
                                          buf.at[slot, which, pl.ds(row, 1), :], sem.at[slot]).start()
            return c

        lax.fori_loop(0, tm // DMA_ISSUE_UNROLL, issue, 0)

    @pl.when(t == 0)
    def _first():
        issue_tile(0, 0)

    @pl.when(t + 1 < pl.num_programs(0))
    def _prefetch():
        issue_tile(t + 1, (t + 1) % 2)

    slot = t % 2
    for which in range(2):
        pltpu.make_async_copy(y_hbm.at[pl.ds(0, tm), :], buf.at[slot, which], sem.at[slot]).wait()
    route = route_ref[...]
    y = route[:, 2:3] * buf[slot, 0] + route[:, 3:4] * buf[slot, 1]
    out = _layer_norm(alpha * x_ref[...] + y, g_ref[...], b_ref[...])
    if n_prompt_tiles is None:
        o_ref[...] = out
        ob_ref[...] = out.astype(BF16)
    else:
        @pl.when(t < n_prompt_tiles)
        def _prompt():
            o_ref[...] = out

        @pl.when(t == n_prompt_tiles)
        def _sample():
            ob_ref[...] = out


def _combine_ln(x, route, y_slots, pos1, pos2, g_stack, b_stack, layer, alpha, n_prompt_rows=None):
    m, d = x.shape
    tm = TOKEN_TILE
    row = pl.BlockSpec((tm, d), lambda t, p1, p2: (t, 0))
    par = pl.BlockSpec((None, 1, d), lambda t, p1, p2: (layer, 0, 0))
    if n_prompt_rows is None:
        npt = None
        out_specs = [row, row]
        out_shape = [jax.ShapeDtypeStruct((m, d), F32), jax.ShapeDtypeStruct((m, d), BF16)]
    else:
        npt = n_prompt_rows // tm
        assert n_prompt_rows % tm == 0 and m == n_prompt_rows + tm
        out_specs = [pl.BlockSpec((tm, d), lambda t, p1, p2: (jnp.minimum(t, npt - 1), 0)),
                     pl.BlockSpec((tm, d), lambda t, p1, p2: (0, 0))]
        out_shape = [jax.ShapeDtypeStruct((n_prompt_rows, d), F32), jax.ShapeDtypeStruct((tm, d), F32)]
    return pl.pallas_call(
        functools.partial(_combine_body, tm=tm, alpha=alpha, n_prompt_tiles=npt),
        grid_spec=pltpu.PrefetchScalarGridSpec(
            num_scalar_prefetch=2,
            grid=(m // tm,),
            in_specs=[row, pl.BlockSpec((tm, LANES), lambda t, p1, p2: (t, 0)), par, par,
                      pl.BlockSpec(memory_space=pl.ANY)],
            out_specs=out_specs,
            scratch_shapes=[pltpu.VMEM((2, 2, tm, d), F32), pltpu.SemaphoreType.DMA((2,))],
        ),
        out_shape=out_shape,
        compiler_params=_params(("arbitrary",), 40),
        name="moe_combine_ln",
    )(pos1, pos2, x, route, g_stack.reshape(-1, 1, d), b_stack.reshape(-1, 1, d), y_slots)


def kernel(x_prompt, x_sample, cache_k, cache_v, page_table, state_conv, state_lru_conv, state_lru_h, ln1_g, ln1_b, ln2_g, ln2_b, sc_w_in, sc_w_conv, sc_w_out, attn_w_qkv, attn_w_o, lru_w_in, lru_w_conv, lru_b_conv, lru_w_gate_a, lru_b_gate_a, lru_w_gate_x, lru_b_gate_x, lru_lambda, lru_w_out, ffn_w_gate, ffn_w_up, ffn_w_down, moe_w_router, moe_w_gate, moe_w_up, moe_w_down):
    bp, seq, d = x_prompt.shape
    bs, dec_seq, _ = x_sample.shape
    depth = ln1_g.shape[0]
    n_heads, hd = cache_k.shape[3], cache_k.shape[4]
    page = cache_k.shape[2]
    past_len = page_table.shape[1] * page
    n_exp = moe_w_router.shape[2]
    assert dec_seq == 1 and bs <= TOKEN_TILE
    assert seq % MOBA_BLOCK == 0 and seq % TOKEN_TILE == 0 and past_len % MOBA_BLOCK == 0
    assert MOBA_BLOCK % page == 0 and past_len // MOBA_BLOCK >= MOBA_TOPK
    alpha = float((2.0 * depth) ** 0.25)
    mp = bp * seq
    n_valid = mp + bs
    m_tot = -(-n_valid // TOKEN_TILE) * TOKEN_TILE
    ppb = MOBA_BLOCK // page

    x = jnp.concatenate([x_prompt.reshape(mp, d), x_sample.reshape(bs, d),
                         jnp.zeros((m_tot - n_valid, d), x_prompt.dtype)], axis=0)
    xb = x.astype(BF16)
    dense_sched = _dense_schedule(m_tot // TOKEN_TILE)

    conv_p, conv_s, kp_l, vp_l, ks_l, vs_l, lc_p, lc_s, lh_p, lh_s = ([] for _ in range(10))
    final = None
    for i in range(depth):
        kind, m = i % 3, i // 3
        if kind == 0:
            proj = _matmul(xb, sc_w_in, m)
            y, new_p, u_s = _short_conv(proj, sc_w_conv, m, state_conv[m, :, 0], state_conv[m, :, 1],
                                        bp=bp, seq=seq, bs=bs)
            conv_p.append(new_p)
            conv_s.append(jnp.stack([state_conv[m, :, 1], u_s], axis=1))
            mixed, w_out = y, sc_w_out
        elif kind == 1:
            qkv = _matmul(xb, attn_w_qkv, m)
            q, k_p, v_p, k_s, v_s = _rope(qkv, n_heads=n_heads, hd=hd, bp=bp, seq=seq, bs=bs, past_len=past_len)
            kp_l.append(k_p.reshape(bp, seq, n_heads, hd))
            vp_l.append(v_p.reshape(bp, seq, n_heads, hd))
            ks_l.append(k_s.reshape(bs, 1, n_heads, hd))
            vs_l.append(v_s.reshape(bs, 1, n_heads, hd))
            o_p = _attn_prompt(q, k_p, v_p, bp=bp, seq=seq, n_heads=n_heads, hd=hd)
            q_s = q[mp:mp + bs]
            kmean = _page_means(cache_k, m, page_table)
            sel = _select_blocks(kmean.transpose(0, 2, 1, 3), q_s)
            sel = sel[:, :MOBA_TOPK, :n_heads].transpose(0, 2, 1)
            logical = sel[..., None] * ppb + jnp.arange(ppb, dtype=I32)
            phys = jnp.take_along_axis(page_table, logical.reshape(bs, -1), axis=1)
            o_s = _decode_attn(q_s, k_s, v_s, cache_k, cache_v, m, phys.reshape(bs, n_heads, -1),
                               n_heads=n_heads, hd=hd)
            tail = jnp.concatenate([o_s.astype(BF16), jnp.zeros((m_tot - n_valid, n_heads * hd), BF16)], axis=0)
            mixed, w_out = o_p, attn_w_o
        else:
            proj = _matmul(xb, lru_w_in, m)
            y, new_c, new_h, u_s, h_s = _rglru(proj, lru_w_conv, lru_b_conv, lru_w_gate_a, lru_b_gate_a,
                                               lru_w_gate_x, lru_b_gate_x, lru_lambda, m,
                                               state_lru_conv[m], state_lru_h[m], bp=bp, seq=seq, bs=bs)
            lc_p.append(new_c)
            lc_s.append(jnp.concatenate([state_lru_conv[m, :, 1:], u_s[:, None, :]], axis=1))
            lh_p.append(new_h.reshape(bp, -1))
            lh_s.append(h_s)
            mixed, w_out = y, lru_w_out
        y_tail = tail if kind == 1 else None
        if i % 2 == 0:
            x, xb = _matmul_add_ln(mixed, w_out, m, x, ln1_g, ln1_b, i, alpha, y_tail=y_tail)
        else:
            x, xb, route = _matmul_add_ln(mixed, w_out, m, x, ln1_g, ln1_b, i, alpha,
                                          w_router=moe_w_router, router_layer=i // 2, y_tail=y_tail)
        j = i // 2
        if i % 2 == 0:
            f = _ffn(xb, ffn_w_gate[:, None], ffn_w_up[:, None], ffn_w_down[:, None], j, *dense_sched)
            x, xb = _add_ln(x, f, ln2_g, ln2_b, i, alpha)
        else:
            pos1, pos2, slot_token, n_used, sched, _ = _moe_plan(route, n_valid, n_exp)
            xs = _gather_rows(x, slot_token, n_used)
            ys = _ffn(xs, moe_w_gate, moe_w_up, moe_w_down, j, *sched, n_used)
            if i == depth - 1 and m_tot == mp + TOKEN_TILE:
                final = _combine_ln(x, route, ys, pos1, pos2, ln2_g, ln2_b, i, alpha, n_prompt_rows=mp)
            else:
                x, xb = _combine_ln(x, route, ys, pos1, pos2, ln2_g, ln2_b, i, alpha)
    x_p, x_s = final if final is not None else (x[:mp], x[mp:])
    return (x_p.reshape(bp, seq, d), x_s[:bs].reshape(bs, 1, d),
            jnp.stack(conv_p), jnp.stack(conv_s),
            jnp.stack(kp_l), jnp.stack(vp_l), jnp.stack(ks_l), jnp.stack(vs_l),
            jnp.stack(lc_p), jnp.stack(lc_s), jnp.stack(lh_p), jnp.stack(lh_s))
```
